```python
import math
import jax, jax.numpy as jnp
from jax import lax
import numpy as np

D_MODEL = 1024
BATCH = 16
SEQ = 2048
DEPTH = 1

CHUNK = 64
Q_BLOCK = 128
FOX_HEADS = 8
FOX_HEAD_DIM = 64
FOX_WIDTH = FOX_HEADS * FOX_HEAD_DIM
MLSTM_HEADS = 4
MLSTM_INNER = D_MODEL
MLSTM_V_DIM = MLSTM_INNER // MLSTM_HEADS
MLSTM_QK_DIM = MLSTM_V_DIM // 2
CONV_WIDTH = 4
D_FF = -(-(8 * D_MODEL) // (3 * 256)) * 256
FOX_FORGET_BIAS = 3.0
MLSTM_FORGET_BIAS = 3.0
LN_EPS = 1e-5
IN_SPLITS = (FOX_WIDTH, FOX_WIDTH, FOX_WIDTH, FOX_HEADS,
             MLSTM_INNER, MLSTM_INNER, MLSTM_HEADS, MLSTM_HEADS, MLSTM_INNER,
             D_MODEL, D_MODEL)
D_IN = sum(IN_SPLITS)

kernel_name = "hybrid_fox_mlstm_adaln_deepnorm_block"


def _ln(x, g=None, b=None):
    xf = x.astype(jnp.float32)
    mu = jnp.mean(xf, axis=-1, keepdims=True)
    var = jnp.mean(jnp.square(xf - mu), axis=-1, keepdims=True)
    y = ((xf - mu) * lax.rsqrt(var + LN_EPS)).astype(x.dtype)
    if g is not None:
        y = y * g + b
    return y


def _forgetting_attention(q, k, v, f_logit):
    B, S, H, d = q.shape
    log_f = jax.nn.log_sigmoid(f_logit.astype(jnp.float32))
    F = jnp.cumsum(log_f, axis=1).transpose(0, 2, 1)
    q = q.transpose(0, 2, 1, 3)
    k = k.transpose(0, 2, 1, 3)
    v = v.transpose(0, 2, 1, 3)
    scale = d ** -0.5
    outs = []
    for start in range(0, S, Q_BLOCK):
        end = start + Q_BLOCK
        logits = jnp.einsum('bhqd,bhkd->bhqk', q[:, :, start:end], k[:, :, :end]).astype(jnp.float32) * scale
        logits = logits + F[:, :, start:end, None] - F[:, :, None, :end]
        q_pos = jnp.arange(start, end)[:, None]
        k_pos = jnp.arange(end)[None, :]
        logits = jnp.where(k_pos <= q_pos, logits, -jnp.inf)
        p = jax.nn.softmax(logits, axis=-1).astype(v.dtype)
        outs.append(jnp.einsum('bhqk,bhkd->bhqd', p, v[:, :, :end]))
    o = jnp.concatenate(outs, axis=2)
    return o.transpose(0, 2, 1, 3).reshape(B, S, H * d)


def _mlstm_chunkwise(q, k, v, i_pre, f_pre):
    out_dtype = v.dtype
    B, S, H, dk = q.shape
    dv = v.shape[-1]
    NC, L = S // CHUNK, CHUNK
    f32 = jnp.float32
    q = (q.astype(f32) * dk ** -0.5).reshape(B, NC, L, H, dk).transpose(0, 3, 1, 2, 4)
    k = k.astype(f32).reshape(B, NC, L, H, dk).transpose(0, 3, 1, 2, 4)
    v = v.astype(f32).reshape(B, NC, L, H, dv).transpose(0, 3, 1, 2, 4)
    ig = i_pre.astype(f32).reshape(B, NC, L, H).transpose(0, 3, 1, 2)
    lf = jax.nn.log_sigmoid(f_pre.astype(f32)).reshape(B, NC, L, H).transpose(0, 3, 1, 2)
    b = jnp.cumsum(lf, axis=-1)
    g = b[..., -1]
    a = g[..., None] - b + ig

    def step(carry, xs):
        C, n, m = carry
        k_c, v_c, a_c, g_c = xs
        m_new = jnp.maximum(g_c + m, jnp.max(a_c, axis=-1))
        decay = jnp.exp(g_c + m - m_new)
        w = jnp.exp(a_c - m_new[..., None])
        C_new = decay[..., None, None] * C + jnp.einsum('bhl,bhlk,bhlv->bhkv', w, k_c, v_c)
        n_new = decay[..., None] * n + jnp.einsum('bhl,bhlk->bhk', w, k_c)
        return (C_new, n_new, m_new), (C, n, m)

    init = (jnp.zeros((B, H, dk, dv), f32), jnp.zeros((B, H, dk), f32), jnp.zeros((B, H), f32))
    xs = (k.transpose(2, 0, 1, 3, 4), v.transpose(2, 0, 1, 3, 4),
          a.transpose(2, 0, 1, 3), g.transpose(2, 0, 1))
    _, (C_prev, n_prev, m_prev) = lax.scan(step, init, xs)
    C_prev = C_prev.transpose(1, 2, 0, 3, 4)
    n_prev = n_prev.transpose(1, 2, 0, 3)
    m_prev = m_prev.transpose(1, 2, 0)

    causal = jnp.tril(jnp.ones((L, L), dtype=bool))
    D = jnp.where(causal, b[..., :, None] - b[..., None, :] + ig[..., None, :], -jnp.inf)
    inter = b + m_prev[..., None]
    m_t = jnp.maximum(inter, jnp.max(D, axis=-1))
    scores = jnp.einsum('bhcld,bhcsd->bhcls', q, k) * jnp.exp(D - m_t[..., None])
    w_inter = jnp.exp(inter - m_t)
    num = (w_inter[..., None] * jnp.einsum('bhcld,bhcdv->bhclv', q, C_prev)
           + jnp.einsum('bhcls,bhcsv->bhclv', scores, v))
    den = w_inter * jnp.einsum('bhcld,bhcd->bhcl', q, n_prev) + jnp.sum(scores, axis=-1)
    h = num / jnp.maximum(jnp.abs(den), jnp.exp(-m_t))[..., None]
    return h.transpose(0, 2, 3, 1, 4).reshape(B, S, H * dv).astype(out_dtype)


def _causal_depthwise_conv(u, w, b):
    out = lax.conv_general_dilated(u, w[:, None, :], window_strides=(1,),
                                   padding=[(CONV_WIDTH - 1, 0)],
                                   dimension_numbers=('NWC', 'WIO', 'NWC'),
                                   feature_group_count=u.shape[-1])
    return out + b


def _head_norm(h, g, n_heads):
    B, S, W = h.shape
    return _ln(h.reshape(B, S, n_heads, W // n_heads)).reshape(B, S, W) * g


def _hybrid_mixer(h, w_in, b_in, conv_w, conv_b, w_mq, w_mk, mh_norm_g, w_pa, w_pb, w_out):
    B, S, _ = h.shape
    proj = jnp.einsum('bsd,de->bse', h, w_in) + b_in
    fq, fk, fv, ff, mu, mv, mi, mf, mo, ga, gb = jnp.split(
        proj, np.cumsum(IN_SPLITS)[:-1].tolist(), axis=-1)
    shp = (B, S, FOX_HEADS, FOX_HEAD_DIM)
    att = _forgetting_attention(fq.reshape(shp), fk.reshape(shp), fv.reshape(shp), ff)
    u = jax.nn.silu(_causal_depthwise_conv(mu, conv_w, conv_b))
    uh = u.reshape(B, S, MLSTM_HEADS, MLSTM_V_DIM)
    mq = jnp.einsum('bshi,hik->bshk', uh, w_mq)
    mk = jnp.einsum('bshi,hik->bshk', uh, w_mk)
    hm = _mlstm_chunkwise(mq, mk, mv.reshape(B, S, MLSTM_HEADS, MLSTM_V_DIM), mi, mf)
    hm = _head_norm(hm, mh_norm_g, MLSTM_HEADS) * jax.nn.sigmoid(mo)
    y = (jax.nn.sigmoid(ga) * jnp.einsum('bsi,id->bsd', att, w_pa)
         + jax.nn.sigmoid(gb) * jnp.einsum('bsi,id->bsd', hm, w_pb))
    return jnp.einsum('bsd,de->bse', y, w_out)


def _swiglu(h, w_ffn_in, w_ffn_down):
    gate, up = jnp.split(jnp.einsum('bsd,df->bsf', h, w_ffn_in), 2, axis=-1)
    return jnp.einsum('bsf,fd->bsd', jax.nn.silu(gate) * up, w_ffn_down)


def setup_inputs(seed: int = 0) -> dict:
    key = jax.random.key(seed)
    ks = jax.random.split(key, 24)
    f32 = jnp.float32
    beta = (8.0 * DEPTH) ** -0.25
    nrm = lambda k, shape, s: jax.random.normal(k, shape, f32) * s
    ada_offset = jnp.concatenate([jnp.zeros((2 * D_MODEL,), f32), jnp.ones((D_MODEL,), f32),
                                  jnp.zeros((2 * D_MODEL,), f32), jnp.ones((D_MODEL,), f32)])
    parts = [jnp.zeros((n,), f32) for n in IN_SPLITS]
    parts[3] = jnp.full((FOX_HEADS,), FOX_FORGET_BIAS, f32)
    parts[7] = jnp.full((MLSTM_HEADS,), MLSTM_FORGET_BIAS, f32)
    in_offset = jnp.concatenate(parts)
    return {
        "x": nrm(ks[0], (BATCH, SEQ, D_MODEL), 1.0),
        "c": nrm(ks[1], (BATCH, D_MODEL), 1.0),
        "w_ada": nrm(ks[2], (DEPTH, D_MODEL, 6 * D_MODEL), 0.3 * D_MODEL ** -0.5),
        "b_ada": nrm(ks[3], (DEPTH, 6 * D_MODEL), 0.02) + ada_offset,
        "w_in": nrm(ks[4], (DEPTH, D_MODEL, D_IN), D_MODEL ** -0.5),
        "b_in": nrm(ks[5], (DEPTH, D_IN), 0.02) + in_offset,
        "conv_w": nrm(ks[6], (DEPTH, CONV_WIDTH, MLSTM_INNER), CONV_WIDTH ** -0.5),
        "conv_b": nrm(ks[7], (DEPTH, MLSTM_INNER), 0.02),
        "w_mq": nrm(ks[8], (DEPTH, MLSTM_HEADS, MLSTM_V_DIM, MLSTM_QK_DIM), MLSTM_V_DIM ** -0.5),
        "w_mk": nrm(ks[9], (DEPTH, MLSTM_HEADS, MLSTM_V_DIM, MLSTM_QK_DIM), MLSTM_V_DIM ** -0.5),
        "mh_norm_g": 1.0 + nrm(ks[10], (DEPTH, MLSTM_INNER), 0.02),
        "w_pa": nrm(ks[11], (DEPTH, FOX_WIDTH, D_MODEL), FOX_WIDTH ** -0.5),
        "w_pb": nrm(ks[12], (DEPTH, MLSTM_INNER, D_MODEL), MLSTM_INNER ** -0.5),
        "w_out": nrm(ks[13], (DEPTH, D_MODEL, D_MODEL), beta * D_MODEL ** -0.5),
        "ln1_g": 1.0 + nrm(ks[14], (DEPTH, D_MODEL), 0.02),
        "ln1_b": nrm(ks[15], (DEPTH, D_MODEL), 0.02),
        "w_ffn_in": nrm(ks[16], (DEPTH, D_MODEL, 2 * D_FF), D_MODEL ** -0.5),
        "w_ffn_down": nrm(ks[17], (DEPTH, D_FF, D_MODEL), beta * D_FF ** -0.5),
        "ln2_g": 1.0 + nrm(ks[18], (DEPTH, D_MODEL), 0.02),
        "ln2_b": nrm(ks[19], (DEPTH, D_MODEL), 0.02),
    }


def reference(x, c, w_ada, b_ada, w_in, b_in, conv_w, conv_b, w_mq, w_mk, mh_norm_g,
              w_pa, w_pb, w_out, ln1_g, ln1_b, w_ffn_in, w_ffn_down, ln2_g, ln2_b):
    alpha = (2.0 * DEPTH) ** 0.25
    for l in range(DEPTH):
        mod = jnp.einsum('bd,de->be', jax.nn.silu(c), w_ada[l]) + b_ada[l]
        sh1, sc1, g1, sh2, sc2, g2 = [m[:, None, :] for m in jnp.split(mod, 6, axis=-1)]
        h = _ln(x) * (1.0 + sc1) + sh1
        y = _hybrid_mixer(h, w_in[l], b_in[l], conv_w[l], conv_b[l], w_mq[l], w_mk[l],
                          mh_norm_g[l], w_pa[l], w_pb[l], w_out[l])
        x = _ln(alpha * x + g1 * y, ln1_g[l], ln1_b[l])
        h = _ln(x) * (1.0 + sc2) + sh2
        x = _ln(alpha * x + g2 * _swiglu(h, w_ffn_in[l], w_ffn_down[l]), ln2_g[l], ln2_b[l])
    return x
```

```python
import functools

import jax
import jax.numpy as jnp
from jax import lax
from jax.experimental import pallas as pl
from jax.experimental.pallas import tpu as pltpu

F32 = jnp.float32
BF16 = jnp.bfloat16

FOX_HEADS = 8
FOX_HEAD_DIM = 64
FOX_WIDTH = FOX_HEADS * FOX_HEAD_DIM
MLSTM_HEADS = 4
CONV_WIDTH = 4
LN_EPS = 1e-5
DEPTH = 1
ALPHA = (2.0 * DEPTH) ** 0.25

VMEM_LIMIT_BYTES = 56 * 1024 * 1024
SUBLANES = 8
GATE_ROWS = 16
VAUG_PAD = 16


def _const_spec(shape):
    nd = len(shape)
    return pl.BlockSpec(shape, lambda *_: (0,) * nd, pipeline_mode=pl.Buffered(1))


def _ln(v):
    mu = jnp.mean(v, axis=-1, keepdims=True)
    d = v - mu
    var = jnp.mean(d * d, axis=-1, keepdims=True)
    return d * lax.rsqrt(var + LN_EPS)


def _silu(v):
    return v * jax.nn.sigmoid(v)


def _mod_kernel(c_ref, w_ref, b_ref, o_ref):
    a = _silu(c_ref[...]).astype(BF16)
    o_ref[...] = jnp.dot(a, w_ref[...].astype(BF16), preferred_element_type=F32) + b_ref[...]


def _modulation(c, w_ada, b_ada):
    bsz, d = c.shape
    n = w_ada.shape[1]
    return pl.pallas_call(
        _mod_kernel,
        grid=(n // d,),
        in_specs=[
            pl.BlockSpec((bsz, d), lambda j: (0, 0)),
            pl.BlockSpec((d, d), lambda j: (0, j)),
            pl.BlockSpec((1, d), lambda j: (0, j)),
        ],
        out_specs=pl.BlockSpec((bsz, d), lambda j: (0, j)),
        out_shape=jax.ShapeDtypeStruct((bsz, n), F32),
        name="modulation",
    )(c, w_ada, b_ada.reshape(1, n))


def _prefix_sum_lanes(v, width):
    pos = lax.broadcasted_iota(jnp.int32, v.shape, 1) & (width - 1)
    d = 1
    while d < width:
        v = v + jnp.where(pos >= d, pltpu.roll(v, d, axis=1), 0.0)
        d *= 2
    return v


def _inproj_kernel(x_ref, mod_ref, wn_ref, bn_ref, wt_ref, bt_ref, cw_ref, cb_ref, wmk_ref, wmqt_ref,
                   fk_ref, mk_ref, fqt_ref, fvt_ref, mqt_ref, mvt_ref, mot_ref, gat_ref, gbt_ref,
                   rows_ref, cols_ref, prev_ref, fcarry_ref, *, tiles_per_seq, chunk, d_model):
    tm = x_ref.shape[0]
    first = (pl.program_id(0) % tiles_per_seq) == 0
    sh1 = mod_ref[0, 0:1, :]
    sc1 = mod_ref[0, 1:2, :]
    h = (_ln(x_ref[...]) * (1.0 + sc1) + sh1).astype(BF16)

    yn = jnp.dot(h, wn_ref[...], preferred_element_type=F32) + bn_ref[...]
    fk_ref[...] = yn[:, :FOX_WIDTH].astype(BF16)
    mu = yn[:, FOX_WIDTH:]

    @pl.when(first)
    def _():
        prev_ref[...] = jnp.zeros_like(prev_ref)
        fcarry_ref[...] = jnp.zeros_like(fcarry_ref)

    full = jnp.concatenate([prev_ref[...], mu], axis=0)
    u = cb_ref[...] + cw_ref[0:1, :] * full[SUBLANES - 3:SUBLANES - 3 + tm]
    for j in range(1, CONV_WIDTH):
        off = SUBLANES - (CONV_WIDTH - 1) + j
        u = u + cw_ref[j:j + 1, :] * full[off:off + tm]
    prev_ref[...] = mu[tm - SUBLANES:, :]
    u = _silu(u).astype(BF16)

    inner = d_model // MLSTM_HEADS
    qk = wmk_ref.shape[2]
    for hd in range(MLSTM_HEADS):
        uh = u[:, hd * inner:(hd + 1) * inner]
        mk_ref[:, hd * qk:(hd + 1) * qk] = jnp.dot(uh, wmk_ref[hd], preferred_element_type=F32).astype(BF16)
        qt = lax.dot_general(wmqt_ref[hd], uh, (((1,), (1,)), ((), ())), preferred_element_type=F32)
        mqt_ref[0, hd * qk:(hd + 1) * qk, :] = (qt * (qk ** -0.5)).astype(BF16)

    def seg(r0, n):
        y = lax.dot_general(wt_ref[r0:r0 + n, :], h, (((1,), (1,)), ((), ())), preferred_element_type=F32)
        return y + bt_ref[r0:r0 + n, :]

    r = 0
    for ref, n in ((fqt_ref, FOX_WIDTH), (fvt_ref, FOX_WIDTH), (mvt_ref, d_model), (mot_ref, d_model),
                   (gat_ref, d_model), (gbt_ref, d_model)):
        ref[0] = seg(r, n).astype(BF16)
        r += n

    g = seg(r, GATE_ROWS)
    ls = jax.nn.log_sigmoid(g)
    fcum = _prefix_sum_lanes(ls[0:FOX_HEADS], tm) + fcarry_ref[:, 0:1]
    fcarry_ref[...] = jnp.broadcast_to(fcum[:, tm - 1:tm], fcarry_ref.shape)
    bcum = _prefix_sum_lanes(ls[FOX_HEADS:], chunk)[MLSTM_HEADS:]
    ig = g[FOX_HEADS:FOX_HEADS + MLSTM_HEADS]
    rows_ref[0] = jnp.concatenate([fcum, bcum, ig], axis=0)
    colsT = jnp.concatenate([fcum, ig - bcum, jnp.zeros_like(ig)], axis=0)
    cols_ref[...] = colsT.T


def _inproj(x2, mod3, wn, bn, wt, bt, conv_w, conv_b, w_mk, w_mqt, *, bsz, seq, tm, chunk):
    tok, d = x2.shape
    nt = seq // tm
    qkw = w_mk.shape[0] * w_mk.shape[2]
    tmaj = lambda width: pl.BlockSpec((tm, width), lambda t: (t, 0))
    fmaj = lambda rows: pl.BlockSpec((1, rows, tm), lambda t: (t // nt, 0, t % nt))
    out_shape = (
        jax.ShapeDtypeStruct((tok, FOX_WIDTH), BF16),
        jax.ShapeDtypeStruct((tok, qkw), BF16),
        jax.ShapeDtypeStruct((bsz, FOX_WIDTH, seq), BF16),
        jax.ShapeDtypeStruct((bsz, FOX_WIDTH, seq), BF16),
        jax.ShapeDtypeStruct((bsz, qkw, seq), BF16),
        jax.ShapeDtypeStruct((bsz, d, seq), BF16),
        jax.ShapeDtypeStruct((bsz, d, seq), BF16),
        jax.ShapeDtypeStruct((bsz, d, seq), BF16),
        jax.ShapeDtypeStruct((bsz, d, seq), BF16),
        jax.ShapeDtypeStruct((bsz, GATE_ROWS, seq), F32),
        jax.ShapeDtypeStruct((tok, GATE_ROWS), F32),
    )
    out_specs = (tmaj(FOX_WIDTH), tmaj(qkw), fmaj(FOX_WIDTH), fmaj(FOX_WIDTH), fmaj(qkw),
                 fmaj(d), fmaj(d), fmaj(d), fmaj(d), fmaj(GATE_ROWS), tmaj(GATE_ROWS))
    kern = functools.partial(_inproj_kernel, tiles_per_seq=nt, chunk=chunk, d_model=d)
    return pl.pallas_call(
        kern,
        grid=(tok // tm,),
        in_specs=[
            pl.BlockSpec((tm, d), lambda t: (t, 0)),
            pl.BlockSpec((1,) + mod3.shape[1:], lambda t: (t // nt, 0, 0)),
            _const_spec(wn.shape), _const_spec(bn.shape), _const_spec(wt.shape), _const_spec(bt.shape),
            _const_spec(conv_w.shape), _const_spec(conv_b.shape), _const_spec(w_mk.shape), _const_spec(w_mqt.shape),
        ],
        out_specs=out_specs,
        out_shape=out_shape,
        scratch_shapes=[pltpu.VMEM((SUBLANES, d), F32), pltpu.VMEM((FOX_HEADS, 128), F32)],
        compiler_params=pltpu.CompilerParams(dimension_semantics=("arbitrary",), vmem_limit_bytes=VMEM_LIMIT_BYTES),
        name="inproj",
    )(x2, mod3, wn, bn, wt, bt, conv_w, conv_b, w_mk, w_mqt)


def _fox_kernel(qt_ref, k_ref, vt_ref, frow_ref, fcol_ref, o_ref, *, blk):
    i = pl.program_id(2)
    hd = FOX_HEAD_DIM
    qt2 = qt_ref[0]
    row = lax.broadcasted_iota(jnp.int32, qt2.shape, 0)
    causal = (lax.broadcasted_iota(jnp.int32, (blk, blk), 0)
              <= lax.broadcasted_iota(jnp.int32, (blk, blk), 1))
    outs = []
    for a in range(2):
        qm = jnp.where((row >= a * hd) & (row < (a + 1) * hd), qt2, jnp.zeros_like(qt2))
        f_t = frow_ref[0, 0, a:a + 1, :]

        def block(j, carry, masked):
            m, l, acc = carry
            off = pl.multiple_of(j * blk, blk)
            kb = k_ref[0, pl.ds(off, blk), :]
            s = jnp.dot(kb, qm, preferred_element_type=F32)
            f_s = fcol_ref[0, 0, pl.ds(off, blk), a:a + 1]
            s = s + (f_t - f_s)
            if masked:
                s = jnp.where(causal, s, -jnp.inf)
            m_new = jnp.maximum(m, jnp.max(s, axis=0, keepdims=True))
            alpha = jnp.exp(m - m_new)
            p = jnp.exp(s - m_new)
            l = alpha * l + jnp.sum(p, axis=0, keepdims=True)
            vb = vt_ref[0, a * hd:(a + 1) * hd, pl.ds(off, blk)]
            acc = alpha * acc + jnp.dot(vb, p.astype(BF16), preferred_element_type=F32)
            return m_new, l, acc

        init = (jnp.full((1, blk), -jnp.inf, F32), jnp.zeros((1, blk), F32), jnp.zeros((hd, blk), F32))
        carry = lax.fori_loop(0, i, lambda j, c: block(j, c, False), init)
        m, l, acc = block(i, carry, True)
        outs.append((acc / l).astype(BF16))
    o_ref[0] = jnp.concatenate(outs, axis=0)


def _fox_attention(fqt, fk3, fvt, frows, fcols, *, blk):
    bsz, width, seq = fqt.shape
    pairs = FOX_HEADS // 2
    pw = 2 * FOX_HEAD_DIM
    return pl.pallas_call(
        functools.partial(_fox_kernel, blk=blk),
        grid=(bsz, pairs, seq // blk),
        in_specs=[
            pl.BlockSpec((1, pw, blk), lambda b, p, i: (b, p, i)),
            pl.BlockSpec((1, seq, pw), lambda b, p, i: (b, 0, p)),
            pl.BlockSpec((1, pw, seq), lambda b, p, i: (b, p, 0)),
            pl.BlockSpec((1, 1, 2, blk), lambda b, p, i: (b, p, 0, i)),
            pl.BlockSpec((1, 1, seq, 2), lambda b, p, i: (b, p, 0, 0)),
        ],
        out_specs=pl.BlockSpec((1, pw, blk), lambda b, p, i: (b, p, i)),
        out_shape=jax.ShapeDtypeStruct((bsz, width, seq), BF16),
        compiler_params=pltpu.CompilerParams(dimension_semantics=("parallel", "parallel", "arbitrary"),
                                             vmem_limit_bytes=VMEM_LIMIT_BYTES),
        name="fox_attention",
    )(fqt, fk3, fvt, frows, fcols)


def _mlstm_kernel(qt_ref, k_ref, vt_ref, ot_ref, rows_ref, cols_ref, g_ref, h_ref, state_ref, m_ref):
    c = pl.program_id(1)
    L = qt_ref.shape[2]
    qk = k_ref.shape[2] // MLSTM_HEADS
    dv = vt_ref.shape[1] // MLSTM_HEADS

    @pl.when(c == 0)
    def _():
        state_ref[...] = jnp.zeros_like(state_ref)
        m_ref[...] = jnp.zeros_like(m_ref)

    src = lax.broadcasted_iota(jnp.int32, (L, L), 0)
    dst = lax.broadcasted_iota(jnp.int32, (L, L), 1)
    causal = src <= dst
    ones_rows = (lax.broadcasted_iota(jnp.int32, (VAUG_PAD, L), 0) == 0).astype(BF16)

    for hd in range(MLSTM_HEADS):
        qt = qt_ref[0, hd * qk:(hd + 1) * qk, :]
        kk = k_ref[0, :, hd * qk:(hd + 1) * qk]
        vaug = jnp.concatenate([vt_ref[0, hd * dv:(hd + 1) * dv, :], ones_rows], axis=0)
        b = rows_ref[0, FOX_HEADS + hd:FOX_HEADS + hd + 1, :]
        ig = rows_ref[0, FOX_HEADS + MLSTM_HEADS + hd:FOX_HEADS + MLSTM_HEADS + hd + 1, :]
        ccol = cols_ref[0, :, FOX_HEADS + hd:FOX_HEADS + hd + 1]
        m_prev = m_ref[hd]
        state = state_ref[hd]

        g = jnp.broadcast_to(b[:, L - 1:L], (1, L))
        a = g - b + ig
        dmat = jnp.where(causal, b + ccol, -jnp.inf)
        inter = b + m_prev
        m_t = jnp.maximum(inter, jnp.max(dmat, axis=0, keepdims=True))
        scores = jnp.dot(kk, qt, preferred_element_type=F32) * jnp.exp(dmat - m_t)
        w_inter = jnp.exp(inter - m_t)
        ht = (w_inter * jnp.dot(state.astype(BF16), qt, preferred_element_type=F32)
              + jnp.dot(vaug, scores.astype(BF16), preferred_element_type=F32))
        den = ht[dv:dv + 1, :]
        hh = ht[:dv, :] / jnp.maximum(jnp.abs(den), jnp.exp(-m_t))
        mu = jnp.mean(hh, axis=0, keepdims=True)
        dlt = hh - mu
        var = jnp.mean(dlt * dlt, axis=0, keepdims=True)
        hn = dlt * lax.rsqrt(var + LN_EPS) * g_ref[hd * dv:(hd + 1) * dv, :]
        gate = jax.nn.sigmoid(ot_ref[0, hd * dv:(hd + 1) * dv, :].astype(F32))
        h_ref[0, hd * dv:(hd + 1) * dv, :] = (hn * gate).astype(BF16)

        m_new = jnp.maximum(g + m_prev, jnp.max(a, axis=1, keepdims=True))
        decay = jnp.exp(g + m_prev - m_new)
        w = jnp.exp(a - m_new)
        upd = jnp.dot((vaug.astype(F32) * w).astype(BF16), kk, preferred_element_type=F32)
        state_ref[hd] = decay[:, :qk] * state + upd
        m_ref[hd] = m_new


def _mlstm(mqt, mk3, mvt, mot, rows, cols3, gcol, *, chunk):
    bsz, qkw, seq = mqt.shape
    d = mvt.shape[1]
    dv = d // MLSTM_HEADS
    qk = qkw // MLSTM_HEADS
    fmaj = lambda r: pl.BlockSpec((1, r, chunk), lambda b, c: (b, 0, c))
    return pl.pallas_call(
        _mlstm_kernel,
        grid=(bsz, seq // chunk),
        in_specs=[
            fmaj(qkw),
            pl.BlockSpec((1, chunk, qkw), lambda b, c: (b, c, 0)),
            fmaj(d), fmaj(d), fmaj(GATE_ROWS),
            pl.BlockSpec((1, chunk, GATE_ROWS), lambda b, c: (b, c, 0)),
            _const_spec(gcol.shape),
        ],
        out_specs=fmaj(d),
        out_shape=jax.ShapeDtypeStruct((bsz, d, seq), BF16),
        scratch_shapes=[pltpu.VMEM((MLSTM_HEADS, dv + VAUG_PAD, qk), F32),
                        pltpu.VMEM((MLSTM_HEADS, 1, chunk), F32)],
        compiler_params=pltpu.CompilerParams(dimension_semantics=("parallel", "arbitrary"),
                                             vmem_limit_bytes=VMEM_LIMIT_BYTES),
        name="mlstm",
    )(mqt, mk3, mvt, mot, rows, cols3, gcol)


def _tail_kernel(x_ref, mod_ref, att_ref, hm_ref, ga_ref, gb_ref, wpa_ref, wpb_ref, wout_ref,
                 l1g_ref, l1b_ref, wfi_ref, wfd_ref, l2g_ref, l2b_ref, o_ref, *, d_ff):
    g1 = mod_ref[0, 2:3, :]
    sh2 = mod_ref[0, 3:4, :]
    sc2 = mod_ref[0, 4:5, :]
    g2 = mod_ref[0, 5:6, :]
    at = jnp.dot(wpa_ref[...], att_ref[0], preferred_element_type=F32)
    bt = jnp.dot(wpb_ref[...], hm_ref[0], preferred_element_type=F32)
    yt = (jax.nn.sigmoid(ga_ref[0].astype(F32)) * at + jax.nn.sigmoid(gb_ref[0].astype(F32)) * bt).astype(BF16)
    z = lax.dot_general(yt, wout_ref[...], (((0,), (0,)), ((), ())), preferred_element_type=F32)
    x1 = _ln(ALPHA * x_ref[...] + g1 * z) * l1g_ref[...] + l1b_ref[...]
    h2 = (_ln(x1) * (1.0 + sc2) + sh2).astype(BF16)
    gu = jnp.dot(h2, wfi_ref[...], preferred_element_type=F32)
    act = (_silu(gu[:, :d_ff]) * gu[:, d_ff:]).astype(BF16)
    ff = jnp.dot(act, wfd_ref[...], preferred_element_type=F32)
    o_ref[...] = _ln(ALPHA * x1 + g2 * ff) * l2g_ref[...] + l2b_ref[...]


def _tail(x2, mod3, att_t, hm_t, ga_t, gb_t, wpa_t, wpb_t, w_out, l1g, l1b, wfi, wfd, l2g, l2b, *, seq, tm):
    tok, d = x2.shape
    nt = seq // tm
    d_ff = wfd.shape[0]
    fmaj = lambda rows: pl.BlockSpec((1, rows, tm), lambda t: (t // nt, 0, t % nt))
    consts = (wpa_t, wpb_t, w_out, l1g, l1b, wfi, wfd, l2g, l2b)
    return pl.pallas_call(
        functools.partial(_tail_kernel, d_ff=d_ff),
        grid=(tok // tm,),
        in_specs=[
            pl.BlockSpec((tm, d), lambda t: (t, 0)),
            pl.BlockSpec((1,) + mod3.shape[1:], lambda t: (t // nt, 0, 0)),
            fmaj(att_t.shape[1]), fmaj(d), fmaj(d), fmaj(d),
        ] + [_const_spec(a.shape) for a in consts],
        out_specs=pl.BlockSpec((tm, d), lambda t: (t, 0)),
        out_shape=jax.ShapeDtypeStruct((tok, d), F32),
        compiler_params=pltpu.CompilerParams(dimension_semantics=("parallel",), vmem_limit_bytes=VMEM_LIMIT_BYTES),
        name="tail",
    )(x2, mod3, att_t, hm_t, ga_t, gb_t, *consts)


def _pick(n, pref):
    while n % pref:
        pref //= 2
    return pref


def kernel(x, c, w_ada, b_ada, w_in, b_in, conv_w, conv_b, w_mq, w_mk, mh_norm_g, w_pa, w_pb, w_out,
           ln1_g, ln1_b, w_ffn_in, w_ffn_down, ln2_g, ln2_b):
    bsz, seq, d = x.shape
    tok = bsz * seq
    blk = _pick(seq, 256)
    tm_in = _pick(seq, 512)
    tm_tail = _pick(seq, 256)

    x2 = x.reshape(tok, d)
    for l in range(DEPTH):
        mod3 = _modulation(c, w_ada[l], b_ada[l]).reshape(bsz, 6, d)

        splits = (FOX_WIDTH, FOX_WIDTH, FOX_WIDTH, FOX_HEADS, d, d, MLSTM_HEADS, MLSTM_HEADS, d, d, d)
        offs = [0]
        for s in splits:
            offs.append(offs[-1] + s)
        col = lambda i: (w_in[l][:, offs[i]:offs[i + 1]], b_in[l][offs[i]:offs[i + 1]])
        (wfq, bfq), (wfk, bfk), (wfv, bfv), (wff, bff), (wmu, bmu), (wmv, bmv), (wmi, bmi), (wmf, bmf), \
            (wmo, bmo), (wga, bga), (wgb, bgb) = [col(i) for i in range(len(splits))]
        scale = FOX_HEAD_DIM ** -0.5
        wn = jnp.concatenate([wfk, wmu], axis=1).astype(BF16)
        bn = jnp.concatenate([bfk, bmu])[None, :]
        wt = jnp.concatenate([wfq * scale, wfv, wmv, wmo, wga, wgb, wff, wmi, wmf], axis=1).T.astype(BF16)
        bt = jnp.concatenate([bfq * scale, bfv, bmv, bmo, bga, bgb, bff, bmi, bmf])[:, None]

        fk, mk, fqt, fvt, mqt, mvt, mot, gat, gbt, rows, cols = _inproj(
            x2, mod3, wn, bn, wt, bt, conv_w[l], conv_b[l][None, :], w_mk[l].astype(BF16),
            jnp.swapaxes(w_mq[l], 1, 2).astype(BF16), bsz=bsz, seq=seq, tm=tm_in, chunk=blk)

        pairs = FOX_HEADS // 2
        frows = rows[:, :FOX_HEADS].reshape(bsz, pairs, 2, seq)
        fcols = cols[:, :FOX_HEADS].reshape(bsz, seq, pairs, 2).transpose(0, 2, 1, 3)
        att_t = _fox_attention(fqt, fk.reshape(bsz, seq, FOX_WIDTH), fvt, frows, fcols, blk=blk)

        hm_t = _mlstm(mqt, mk.reshape(bsz, seq, -1), mvt, mot, rows, cols.reshape(bsz, seq, GATE_ROWS),
                      mh_norm_g[l][:, None], chunk=blk)

        x2 = _tail(x2, mod3, att_t, hm_t, gat, gbt, w_pa[l].T.astype(BF16), w_pb[l].T.astype(BF16),
                   w_out[l].astype(BF16), ln1_g[l][None, :], ln1_b[l][None, :], w_ffn_in[l].astype(BF16),
                   w_ffn_down[l].astype(BF16), ln2_g[l][None, :], ln2_b[l][None, :], seq=seq, tm=tm_tail)
    return x2.reshape(bsz, seq, d)
```

```python
import functools

import jax
import jax.numpy as jnp
from jax import lax
from jax.experimental import pallas as pl
from jax.experimental.pallas import tpu as pltpu

F32 = jnp.float32
BF16 = jnp.bfloat16

FOX_HEADS = 8
FOX_HEAD_DIM = 64
FOX_WIDTH = FOX_HEADS * FOX_HEAD_DIM
MLSTM_HEADS = 4
CONV_WIDTH = 4
LN_EPS = 1e-5
DEPTH = 1
ALPHA = (2.0 * DEPTH) ** 0.25

VMEM_LIMIT_BYTES = 56 * 1024 * 1024
SUBLANES = 8
GATE_ROWS = 16
VAUG_PAD = 16


def _const_spec(shape):
    nd = len(shape)
    return pl.BlockSpec(shape, lambda *_: (0,) * nd, pipeline_mode=pl.Buffered(1))


def _ln(v):
    mu = jnp.mean(v, axis=-1, keepdims=True)
    d = v - mu
    var = jnp.mean(d * d, axis=-1, keepdims=True)
    return d * lax.rsqrt(var + LN_EPS)


def _silu(v):
    return v * jax.nn.sigmoid(v)


def _mod_kernel(c_ref, w_ref, b_ref, o_ref):
    a = _silu(c_ref[...]).astype(BF16)
    o_ref[...] = jnp.dot(a, w_ref[...].astype(BF16), preferred_element_type=F32) + b_ref[...]


def _modulation(c, w_ada, b_ada):
    bsz, d = c.shape
    n = w_ada.shape[1]
    return pl.pallas_call(
        _mod_kernel,
        grid=(n // d,),
        in_specs=[
            pl.BlockSpec((bsz, d), lambda j: (0, 0)),
            pl.BlockSpec((d, d), lambda j: (0, j)),
            pl.BlockSpec((1, d), lambda j: (0, j)),
        ],
        out_specs=pl.BlockSpec((bsz, d), lambda j: (0, j)),
        out_shape=jax.ShapeDtypeStruct((bsz, n), F32),
        name="modulation",
    )(c, w_ada, b_ada.reshape(1, n))


def _prefix_sum_lanes(v, width):
    pos = lax.broadcasted_iota(jnp.int32, v.shape, 1) & (width - 1)
    d = 1
    while d < width:
        v = v + jnp.where(pos >= d, pltpu.roll(v, d, axis=1), 0.0)
        d *= 2
    return v


def _inproj_kernel(x_ref, mod_ref, wn_ref, bn_ref, wt_ref, bt_ref, cw_ref, cb_ref, wmk_ref, wmqt_ref,
                   fk_ref, mk_ref, fqt_ref, fvt_ref, mqt_ref, mvt_ref, mot_ref, gat_ref, gbt_ref,
                   rows_ref, cols_ref, prev_ref, fcarry_ref, *, tiles_per_seq, chunk, d_model):
    tm = x_ref.shape[0]
    first = (pl.program_id(0) % tiles_per_seq) == 0
    sh1 = mod_ref[0, 0:1, :]
    sc1 = mod_ref[0, 1:2, :]
    h = (_ln(x_ref[...]) * (1.0 + sc1) + sh1).astype(BF16)

    yn = jnp.dot(h, wn_ref[...], preferred_element_type=F32) + bn_ref[...]
    fk_ref[...] = yn[:, :FOX_WIDTH].astype(BF16)
    mu = yn[:, FOX_WIDTH:]

    @pl.when(first)
    def _():
        prev_ref[...] = jnp.zeros_like(prev_ref)
        fcarry_ref[...] = jnp.zeros_like(fcarry_ref)

    full = jnp.concatenate([prev_ref[...], mu], axis=0)
    u = cb_ref[...] + cw_ref[0:1, :] * full[SUBLANES - 3:SUBLANES - 3 + tm]
    for j in range(1, CONV_WIDTH):
        off = SUBLANES - (CONV_WIDTH - 1) + j
        u = u + cw_ref[j:j + 1, :] * full[off:off + tm]
    prev_ref[...] = mu[tm - SUBLANES:, :]
    u = _silu(u).astype(BF16)

    inner = d_model // MLSTM_HEADS
    qk = wmk_ref.shape[2]
    for hd in range(MLSTM_HEADS):
        uh = u[:, hd * inner:(hd + 1) * inner]
        mk_ref[:, hd * qk:(hd + 1) * qk] = jnp.dot(uh, wmk_ref[hd], preferred_element_type=F32).astype(BF16)
        qt = lax.dot_general(wmqt_ref[hd], uh, (((1,), (1,)), ((), ())), preferred_element_type=F32)
        mqt_ref[0, hd * qk:(hd + 1) * qk, :] = (qt * (qk ** -0.5)).astype(BF16)

    def seg(r0, n):
        y = lax.dot_general(wt_ref[r0:r0 + n, :], h, (((1,), (1,)), ((), ())), preferred_element_type=F32)
        return y + bt_ref[r0:r0 + n, :]

    r = 0
    for ref, n in ((fqt_ref, FOX_WIDTH), (fvt_ref, FOX_WIDTH), (mvt_ref, d_model), (mot_ref, d_model),
                   (gat_ref, d_model), (gbt_ref, d_model)):
        ref[0] = seg(r, n).astype(BF16)
        r += n

    g = seg(r, GATE_ROWS)
    ls = jax.nn.log_sigmoid(g)
    fcum = _prefix_sum_lanes(ls[0:FOX_HEADS], tm) + fcarry_ref[:, 0:1]
    fcarry_ref[...] = jnp.broadcast_to(fcum[:, tm - 1:tm], fcarry_ref.shape)
    bcum = _prefix_sum_lanes(ls[FOX_HEADS:], chunk)[MLSTM_HEADS:]
    ig = g[FOX_HEADS:FOX_HEADS + MLSTM_HEADS]
    rows_ref[0] = jnp.concatenate([fcum, bcum, ig], axis=0)
    colsT = jnp.concatenate([fcum, ig - bcum, jnp.zeros_like(ig)], axis=0)
    cols_ref[...] = colsT.T


def _inproj(x2, mod3, wn, bn, wt, bt, conv_w, conv_b, w_mk, w_mqt, *, bsz, seq, tm, chunk):
    tok, d = x2.shape
    nt = seq // tm
    qkw = w_mk.shape[0] * w_mk.shape[2]
    tmaj = lambda width: pl.BlockSpec((tm, width), lambda t: (t, 0))
    fmaj = lambda rows: pl.BlockSpec((1, rows, tm), lambda t: (t // nt, 0, t % nt))
    out_shape = (
        jax.ShapeDtypeStruct((tok, FOX_WIDTH), BF16),
        jax.ShapeDtypeStruct((tok, qkw), BF16),
        jax.ShapeDtypeStruct((bsz, FOX_WIDTH, seq), BF16),
        jax.ShapeDtypeStruct((bsz, FOX_WIDTH, seq), BF16),
        jax.ShapeDtypeStruct((bsz, qkw, seq), BF16),
        jax.ShapeDtypeStruct((bsz, d, seq), BF16),
        jax.ShapeDtypeStruct((bsz, d, seq), BF16),
        jax.ShapeDtypeStruct((bsz, d, seq), BF16),
        jax.ShapeDtypeStruct((bsz, d, seq), BF16),
        jax.ShapeDtypeStruct((bsz, GATE_ROWS, seq), F32),
        jax.ShapeDtypeStruct((tok, GATE_ROWS), F32),
    )
    out_specs = (tmaj(FOX_WIDTH), tmaj(qkw), fmaj(FOX_WIDTH), fmaj(FOX_WIDTH), fmaj(qkw),
                 fmaj(d), fmaj(d), fmaj(d), fmaj(d), fmaj(GATE_ROWS), tmaj(GATE_ROWS))
    kern = functools.partial(_inproj_kernel, tiles_per_seq=nt, chunk=chunk, d_model=d)
    return pl.pallas_call(
        kern,
        grid=(tok // tm,),
        in_specs=[
            pl.BlockSpec((tm, d), lambda t: (t, 0)),
            pl.BlockSpec((1,) + mod3.shape[1:], lambda t: (t // nt, 0, 0)),
            _const_spec(wn.shape), _const_spec(bn.shape), _const_spec(wt.shape), _const_spec(bt.shape),
            _const_spec(conv_w.shape), _const_spec(conv_b.shape), _const_spec(w_mk.shape), _const_spec(w_mqt.shape),
        ],
        out_specs=out_specs,
        out_shape=out_shape,
        scratch_shapes=[pltpu.VMEM((SUBLANES, d), F32), pltpu.VMEM((FOX_HEADS, 128), F32)],
        compiler_params=pltpu.CompilerParams(dimension_semantics=("arbitrary",), vmem_limit_bytes=VMEM_LIMIT_BYTES),
        name="inproj",
    )(x2, mod3, wn, bn, wt, bt, conv_w, conv_b, w_mk, w_mqt)


def _fox_tile(nkb, qt_ref, k_ref, vt_ref, frow_ref, fcol_ref, o_ref, blk):
    hd = FOX_HEAD_DIM
    top = (nkb - 1) * blk
    qt2 = qt_ref[0]
    row = lax.broadcasted_iota(jnp.int32, qt2.shape, 0)
    causal = (lax.broadcasted_iota(jnp.int32, (blk, blk), 0)
              <= lax.broadcasted_iota(jnp.int32, (blk, blk), 1))
    kb = k_ref[0, 0:top + blk, :]
    outs = []
    for a in range(2):
        qm = jnp.where((row >= a * hd) & (row < (a + 1) * hd), qt2, jnp.zeros_like(qt2))
        f_t = frow_ref[0, 0, a:a + 1, :]
        f_s = fcol_ref[0, 0, 0:top + blk, a:a + 1]
        s = jnp.dot(kb, qm, preferred_element_type=F32) + (f_t - f_s)
        s_diag = jnp.where(causal, s[top:], -jnp.inf)
        m = jnp.max(s_diag, axis=0, keepdims=True)
        if top:
            m = jnp.maximum(m, jnp.max(s[:top], axis=0, keepdims=True))
        p_diag = jnp.exp(s_diag - m)
        l = jnp.sum(p_diag, axis=0, keepdims=True)
        acc = jnp.dot(vt_ref[0, a * hd:(a + 1) * hd, top:top + blk], p_diag.astype(BF16),
                      preferred_element_type=F32)
        if top:
            p_top = jnp.exp(s[:top] - m)
            l = l + jnp.sum(p_top, axis=0, keepdims=True)
            acc = acc + jnp.dot(vt_ref[0, a * hd:(a + 1) * hd, 0:top], p_top.astype(BF16),
                                preferred_element_type=F32)
        outs.append((acc / l).astype(BF16))
    o_ref[0] = jnp.concatenate(outs, axis=0)


def _fox_kernel(qt_ref, k_ref, vt_ref, frow_ref, fcol_ref, o_ref, *, blk):
    i = pl.program_id(2)
    for nkb in range(1, k_ref.shape[1] // blk + 1):
        pl.when(i == nkb - 1)(functools.partial(
            _fox_tile, nkb, qt_ref, k_ref, vt_ref, frow_ref, fcol_ref, o_ref, blk))


def _fox_attention(fqt, fk3, fvt, frows, fcols, *, blk):
    bsz, width, seq = fqt.shape
    pairs = FOX_HEADS // 2
    pw = 2 * FOX_HEAD_DIM
    return pl.pallas_call(
        functools.partial(_fox_kernel, blk=blk),
        grid=(bsz, pairs, seq // blk),
        in_specs=[
            pl.BlockSpec((1, pw, blk), lambda b, p, i: (b, p, i)),
            pl.BlockSpec((1, seq, pw), lambda b, p, i: (b, 0, p)),
            pl.BlockSpec((1, pw, seq), lambda b, p, i: (b, p, 0)),
            pl.BlockSpec((1, 1, 2, blk), lambda b, p, i: (b, p, 0, i)),
            pl.BlockSpec((1, 1, seq, 2), lambda b, p, i: (b, p, 0, 0)),
        ],
        out_specs=pl.BlockSpec((1, pw, blk), lambda b, p, i: (b, p, i)),
        out_shape=jax.ShapeDtypeStruct((bsz, width, seq), BF16),
        compiler_params=pltpu.CompilerParams(dimension_semantics=("parallel", "parallel", "arbitrary"),
                                             vmem_limit_bytes=VMEM_LIMIT_BYTES),
        name="fox_attention",
    )(fqt, fk3, fvt, frows, fcols)


def _mlstm_kernel(qt_ref, k_ref, vt_ref, ot_ref, rows_ref, cols_ref, g_ref, h_ref, state_ref, m_ref):
    c = pl.program_id(1)
    L = qt_ref.shape[2]
    qk = k_ref.shape[2] // MLSTM_HEADS
    dv = vt_ref.shape[1] // MLSTM_HEADS

    @pl.when(c == 0)
    def _():
        state_ref[...] = jnp.zeros_like(state_ref)
        m_ref[...] = jnp.zeros_like(m_ref)

    src = lax.broadcasted_iota(jnp.int32, (L, L), 0)
    dst = lax.broadcasted_iota(jnp.int32, (L, L), 1)
    causal = src <= dst
    ones_rows = (lax.broadcasted_iota(jnp.int32, (VAUG_PAD, L), 0) == 0).astype(BF16)

    for hd in range(MLSTM_HEADS):
        qt = qt_ref[0, hd * qk:(hd + 1) * qk, :]
        kk = k_ref[0, :, hd * qk:(hd + 1) * qk]
        vaug = jnp.concatenate([vt_ref[0, hd * dv:(hd + 1) * dv, :], ones_rows], axis=0)
        b = rows_ref[0, FOX_HEADS + hd:FOX_HEADS + hd + 1, :]
        ig = rows_ref[0, FOX_HEADS + MLSTM_HEADS + hd:FOX_HEADS + MLSTM_HEADS + hd + 1, :]
        ccol = cols_ref[0, :, FOX_HEADS + hd:FOX_HEADS + hd + 1]
        m_prev = m_ref[hd]
        state = state_ref[hd]

        g = jnp.broadcast_to(b[:, L - 1:L], (1, L))
        a = g - b + ig
        dmat = jnp.where(causal, b + ccol, -jnp.inf)
        inter = b + m_prev
        m_t = jnp.maximum(inter, jnp.max(dmat, axis=0, keepdims=True))
        scores = jnp.dot(kk, qt, preferred_element_type=F32) * jnp.exp(dmat - m_t)
        w_inter = jnp.exp(inter - m_t)
        ht = (w_inter * jnp.dot(state.astype(BF16), qt, preferred_element_type=F32)
              + jnp.dot(vaug, scores.astype(BF16), preferred_element_type=F32))
        den = ht[dv:dv + 1, :]
        hh = ht[:dv, :] / jnp.maximum(jnp.abs(den), jnp.exp(-m_t))
        mu = jnp.mean(hh, axis=0, keepdims=True)
        dlt = hh - mu
        var = jnp.mean(dlt * dlt, axis=0, keepdims=True)
        hn = dlt * lax.rsqrt(var + LN_EPS) * g_ref[hd * dv:(hd + 1) * dv, :]
        gate = jax.nn.sigmoid(ot_ref[0, hd * dv:(hd + 1) * dv, :].astype(F32))
        h_ref[0, hd * dv:(hd + 1) * dv, :] = (hn * gate).astype(BF16)

        m_new = jnp.maximum(g + m_prev, jnp.max(a, axis=1, keepdims=True))
        decay = jnp.exp(g + m_prev - m_new)
        w = jnp.exp(a - m_new)
        upd = jnp.dot((vaug.astype(F32) * w).astype(BF16), kk, preferred_element_type=F32)
        state_ref[hd] = decay[:, :qk] * state + upd
        m_ref[hd] = m_new


def _mlstm(mqt, mk3, mvt, mot, rows, cols3, gcol, *, chunk):
    bsz, qkw, seq = mqt.shape
    d = mvt.shape[1]
    dv = d // MLSTM_HEADS
    qk = qkw // MLSTM_HEADS
    fmaj = lambda r: pl.BlockSpec((1, r, chunk), lambda b, c: (b, 0, c))
    return pl.pallas_call(
        _mlstm_kernel,
        grid=(bsz, seq // chunk),
        in_specs=[
            fmaj(qkw),
            pl.BlockSpec((1, chunk, qkw), lambda b, c: (b, c, 0)),
            fmaj(d), fmaj(d), fmaj(GATE_ROWS),
            pl.BlockSpec((1, chunk, GATE_ROWS), lambda b, c: (b, c, 0)),
            _const_spec(gcol.shape),
        ],
        out_specs=fmaj(d),
        out_shape=jax.ShapeDtypeStruct((bsz, d, seq), BF16),
        scratch_shapes=[pltpu.VMEM((MLSTM_HEADS, dv + VAUG_PAD, qk), F32),
                        pltpu.VMEM((MLSTM_HEADS, 1, chunk), F32)],
        compiler_params=pltpu.CompilerParams(dimension_semantics=("parallel", "arbitrary"),
                                             vmem_limit_bytes=VMEM_LIMIT_BYTES),
        name="mlstm",
    )(mqt, mk3, mvt, mot, rows, cols3, gcol)


def _tail_kernel(x_ref, mod_ref, att_ref, hm_ref, ga_ref, gb_ref, wpa_ref, wpb_ref, wout_ref,
                 l1g_ref, l1b_ref, wfi_ref, wfd_ref, l2g_ref, l2b_ref, o_ref, *, d_ff):
    g1 = mod_ref[0, 2:3, :]
    sh2 = mod_ref[0, 3:4, :]
    sc2 = mod_ref[0, 4:5, :]
    g2 = mod_ref[0, 5:6, :]
    at = jnp.dot(wpa_ref[...], att_ref[0], preferred_element_type=F32)
    bt = jnp.dot(wpb_ref[...], hm_ref[0], preferred_element_type=F32)
    yt = (jax.nn.sigmoid(ga_ref[0].astype(F32)) * at + jax.nn.sigmoid(gb_ref[0].astype(F32)) * bt).astype(BF16)
    z = lax.dot_general(yt, wout_ref[...], (((0,), (0,)), ((), ())), preferred_element_type=F32)
    x1 = _ln(ALPHA * x_ref[...] + g1 * z) * l1g_ref[...] + l1b_ref[...]
    h2 = (_ln(x1) * (1.0 + sc2) + sh2).astype(BF16)
    gu = jnp.dot(h2, wfi_ref[...], preferred_element_type=F32)
    act = (_silu(gu[:, :d_ff]) * gu[:, d_ff:]).astype(BF16)
    ff = jnp.dot(act, wfd_ref[...], preferred_element_type=F32)
    o_ref[...] = _ln(ALPHA * x1 + g2 * ff) * l2g_ref[...] + l2b_ref[...]


def _tail(x2, mod3, att_t, hm_t, ga_t, gb_t, wpa_t, wpb_t, w_out, l1g, l1b, wfi, wfd, l2g, l2b, *, seq, tm):
    tok, d = x2.shape
    nt = seq // tm
    d_ff = wfd.shape[0]
    fmaj = lambda rows: pl.BlockSpec((1, rows, tm), lambda t: (t // nt, 0, t % nt))
    consts = (wpa_t, wpb_t, w_out, l1g, l1b, wfi, wfd, l2g, l2b)
    return pl.pallas_call(
        functools.partial(_tail_kernel, d_ff=d_ff),
        grid=(tok // tm,),
        in_specs=[
            pl.BlockSpec((tm, d), lambda t: (t, 0)),
            pl.BlockSpec((1,) + mod3.shape[1:], lambda t: (t // nt, 0, 0)),
            fmaj(att_t.shape[1]), fmaj(d), fmaj(d), fmaj(d),
        ] + [_const_spec(a.shape) for a in consts],
        out_specs=pl.BlockSpec((tm, d), lambda t: (t, 0)),
        out_shape=jax.ShapeDtypeStruct((tok, d), F32),
        compiler_params=pltpu.CompilerParams(dimension_semantics=("parallel",), vmem_limit_bytes=VMEM_LIMIT_BYTES),
        name="tail",
    )(x2, mod3, att_t, hm_t, ga_t, gb_t, *consts)


def _pick(n, pref):
    while n % pref:
        pref //= 2
    return pref


def kernel(x, c, w_ada, b_ada, w_in, b_in, conv_w, conv_b, w_mq, w_mk, mh_norm_g, w_pa, w_pb, w_out,
           ln1_g, ln1_b, w_ffn_in, w_ffn_down, ln2_g, ln2_b):
    bsz, seq, d = x.shape
    tok = bsz * seq
    chunk = _pick(seq, 256)
    fox_blk = _pick(seq, 512)
    tm_in = _pick(seq, 512)
    tm_tail = _pick(seq, 256)

    x2 = x.reshape(tok, d)
    for l in range(DEPTH):
        mod3 = _modulation(c, w_ada[l], b_ada[l]).reshape(bsz, 6, d)

        splits = (FOX_WIDTH, FOX_WIDTH, FOX_WIDTH, FOX_HEADS, d, d, MLSTM_HEADS, MLSTM_HEADS, d, d, d)
        offs = [0]
        for s in splits:
            offs.append(offs[-1] + s)
        col = lambda i: (w_in[l][:, offs[i]:offs[i + 1]], b_in[l][offs[i]:offs[i + 1]])
        (wfq, bfq), (wfk, bfk), (wfv, bfv), (wff, bff), (wmu, bmu), (wmv, bmv), (wmi, bmi), (wmf, bmf), \
            (wmo, bmo), (wga, bga), (wgb, bgb) = [col(i) for i in range(len(splits))]
        scale = FOX_HEAD_DIM ** -0.5
        wn = jnp.concatenate([wfk, wmu], axis=1).astype(BF16)
        bn = jnp.concatenate([bfk, bmu])[None, :]
        wt = jnp.concatenate([wfq * scale, wfv, wmv, wmo, wga, wgb, wff, wmi, wmf], axis=1).T.astype(BF16)
        bt = jnp.concatenate([bfq * scale, bfv, bmv, bmo, bga, bgb, bff, bmi, bmf])[:, None]

        fk, mk, fqt, fvt, mqt, mvt, mot, gat, gbt, rows, cols = _inproj(
            x2, mod3, wn, bn, wt, bt, conv_w[l], conv_b[l][None, :], w_mk[l].astype(BF16),
            jnp.swapaxes(w_mq[l], 1, 2).astype(BF16), bsz=bsz, seq=seq, tm=tm_in, chunk=chunk)

        pairs = FOX_HEADS // 2
        frows = rows[:, :FOX_HEADS].reshape(bsz, pairs, 2, seq)
        fcols = cols[:, :FOX_HEADS].reshape(bsz, seq, pairs, 2).transpose(0, 2, 1, 3)
        att_t = _fox_attention(fqt, fk.reshape(bsz, seq, FOX_WIDTH), fvt, frows, fcols, blk=fox_blk)

        hm_t = _mlstm(mqt, mk.reshape(bsz, seq, -1), mvt, mot, rows, cols.reshape(bsz, seq, GATE_ROWS),
                      mh_norm_g[l][:, None], chunk=chunk)

        x2 = _tail(x2, mod3, att_t, hm_t, gat, gbt, w_pa[l].T.astype(BF16), w_pb[l].T.astype(BF16),
                   w_out[l].astype(BF16), ln1_g[l][None, :], ln1_b[l][None, :], w_ffn_in[l].astype(BF16),
                   w_ffn_down[l].astype(BF16), ln2_g[l][None, :], ln2_b[l][None, :], seq=seq, tm=tm_tail)
    return x2.reshape(bsz, seq, d)
```

```python
import functools

import jax
import jax.numpy as jnp
from jax import lax
from jax.experimental import pallas as pl
from jax.experimental.pallas import tpu as pltpu

F32 = jnp.float32
BF16 = jnp.bfloat16

FOX_HEADS = 8
FOX_HEAD_DIM = 64
FOX_WIDTH = FOX_HEADS * FOX_HEAD_DIM
MLSTM_HEADS = 4
CONV_WIDTH = 4
LN_EPS = 1e-5
DEPTH = 1
ALPHA = (2.0 * DEPTH) ** 0.25

VMEM_LIMIT_BYTES = 56 * 1024 * 1024
SUBLANES = 8
GATE_ROWS = 16
VAUG_PAD = 16


def _const_spec(shape):
    nd = len(shape)
    return pl.BlockSpec(shape, lambda *_: (0,) * nd, pipeline_mode=pl.Buffered(1))


def _ln(v):
    mu = jnp.mean(v, axis=-1, keepdims=True)
    d = v - mu
    var = jnp.mean(d * d, axis=-1, keepdims=True)
    return d * lax.rsqrt(var + LN_EPS)


def _silu(v):
    return v * jax.nn.sigmoid(v)


def _mod_kernel(c_ref, w_ref, b_ref, o_ref):
    a = _silu(c_ref[...]).astype(BF16)
    o_ref[...] = jnp.dot(a, w_ref[...].astype(BF16), preferred_element_type=F32) + b_ref[...]


def _modulation(c, w_ada, b_ada):
    bsz, d = c.shape
    n = w_ada.shape[1]
    return pl.pallas_call(
        _mod_kernel,
        grid=(n // d,),
        in_specs=[
            pl.BlockSpec((bsz, d), lambda j: (0, 0)),
            pl.BlockSpec((d, d), lambda j: (0, j)),
            pl.BlockSpec((1, d), lambda j: (0, j)),
        ],
        out_specs=pl.BlockSpec((bsz, d), lambda j: (0, j)),
        out_shape=jax.ShapeDtypeStruct((bsz, n), F32),
        name="modulation",
    )(c, w_ada, b_ada.reshape(1, n))


def _prefix_sum_lanes(v, width):
    pos = lax.broadcasted_iota(jnp.int32, v.shape, 1) & (width - 1)
    d = 1
    while d < width:
        v = v + jnp.where(pos >= d, pltpu.roll(v, d, axis=1), 0.0)
        d *= 2
    return v


def _inproj_kernel(x_ref, mod_ref, wn_ref, bn_ref, wt_ref, bt_ref, cw_ref, cb_ref, wmk_ref, wmqt_ref,
                   fk_ref, mk_ref, fqt_ref, fvt_ref, mqt_ref, mvt_ref, mot_ref, gat_ref, gbt_ref,
                   rows_ref, cols_ref, prev_ref, fcarry_ref, *, tiles_per_seq, chunk, d_model):
    tm = x_ref.shape[0]
    first = (pl.program_id(0) % tiles_per_seq) == 0
    sh1 = mod_ref[0, 0:1, :]
    sc1 = mod_ref[0, 1:2, :]
    h = (_ln(x_ref[...]) * (1.0 + sc1) + sh1).astype(BF16)

    @pl.when(first)
    def _():
        prev_ref[...] = jnp.zeros_like(prev_ref)
        fcarry_ref[...] = jnp.zeros_like(fcarry_ref)

    yn = jnp.dot(h, wn_ref[...], preferred_element_type=F32) + bn_ref[...]
    fk_ref[...] = yn[:, :FOX_WIDTH].astype(BF16)
    mu = yn[:, FOX_WIDTH:]

    def seg(r0, n):
        y = lax.dot_general(wt_ref[r0:r0 + n, :], h, (((1,), (1,)), ((), ())), preferred_element_type=F32)
        return y + bt_ref[r0:r0 + n, :]

    g = seg(wt_ref.shape[0] - GATE_ROWS, GATE_ROWS)
    ls = jax.nn.log_sigmoid(g)
    fcum = _prefix_sum_lanes(ls[0:FOX_HEADS], tm) + fcarry_ref[:, 0:1]
    fcarry_ref[...] = jnp.broadcast_to(fcum[:, tm - 1:tm], fcarry_ref.shape)
    bcum = _prefix_sum_lanes(ls[FOX_HEADS:], chunk)[MLSTM_HEADS:]
    ig = g[FOX_HEADS:FOX_HEADS + MLSTM_HEADS]
    rows_ref[0] = jnp.concatenate([fcum, bcum, ig], axis=0)
    colsT = jnp.concatenate([fcum, ig - bcum, jnp.zeros_like(ig)], axis=0)
    cols_ref[...] = colsT.T

    r = 0
    for ref, n in ((fqt_ref, FOX_WIDTH), (fvt_ref, FOX_WIDTH), (mvt_ref, d_model), (mot_ref, d_model),
                   (gat_ref, d_model), (gbt_ref, d_model)):
        ref[0] = seg(r, n).astype(BF16)
        r += n

    full = jnp.concatenate([prev_ref[...], mu], axis=0)
    u = cb_ref[...] + cw_ref[0:1, :] * full[SUBLANES - 3:SUBLANES - 3 + tm]
    for j in range(1, CONV_WIDTH):
        off = SUBLANES - (CONV_WIDTH - 1) + j
        u = u + cw_ref[j:j + 1, :] * full[off:off + tm]
    prev_ref[...] = mu[tm - SUBLANES:, :]
    u = _silu(u).astype(BF16)

    inner = d_model // MLSTM_HEADS
    qk = wmk_ref.shape[2]
    for hd in range(MLSTM_HEADS):
        uh = u[:, hd * inner:(hd + 1) * inner]
        mk_ref[:, hd * qk:(hd + 1) * qk] = jnp.dot(uh, wmk_ref[hd], preferred_element_type=F32).astype(BF16)
        qt = lax.dot_general(wmqt_ref[hd], uh, (((1,), (1,)), ((), ())), preferred_element_type=F32)
        mqt_ref[0, hd * qk:(hd + 1) * qk, :] = (qt * (qk ** -0.5)).astype(BF16)


def _inproj(x2, mod3, wn, bn, wt, bt, conv_w, conv_b, w_mk, w_mqt, *, bsz, seq, tm, chunk):
    tok, d = x2.shape
    nt = seq // tm
    qkw = w_mk.shape[0] * w_mk.shape[2]
    tmaj = lambda width: pl.BlockSpec((tm, width), lambda t: (t, 0))
    fmaj = lambda rows: pl.BlockSpec((1, rows, tm), lambda t: (t // nt, 0, t % nt))
    out_shape = (
        jax.ShapeDtypeStruct((tok, FOX_WIDTH), BF16),
        jax.ShapeDtypeStruct((tok, qkw), BF16),
        jax.ShapeDtypeStruct((bsz, FOX_WIDTH, seq), BF16),
        jax.ShapeDtypeStruct((bsz, FOX_WIDTH, seq), BF16),
        jax.ShapeDtypeStruct((bsz, qkw, seq), BF16),
        jax.ShapeDtypeStruct((bsz, d, seq), BF16),
        jax.ShapeDtypeStruct((bsz, d, seq), BF16),
        jax.ShapeDtypeStruct((bsz, d, seq), BF16),
        jax.ShapeDtypeStruct((bsz, d, seq), BF16),
        jax.ShapeDtypeStruct((bsz, GATE_ROWS, seq), F32),
        jax.ShapeDtypeStruct((tok, GATE_ROWS), F32),
    )
    out_specs = (tmaj(FOX_WIDTH), tmaj(qkw), fmaj(FOX_WIDTH), fmaj(FOX_WIDTH), fmaj(qkw),
                 fmaj(d), fmaj(d), fmaj(d), fmaj(d), fmaj(GATE_ROWS), tmaj(GATE_ROWS))
    kern = functools.partial(_inproj_kernel, tiles_per_seq=nt, chunk=chunk, d_model=d)
    return pl.pallas_call(
        kern,
        grid=(tok // tm,),
        in_specs=[
            pl.BlockSpec((tm, d), lambda t: (t, 0)),
            pl.BlockSpec((1,) + mod3.shape[1:], lambda t: (t // nt, 0, 0)),
            _const_spec(wn.shape), _const_spec(bn.shape), _const_spec(wt.shape), _const_spec(bt.shape),
            _const_spec(conv_w.shape), _const_spec(conv_b.shape), _const_spec(w_mk.shape), _const_spec(w_mqt.shape),
        ],
        out_specs=out_specs,
        out_shape=out_shape,
        scratch_shapes=[pltpu.VMEM((SUBLANES, d), F32), pltpu.VMEM((FOX_HEADS, 128), F32)],
        compiler_params=pltpu.CompilerParams(dimension_semantics=("arbitrary",), vmem_limit_bytes=VMEM_LIMIT_BYTES),
        name="inproj",
    )(x2, mod3, wn, bn, wt, bt, conv_w, conv_b, w_mk, w_mqt)


def _fox_tile(nkb, qt_ref, k_ref, vt_ref, frow_ref, fcol_ref, o_ref, blk):
    hd = FOX_HEAD_DIM
    top = (nkb - 1) * blk
    qt2 = qt_ref[0]
    row = lax.broadcasted_iota(jnp.int32, qt2.shape, 0)
    causal = (lax.broadcasted_iota(jnp.int32, (blk, blk), 0)
              <= lax.broadcasted_iota(jnp.int32, (blk, blk), 1))
    kb = k_ref[0, 0:top + blk, :]
    outs = []
    for a in range(2):
        qm = jnp.where((row >= a * hd) & (row < (a + 1) * hd), qt2, jnp.zeros_like(qt2))
        f_t = frow_ref[0, 0, a:a + 1, :]
        f_s = fcol_ref[0, 0, 0:top + blk, a:a + 1]
        s = jnp.dot(kb, qm, preferred_element_type=F32) + (f_t - f_s)
        s_diag = jnp.where(causal, s[top:], -jnp.inf)
        m = jnp.max(s_diag, axis=0, keepdims=True)
        if top:
            m = jnp.maximum(m, jnp.max(s[:top], axis=0, keepdims=True))
        p_diag = jnp.exp(s_diag - m)
        l = jnp.sum(p_diag, axis=0, keepdims=True)
        acc = jnp.dot(vt_ref[0, a * hd:(a + 1) * hd, top:top + blk], p_diag.astype(BF16),
                      preferred_element_type=F32)
        if top:
            p_top = jnp.exp(s[:top] - m)
            l = l + jnp.sum(p_top, axis=0, keepdims=True)
            acc = acc + jnp.dot(vt_ref[0, a * hd:(a + 1) * hd, 0:top], p_top.astype(BF16),
                                preferred_element_type=F32)
        outs.append((acc / l).astype(BF16))
    o_ref[0] = jnp.concatenate(outs, axis=0)


def _fox_kernel(qt_ref, k_ref, vt_ref, frow_ref, fcol_ref, o_ref, *, blk):
    i = pl.program_id(2)
    for nkb in range(1, k_ref.shape[1] // blk + 1):
        pl.when(i == nkb - 1)(functools.partial(
            _fox_tile, nkb, qt_ref, k_ref, vt_ref, frow_ref, fcol_ref, o_ref, blk))


def _fox_attention(fqt, fk3, fvt, frows, fcols, *, blk):
    bsz, width, seq = fqt.shape
    pairs = FOX_HEADS // 2
    pw = 2 * FOX_HEAD_DIM
    return pl.pallas_call(
        functools.partial(_fox_kernel, blk=blk),
        grid=(bsz, pairs, seq // blk),
        in_specs=[
            pl.BlockSpec((1, pw, blk), lambda b, p, i: (b, p, i)),
            pl.BlockSpec((1, seq, pw), lambda b, p, i: (b, 0, p)),
            pl.BlockSpec((1, pw, seq), lambda b, p, i: (b, p, 0)),
            pl.BlockSpec((1, 1, 2, blk), lambda b, p, i: (b, p, 0, i)),
            pl.BlockSpec((1, 1, seq, 2), lambda b, p, i: (b, p, 0, 0)),
        ],
        out_specs=pl.BlockSpec((1, pw, blk), lambda b, p, i: (b, p, i)),
        out_shape=jax.ShapeDtypeStruct((bsz, width, seq), BF16),
        compiler_params=pltpu.CompilerParams(dimension_semantics=("parallel", "parallel", "arbitrary"),
                                             vmem_limit_bytes=VMEM_LIMIT_BYTES),
        name="fox_attention",
    )(fqt, fk3, fvt, frows, fcols)


def _mlstm_kernel(qt_ref, k_ref, vt_ref, ot_ref, rows_ref, cols_ref, g_ref, h_ref, state_ref, m_ref):
    c = pl.program_id(1)
    L = qt_ref.shape[2]
    qk = k_ref.shape[2] // MLSTM_HEADS
    dv = vt_ref.shape[1] // MLSTM_HEADS

    @pl.when(c == 0)
    def _():
        state_ref[...] = jnp.zeros_like(state_ref)
        m_ref[...] = jnp.zeros_like(m_ref)

    src = lax.broadcasted_iota(jnp.int32, (L, L), 0)
    dst = lax.broadcasted_iota(jnp.int32, (L, L), 1)
    causal = src <= dst
    ones_rows = (lax.broadcasted_iota(jnp.int32, (VAUG_PAD, L), 0) == 0).astype(BF16)

    for hd in range(MLSTM_HEADS):
        qt = qt_ref[0, hd * qk:(hd + 1) * qk, :]
        kk = k_ref[0, :, hd * qk:(hd + 1) * qk]
        vaug = jnp.concatenate([vt_ref[0, hd * dv:(hd + 1) * dv, :], ones_rows], axis=0)
        b = rows_ref[0, FOX_HEADS + hd:FOX_HEADS + hd + 1, :]
        ig = rows_ref[0, FOX_HEADS + MLSTM_HEADS + hd:FOX_HEADS + MLSTM_HEADS + hd + 1, :]
        ccol = cols_ref[0, :, FOX_HEADS + hd:FOX_HEADS + hd + 1]
        m_prev = m_ref[hd]
        state = state_ref[hd]

        g = jnp.broadcast_to(b[:, L - 1:L], (1, L))
        a = g - b + ig
        dmat = jnp.where(causal, b + ccol, -jnp.inf)
        inter = b + m_prev
        m_t = jnp.maximum(inter, jnp.max(dmat, axis=0, keepdims=True))
        scores = jnp.dot(kk, qt, preferred_element_type=F32) * jnp.exp(dmat - m_t)
        w_inter = jnp.exp(inter - m_t)
        ht = (w_inter * jnp.dot(state.astype(BF16), qt, preferred_element_type=F32)
              + jnp.dot(vaug, scores.astype(BF16), preferred_element_type=F32))
        den = ht[dv:dv + 1, :]
        hh = ht[:dv, :] / jnp.maximum(jnp.abs(den), jnp.exp(-m_t))
        mu = jnp.mean(hh, axis=0, keepdims=True)
        dlt = hh - mu
        var = jnp.mean(dlt * dlt, axis=0, keepdims=True)
        hn = dlt * lax.rsqrt(var + LN_EPS) * g_ref[hd * dv:(hd + 1) * dv, :]
        gate = jax.nn.sigmoid(ot_ref[0, hd * dv:(hd + 1) * dv, :].astype(F32))
        h_ref[0, hd * dv:(hd + 1) * dv, :] = (hn * gate).astype(BF16)

        m_new = jnp.maximum(g + m_prev, jnp.max(a, axis=1, keepdims=True))
        decay = jnp.exp(g + m_prev - m_new)
        w = jnp.exp(a - m_new)
        upd = jnp.dot((vaug.astype(F32) * w).astype(BF16), kk, preferred_element_type=F32)
        state_ref[hd] = decay[:, :qk] * state + upd
        m_ref[hd] = m_new


def _mlstm(mqt, mk3, mvt, mot, rows, cols3, gcol, *, chunk):
    bsz, qkw, seq = mqt.shape
    d = mvt.shape[1]
    dv = d // MLSTM_HEADS
    qk = qkw // MLSTM_HEADS
    fmaj = lambda r: pl.BlockSpec((1, r, chunk), lambda b, c: (b, 0, c))
    return pl.pallas_call(
        _mlstm_kernel,
        grid=(bsz, seq // chunk),
        in_specs=[
            fmaj(qkw),
            pl.BlockSpec((1, chunk, qkw), lambda b, c: (b, c, 0)),
            fmaj(d), fmaj(d), fmaj(GATE_ROWS),
            pl.BlockSpec((1, chunk, GATE_ROWS), lambda b, c: (b, c, 0)),
            _const_spec(gcol.shape),
        ],
        out_specs=fmaj(d),
        out_shape=jax.ShapeDtypeStruct((bsz, d, seq), BF16),
        scratch_shapes=[pltpu.VMEM((MLSTM_HEADS, dv + VAUG_PAD, qk), F32),
                        pltpu.VMEM((MLSTM_HEADS, 1, chunk), F32)],
        compiler_params=pltpu.CompilerParams(dimension_semantics=("parallel", "arbitrary"),
                                             vmem_limit_bytes=VMEM_LIMIT_BYTES),
        name="mlstm",
    )(mqt, mk3, mvt, mot, rows, cols3, gcol)


def _tail_kernel(x_ref, mod_ref, att_ref, hm_ref, ga_ref, gb_ref, wpa_ref, wpb_ref, wout_ref,
                 l1g_ref, l1b_ref, wfi_ref, wfd_ref, l2g_ref, l2b_ref, o_ref, *, d_ff, nsub):
    g1 = mod_ref[0, 2:3, :]
    sh2 = mod_ref[0, 3:4, :]
    sc2 = mod_ref[0, 4:5, :]
    g2 = mod_ref[0, 5:6, :]
    sub = x_ref.shape[0] // nsub
    sl = [slice(k * sub, (k + 1) * sub) for k in range(nsub)]
    at = [jnp.dot(wpa_ref[...], att_ref[0, :, s], preferred_element_type=F32) for s in sl]
    bt = [jnp.dot(wpb_ref[...], hm_ref[0, :, s], preferred_element_type=F32) for s in sl]
    yt = [(jax.nn.sigmoid(ga_ref[0, :, s].astype(F32)) * a
           + jax.nn.sigmoid(gb_ref[0, :, s].astype(F32)) * b).astype(BF16) for s, a, b in zip(sl, at, bt)]
    z = [lax.dot_general(y, wout_ref[...], (((0,), (0,)), ((), ())), preferred_element_type=F32) for y in yt]
    x1 = [_ln(ALPHA * x_ref[s, :] + g1 * zz) * l1g_ref[...] + l1b_ref[...] for s, zz in zip(sl, z)]
    h2 = [(_ln(v) * (1.0 + sc2) + sh2).astype(BF16) for v in x1]
    gu = [jnp.dot(v, wfi_ref[...], preferred_element_type=F32) for v in h2]
    act = [(_silu(v[:, :d_ff]) * v[:, d_ff:]).astype(BF16) for v in gu]
    ff = [jnp.dot(v, wfd_ref[...], preferred_element_type=F32) for v in act]
    for s, v, f in zip(sl, x1, ff):
        o_ref[s, :] = _ln(ALPHA * v + g2 * f) * l2g_ref[...] + l2b_ref[...]


def _tail(x2, mod3, att_t, hm_t, ga_t, gb_t, wpa_t, wpb_t, w_out, l1g, l1b, wfi, wfd, l2g, l2b, *, seq, tm):
    tok, d = x2.shape
    nt = seq // tm
    d_ff = wfd.shape[0]
    fmaj = lambda rows: pl.BlockSpec((1, rows, tm), lambda t: (t // nt, 0, t % nt))
    consts = (wpa_t, wpb_t, w_out, l1g, l1b, wfi, wfd, l2g, l2b)
    return pl.pallas_call(
        functools.partial(_tail_kernel, d_ff=d_ff, nsub=2),
        grid=(tok // tm,),
        in_specs=[
            pl.BlockSpec((tm, d), lambda t: (t, 0)),
            pl.BlockSpec((1,) + mod3.shape[1:], lambda t: (t // nt, 0, 0)),
            fmaj(att_t.shape[1]), fmaj(d), fmaj(d), fmaj(d),
        ] + [_const_spec(a.shape) for a in consts],
        out_specs=pl.BlockSpec((tm, d), lambda t: (t, 0)),
        out_shape=jax.ShapeDtypeStruct((tok, d), F32),
        compiler_params=pltpu.CompilerParams(dimension_semantics=("parallel",), vmem_limit_bytes=VMEM_LIMIT_BYTES),
        name="tail",
    )(x2, mod3, att_t, hm_t, ga_t, gb_t, *consts)


def _pick(n, pref):
    while n % pref:
        pref //= 2
    return pref


def kernel(x, c, w_ada, b_ada, w_in, b_in, conv_w, conv_b, w_mq, w_mk, mh_norm_g, w_pa, w_pb, w_out,
           ln1_g, ln1_b, w_ffn_in, w_ffn_down, ln2_g, ln2_b):
    bsz, seq, d = x.shape
    tok = bsz * seq
    chunk = _pick(seq, 256)
    fox_blk = _pick(seq, 512)
    tm_in = _pick(seq, 512)
    tm_tail = _pick(seq, 512)

    x2 = x.reshape(tok, d)
    for l in range(DEPTH):
        mod3 = _modulation(c, w_ada[l], b_ada[l]).reshape(bsz, 6, d)

        splits = (FOX_WIDTH, FOX_WIDTH, FOX_WIDTH, FOX_HEADS, d, d, MLSTM_HEADS, MLSTM_HEADS, d, d, d)
        offs = [0]
        for s in splits:
            offs.append(offs[-1] + s)
        col = lambda i: (w_in[l][:, offs[i]:offs[i + 1]], b_in[l][offs[i]:offs[i + 1]])
        (wfq, bfq), (wfk, bfk), (wfv, bfv), (wff, bff), (wmu, bmu), (wmv, bmv), (wmi, bmi), (wmf, bmf), \
            (wmo, bmo), (wga, bga), (wgb, bgb) = [col(i) for i in range(len(splits))]
        scale = FOX_HEAD_DIM ** -0.5
        wn = jnp.concatenate([wfk, wmu], axis=1).astype(BF16)
        bn = jnp.concatenate([bfk, bmu])[None, :]
        wt = jnp.concatenate([wfq * scale, wfv, wmv, wmo, wga, wgb, wff, wmi, wmf], axis=1).T.astype(BF16)
        bt = jnp.concatenate([bfq * scale, bfv, bmv, bmo, bga, bgb, bff, bmi, bmf])[:, None]

        fk, mk, fqt, fvt, mqt, mvt, mot, gat, gbt, rows, cols = _inproj(
            x2, mod3, wn, bn, wt, bt, conv_w[l], conv_b[l][None, :], w_mk[l].astype(BF16),
            jnp.swapaxes(w_mq[l], 1, 2).astype(BF16), bsz=bsz, seq=seq, tm=tm_in, chunk=chunk)

        pairs = FOX_HEADS // 2
        frows = rows[:, :FOX_HEADS].reshape(bsz, pairs, 2, seq)
        fcols = cols[:, :FOX_HEADS].reshape(bsz, seq, pairs, 2).transpose(0, 2, 1, 3)
        att_t = _fox_attention(fqt, fk.reshape(bsz, seq, FOX_WIDTH), fvt, frows, fcols, blk=fox_blk)

        hm_t = _mlstm(mqt, mk.reshape(bsz, seq, -1), mvt, mot, rows, cols.reshape(bsz, seq, GATE_ROWS),
                      mh_norm_g[l][:, None], chunk=chunk)

        x2 = _tail(x2, mod3, att_t, hm_t, gat, gbt, w_pa[l].T.astype(BF16), w_pb[l].T.astype(BF16),
                   w_out[l].astype(BF16), ln1_g[l][None, :], ln1_b[l][None, :], w_ffn_in[l].astype(BF16),
                   w_ffn_down[l].astype(BF16), ln2_g[l][None, :], ln2_b[l][None, :], seq=seq, tm=tm_tail)
    return x2.reshape(bsz, seq, d)
```

```python
import functools

import jax
import jax.numpy as jnp
from jax import lax
from jax.experimental import pallas as pl
from jax.experimental.pallas import tpu as pltpu

F32 = jnp.float32
BF16 = jnp.bfloat16

FOX_HEADS = 8
FOX_HEAD_DIM = 64
FOX_WIDTH = FOX_HEADS * FOX_HEAD_DIM
MLSTM_HEADS = 4
CONV_WIDTH = 4
LN_EPS = 1e-5
DEPTH = 1
ALPHA = (2.0 * DEPTH) ** 0.25
LOG2E = 1.4426950408889634

VMEM_LIMIT_BYTES = 56 * 1024 * 1024
SUBLANES = 8
GATE_ROWS = 16
VAUG_PAD = 16


def _const_spec(shape):
    nd = len(shape)
    return pl.BlockSpec(shape, lambda *_: (0,) * nd, pipeline_mode=pl.Buffered(1))


def _ln(v):
    mu = jnp.mean(v, axis=-1, keepdims=True)
    d = v - mu
    var = jnp.mean(d * d, axis=-1, keepdims=True)
    return d * lax.rsqrt(var + LN_EPS)


def _silu(v):
    return v * jax.nn.sigmoid(v)


def _mod_kernel(c_ref, w_ref, b_ref, o_ref):
    a = _silu(c_ref[...]).astype(BF16)
    o_ref[...] = jnp.dot(a, w_ref[...].astype(BF16), preferred_element_type=F32) + b_ref[...]


def _modulation(c, w_ada, b_ada):
    bsz, d = c.shape
    n = w_ada.shape[1]
    return pl.pallas_call(
        _mod_kernel,
        grid=(n // d,),
        in_specs=[
            pl.BlockSpec((bsz, d), lambda j: (0, 0)),
            pl.BlockSpec((d, d), lambda j: (0, j)),
            pl.BlockSpec((1, d), lambda j: (0, j)),
        ],
        out_specs=pl.BlockSpec((bsz, d), lambda j: (0, j)),
        out_shape=jax.ShapeDtypeStruct((bsz, n), F32),
        name="modulation",
    )(c, w_ada, b_ada.reshape(1, n))


def _prefix_sum_lanes(v, width):
    pos = lax.broadcasted_iota(jnp.int32, v.shape, 1) & (width - 1)
    d = 1
    while d < width:
        v = v + jnp.where(pos >= d, pltpu.roll(v, d, axis=1), 0.0)
        d *= 2
    return v


def _inproj_kernel(x_ref, mod_ref, wn_ref, bn_ref, wt_ref, bt_ref, cw_ref, cb_ref, wmk_ref, wmqt_ref,
                   fk_ref, mk_ref, fqt_ref, fvt_ref, mqt_ref, mvt_ref, mot_ref, gat_ref, gbt_ref,
                   rows_ref, cols_ref, prev_ref, fcarry_ref, *, tiles_per_seq, chunk, d_model):
    tm = x_ref.shape[0]
    first = (pl.program_id(0) % tiles_per_seq) == 0
    sh1 = mod_ref[0, 0:1, :]
    sc1 = mod_ref[0, 1:2, :]
    h = (_ln(x_ref[...]) * (1.0 + sc1) + sh1).astype(BF16)

    @pl.when(first)
    def _():
        prev_ref[...] = jnp.zeros_like(prev_ref)
        fcarry_ref[...] = jnp.zeros_like(fcarry_ref)

    yn = jnp.dot(h, wn_ref[...], preferred_element_type=F32) + bn_ref[...]
    fk_ref[...] = yn[:, :FOX_WIDTH].astype(BF16)
    mu = yn[:, FOX_WIDTH:]

    def seg(r0, n):
        y = lax.dot_general(wt_ref[r0:r0 + n, :], h, (((1,), (1,)), ((), ())), preferred_element_type=F32)
        return y + bt_ref[r0:r0 + n, :]

    g = seg(wt_ref.shape[0] - GATE_ROWS, GATE_ROWS)
    ls = jax.nn.log_sigmoid(g)
    fcum = _prefix_sum_lanes(ls[0:FOX_HEADS] * LOG2E, tm) + fcarry_ref[:, 0:1]
    fcarry_ref[...] = jnp.broadcast_to(fcum[:, tm - 1:tm], fcarry_ref.shape)
    bcum = _prefix_sum_lanes(ls[FOX_HEADS:], chunk)[MLSTM_HEADS:]
    ig = g[FOX_HEADS:FOX_HEADS + MLSTM_HEADS]
    rows_ref[0] = jnp.concatenate([fcum, bcum, ig], axis=0)
    colsT = jnp.concatenate([fcum, ig - bcum, jnp.zeros_like(ig)], axis=0)
    cols_ref[...] = colsT.T

    r = 0
    for ref, n in ((fqt_ref, FOX_WIDTH), (fvt_ref, FOX_WIDTH), (mvt_ref, d_model), (mot_ref, d_model),
                   (gat_ref, d_model), (gbt_ref, d_model)):
        ref[0] = seg(r, n).astype(BF16)
        r += n

    full = jnp.concatenate([prev_ref[...], mu], axis=0)
    u = cb_ref[...] + cw_ref[0:1, :] * full[SUBLANES - 3:SUBLANES - 3 + tm]
    for j in range(1, CONV_WIDTH):
        off = SUBLANES - (CONV_WIDTH - 1) + j
        u = u + cw_ref[j:j + 1, :] * full[off:off + tm]
    prev_ref[...] = mu[tm - SUBLANES:, :]
    u = _silu(u).astype(BF16)

    inner = d_model // MLSTM_HEADS
    qk = wmk_ref.shape[2]
    for hd in range(MLSTM_HEADS):
        uh = u[:, hd * inner:(hd + 1) * inner]
        mk_ref[:, hd * qk:(hd + 1) * qk] = jnp.dot(uh, wmk_ref[hd], preferred_element_type=F32).astype(BF16)
        qt = lax.dot_general(wmqt_ref[hd], uh, (((1,), (1,)), ((), ())), preferred_element_type=F32)
        mqt_ref[0, hd * qk:(hd + 1) * qk, :] = (qt * (qk ** -0.5)).astype(BF16)


def _inproj(x2, mod3, wn, bn, wt, bt, conv_w, conv_b, w_mk, w_mqt, *, bsz, seq, tm, chunk):
    tok, d = x2.shape
    nt = seq // tm
    qkw = w_mk.shape[0] * w_mk.shape[2]
    tmaj = lambda width: pl.BlockSpec((tm, width), lambda t: (t, 0))
    fmaj = lambda rows: pl.BlockSpec((1, rows, tm), lambda t: (t // nt, 0, t % nt))
    out_shape = (
        jax.ShapeDtypeStruct((tok, FOX_WIDTH), BF16),
        jax.ShapeDtypeStruct((tok, qkw), BF16),
        jax.ShapeDtypeStruct((bsz, FOX_WIDTH, seq), BF16),
        jax.ShapeDtypeStruct((bsz, FOX_WIDTH, seq), BF16),
        jax.ShapeDtypeStruct((bsz, qkw, seq), BF16),
        jax.ShapeDtypeStruct((bsz, d, seq), BF16),
        jax.ShapeDtypeStruct((bsz, d, seq), BF16),
        jax.ShapeDtypeStruct((bsz, d, seq), BF16),
        jax.ShapeDtypeStruct((bsz, d, seq), BF16),
        jax.ShapeDtypeStruct((bsz, GATE_ROWS, seq), F32),
        jax.ShapeDtypeStruct((tok, GATE_ROWS), F32),
    )
    out_specs = (tmaj(FOX_WIDTH), tmaj(qkw), fmaj(FOX_WIDTH), fmaj(FOX_WIDTH), fmaj(qkw),
                 fmaj(d), fmaj(d), fmaj(d), fmaj(d), fmaj(GATE_ROWS), tmaj(GATE_ROWS))
    kern = functools.partial(_inproj_kernel, tiles_per_seq=nt, chunk=chunk, d_model=d)
    return pl.pallas_call(
        kern,
        grid=(tok // tm,),
        in_specs=[
            pl.BlockSpec((tm, d), lambda t: (t, 0)),
            pl.BlockSpec((1,) + mod3.shape[1:], lambda t: (t // nt, 0, 0)),
            _const_spec(wn.shape), _const_spec(bn.shape), _const_spec(wt.shape), _const_spec(bt.shape),
            _const_spec(conv_w.shape), _const_spec(conv_b.shape), _const_spec(w_mk.shape), _const_spec(w_mqt.shape),
        ],
        out_specs=out_specs,
        out_shape=out_shape,
        scratch_shapes=[pltpu.VMEM((SUBLANES, d), F32), pltpu.VMEM((FOX_HEADS, 128), F32)],
        compiler_params=pltpu.CompilerParams(dimension_semantics=("arbitrary",), vmem_limit_bytes=VMEM_LIMIT_BYTES),
        name="inproj",
    )(x2, mod3, wn, bn, wt, bt, conv_w, conv_b, w_mk, w_mqt)


def _fox_tile(qi, qt_ref, k_ref, vt_ref, frow_ref, fcol_ref, o_ref, blk):
    hd = FOX_HEAD_DIM
    half = blk // 2
    q0 = qi * blk
    main = q0 + half
    keys = q0 + blk
    qt2 = qt_ref[0, :, q0:keys]
    row = lax.broadcasted_iota(jnp.int32, qt2.shape, 0)
    tri_main = (lax.broadcasted_iota(jnp.int32, (half, blk), 0)
                <= lax.broadcasted_iota(jnp.int32, (half, blk), 1))
    tri_last = tri_main[:, :half]
    ones = (lax.broadcasted_iota(jnp.int32, (VAUG_PAD, keys), 0) == 0).astype(BF16)
    kb = k_ref[0, 0:keys, :]
    outs = []
    for a in range(2):
        qm = jnp.where((row >= a * hd) & (row < (a + 1) * hd), qt2, jnp.zeros_like(qt2))
        f_t = frow_ref[0, 0, a:a + 1, q0:keys]
        f_s = fcol_ref[0, 0, 0:keys, a:a + 1]
        vaug = jnp.concatenate([vt_ref[0, a * hd:(a + 1) * hd, 0:keys], ones], axis=0)
        s_main = jnp.dot(kb[:main], qm, preferred_element_type=F32) + (f_t - f_s[:main])
        s_last = (jnp.dot(kb[main:], qm[:, half:], preferred_element_type=F32)
                  + (f_t[:, half:] - f_s[main:]))
        s_edge = jnp.where(tri_main, s_main[q0:], -jnp.inf)
        s_last = jnp.where(tri_last, s_last, -jnp.inf)
        m = jnp.max(s_edge, axis=0, keepdims=True)
        if q0:
            m = jnp.maximum(m, jnp.max(s_main[:q0], axis=0, keepdims=True))
        m_hi = jnp.maximum(m[:, half:], jnp.max(s_last, axis=0, keepdims=True))
        m = jnp.concatenate([m[:, :half], m_hi], axis=1)
        acc = jnp.dot(vaug[:, q0:main], jnp.exp2(s_edge - m).astype(BF16), preferred_element_type=F32)
        if q0:
            acc = acc + jnp.dot(vaug[:, :q0], jnp.exp2(s_main[:q0] - m).astype(BF16),
                                preferred_element_type=F32)
        acc_hi = jnp.dot(vaug[:, main:], jnp.exp2(s_last - m_hi).astype(BF16), preferred_element_type=F32)
        acc = jnp.concatenate([acc[:, :half], acc[:, half:] + acc_hi], axis=1)
        outs.append((acc[:hd] / acc[hd:hd + 1]).astype(BF16))
    o_ref[0, :, q0:keys] = jnp.concatenate(outs, axis=0)


def _fox_kernel(qt_ref, k_ref, vt_ref, frow_ref, fcol_ref, o_ref, *, blk):
    for qi in range(k_ref.shape[1] // blk):
        _fox_tile(qi, qt_ref, k_ref, vt_ref, frow_ref, fcol_ref, o_ref, blk)


def _fox_attention(fqt, fk3, fvt, frows, fcols, *, blk):
    bsz, width, seq = fqt.shape
    pairs = FOX_HEADS // 2
    pw = 2 * FOX_HEAD_DIM
    return pl.pallas_call(
        functools.partial(_fox_kernel, blk=blk),
        grid=(bsz, pairs),
        in_specs=[
            pl.BlockSpec((1, pw, seq), lambda b, p: (b, p, 0)),
            pl.BlockSpec((1, seq, pw), lambda b, p: (b, 0, p)),
            pl.BlockSpec((1, pw, seq), lambda b, p: (b, p, 0)),
            pl.BlockSpec((1, 1, 2, seq), lambda b, p: (b, p, 0, 0)),
            pl.BlockSpec((1, 1, seq, 2), lambda b, p: (b, p, 0, 0)),
        ],
        out_specs=pl.BlockSpec((1, pw, seq), lambda b, p: (b, p, 0)),
        out_shape=jax.ShapeDtypeStruct((bsz, width, seq), BF16),
        compiler_params=pltpu.CompilerParams(dimension_semantics=("parallel", "parallel"),
                                             vmem_limit_bytes=VMEM_LIMIT_BYTES),
        name="fox_attention",
    )(fqt, fk3, fvt, frows, fcols)


def _mlstm_kernel(qt_ref, k_ref, vt_ref, ot_ref, rows_ref, cols_ref, g_ref, h_ref, state_ref, m_ref):
    c = pl.program_id(1)
    L = qt_ref.shape[2]
    qk = k_ref.shape[2] // MLSTM_HEADS
    dv = vt_ref.shape[1] // MLSTM_HEADS

    @pl.when(c == 0)
    def _():
        state_ref[...] = jnp.zeros_like(state_ref)
        m_ref[...] = jnp.zeros_like(m_ref)

    src = lax.broadcasted_iota(jnp.int32, (L, L), 0)
    dst = lax.broadcasted_iota(jnp.int32, (L, L), 1)
    causal = src <= dst
    ones_rows = (lax.broadcasted_iota(jnp.int32, (VAUG_PAD, L), 0) == 0).astype(BF16)

    for bi, hd in [(bi, hd) for bi in range(qt_ref.shape[0]) for hd in range(MLSTM_HEADS)]:
        qt = qt_ref[bi, hd * qk:(hd + 1) * qk, :]
        kk = k_ref[bi, :, hd * qk:(hd + 1) * qk]
        vaug = jnp.concatenate([vt_ref[bi, hd * dv:(hd + 1) * dv, :], ones_rows], axis=0)
        b = rows_ref[bi, FOX_HEADS + hd:FOX_HEADS + hd + 1, :]
        ig = rows_ref[bi, FOX_HEADS + MLSTM_HEADS + hd:FOX_HEADS + MLSTM_HEADS + hd + 1, :]
        ccol = cols_ref[bi, :, FOX_HEADS + hd:FOX_HEADS + hd + 1]
        m_prev = m_ref[bi, hd]
        state = state_ref[bi, hd]

        g = jnp.broadcast_to(b[:, L - 1:L], (1, L))
        a = g - b + ig
        dmat = jnp.where(causal, b + ccol, -jnp.inf)
        inter = b + m_prev
        m_t = jnp.maximum(inter, jnp.max(dmat, axis=0, keepdims=True))
        scores = jnp.dot(kk, qt, preferred_element_type=F32) * jnp.exp(dmat - m_t)
        w_inter = jnp.exp(inter - m_t)
        ht = (w_inter * jnp.dot(state.astype(BF16), qt, preferred_element_type=F32)
              + jnp.dot(vaug, scores.astype(BF16), preferred_element_type=F32))
        den = ht[dv:dv + 1, :]
        hh = ht[:dv, :] / jnp.maximum(jnp.abs(den), jnp.exp(-m_t))
        mu = jnp.mean(hh, axis=0, keepdims=True)
        dlt = hh - mu
        var = jnp.mean(dlt * dlt, axis=0, keepdims=True)
        hn = dlt * lax.rsqrt(var + LN_EPS) * g_ref[hd * dv:(hd + 1) * dv, :]
        gate = jax.nn.sigmoid(ot_ref[bi, hd * dv:(hd + 1) * dv, :].astype(F32))
        h_ref[bi, hd * dv:(hd + 1) * dv, :] = (hn * gate).astype(BF16)

        m_new = jnp.maximum(g + m_prev, jnp.max(a, axis=1, keepdims=True))
        decay = jnp.exp(g + m_prev - m_new)
        w = jnp.exp(a - m_new)
        upd = jnp.dot((vaug.astype(F32) * w).astype(BF16), kk, preferred_element_type=F32)
        state_ref[bi, hd] = decay[:, :qk] * state + upd
        m_ref[bi, hd] = m_new


def _mlstm(mqt, mk3, mvt, mot, rows, cols3, gcol, *, chunk, nb):
    bsz, qkw, seq = mqt.shape
    d = mvt.shape[1]
    dv = d // MLSTM_HEADS
    qk = qkw // MLSTM_HEADS
    fmaj = lambda r: pl.BlockSpec((nb, r, chunk), lambda b, c: (b, 0, c))
    return pl.pallas_call(
        _mlstm_kernel,
        grid=(bsz // nb, seq // chunk),
        in_specs=[
            fmaj(qkw),
            pl.BlockSpec((nb, chunk, qkw), lambda b, c: (b, c, 0)),
            fmaj(d), fmaj(d), fmaj(GATE_ROWS),
            pl.BlockSpec((nb, chunk, GATE_ROWS), lambda b, c: (b, c, 0)),
            _const_spec(gcol.shape),
        ],
        out_specs=fmaj(d),
        out_shape=jax.ShapeDtypeStruct((bsz, d, seq), BF16),
        scratch_shapes=[pltpu.VMEM((nb, MLSTM_HEADS, dv + VAUG_PAD, qk), F32),
                        pltpu.VMEM((nb, MLSTM_HEADS, 1, chunk), F32)],
        compiler_params=pltpu.CompilerParams(dimension_semantics=("parallel", "arbitrary"),
                                             vmem_limit_bytes=VMEM_LIMIT_BYTES),
        name="mlstm",
    )(mqt, mk3, mvt, mot, rows, cols3, gcol)


def _tail_kernel(x_ref, mod_ref, att_ref, hm_ref, ga_ref, gb_ref, wpa_ref, wpb_ref, wout_ref,
                 l1g_ref, l1b_ref, wfi_ref, wfd_ref, l2g_ref, l2b_ref, o_ref, *, d_ff, nsub):
    g1 = mod_ref[0, 2:3, :]
    sh2 = mod_ref[0, 3:4, :]
    sc2 = mod_ref[0, 4:5, :]
    g2 = mod_ref[0, 5:6, :]
    sub = x_ref.shape[0] // nsub
    sl = [slice(k * sub, (k + 1) * sub) for k in range(nsub)]
    at = [jnp.dot(wpa_ref[...], att_ref[0, :, s], preferred_element_type=F32) for s in sl]
    bt = [jnp.dot(wpb_ref[...], hm_ref[0, :, s], preferred_element_type=F32) for s in sl]
    yt = [(jax.nn.sigmoid(ga_ref[0, :, s].astype(F32)) * a
           + jax.nn.sigmoid(gb_ref[0, :, s].astype(F32)) * b).astype(BF16) for s, a, b in zip(sl, at, bt)]
    z = [lax.dot_general(y, wout_ref[...], (((0,), (0,)), ((), ())), preferred_element_type=F32) for y in yt]
    x1 = [_ln(ALPHA * x_ref[s, :] + g1 * zz) * l1g_ref[...] + l1b_ref[...] for s, zz in zip(sl, z)]
    h2 = [(_ln(v) * (1.0 + sc2) + sh2).astype(BF16) for v in x1]
    gu = [jnp.dot(v, wfi_ref[...], preferred_element_type=F32) for v in h2]
    act = [(_silu(v[:, :d_ff]) * v[:, d_ff:]).astype(BF16) for v in gu]
    ff = [jnp.dot(v, wfd_ref[...], preferred_element_type=F32) for v in act]
    for s, v, f in zip(sl, x1, ff):
        o_ref[s, :] = _ln(ALPHA * v + g2 * f) * l2g_ref[...] + l2b_ref[...]


def _tail(x2, mod3, att_t, hm_t, ga_t, gb_t, wpa_t, wpb_t, w_out, l1g, l1b, wfi, wfd, l2g, l2b, *, seq, tm):
    tok, d = x2.shape
    nt = seq // tm
    d_ff = wfd.shape[0]
    fmaj = lambda rows: pl.BlockSpec((1, rows, tm), lambda t: (t // nt, 0, t % nt))
    consts = (wpa_t, wpb_t, w_out, l1g, l1b, wfi, wfd, l2g, l2b)
    return pl.pallas_call(
        functools.partial(_tail_kernel, d_ff=d_ff, nsub=2),
        grid=(tok // tm,),
        in_specs=[
            pl.BlockSpec((tm, d), lambda t: (t, 0)),
            pl.BlockSpec((1,) + mod3.shape[1:], lambda t: (t // nt, 0, 0)),
            fmaj(att_t.shape[1]), fmaj(d), fmaj(d), fmaj(d),
        ] + [_const_spec(a.shape) for a in consts],
        out_specs=pl.BlockSpec((tm, d), lambda t: (t, 0)),
        out_shape=jax.ShapeDtypeStruct((tok, d), F32),
        compiler_params=pltpu.CompilerParams(dimension_semantics=("parallel",), vmem_limit_bytes=VMEM_LIMIT_BYTES),
        name="tail",
    )(x2, mod3, att_t, hm_t, ga_t, gb_t, *consts)


def _pick(n, pref):
    while n % pref:
        pref //= 2
    return pref


def kernel(x, c, w_ada, b_ada, w_in, b_in, conv_w, conv_b, w_mq, w_mk, mh_norm_g, w_pa, w_pb, w_out,
           ln1_g, ln1_b, w_ffn_in, w_ffn_down, ln2_g, ln2_b):
    bsz, seq, d = x.shape
    tok = bsz * seq
    chunk = _pick(seq, 256)
    fox_blk = _pick(seq, 512)
    tm_in = _pick(seq, 512)
    tm_tail = _pick(seq, 512)

    x2 = x.reshape(tok, d)
    for l in range(DEPTH):
        mod3 = _modulation(c, w_ada[l], b_ada[l]).reshape(bsz, 6, d)

        splits = (FOX_WIDTH, FOX_WIDTH, FOX_WIDTH, FOX_HEADS, d, d, MLSTM_HEADS, MLSTM_HEADS, d, d, d)
        offs = [0]
        for s in splits:
            offs.append(offs[-1] + s)
        col = lambda i: (w_in[l][:, offs[i]:offs[i + 1]], b_in[l][offs[i]:offs[i + 1]])
        (wfq, bfq), (wfk, bfk), (wfv, bfv), (wff, bff), (wmu, bmu), (wmv, bmv), (wmi, bmi), (wmf, bmf), \
            (wmo, bmo), (wga, bga), (wgb, bgb) = [col(i) for i in range(len(splits))]
        scale = FOX_HEAD_DIM ** -0.5 * LOG2E
        wn = jnp.concatenate([wfk, wmu], axis=1).astype(BF16)
        bn = jnp.concatenate([bfk, bmu])[None, :]
        wt = jnp.concatenate([wfq * scale, wfv, wmv, wmo, wga, wgb, wff, wmi, wmf], axis=1).T.astype(BF16)
        bt = jnp.concatenate([bfq * scale, bfv, bmv, bmo, bga, bgb, bff, bmi, bmf])[:, None]

        fk, mk, fqt, fvt, mqt, mvt, mot, gat, gbt, rows, cols = _inproj(
            x2, mod3, wn, bn, wt, bt, conv_w[l], conv_b[l][None, :], w_mk[l].astype(BF16),
            jnp.swapaxes(w_mq[l], 1, 2).astype(BF16), bsz=bsz, seq=seq, tm=tm_in, chunk=chunk)

        pairs = FOX_HEADS // 2
        frows = rows[:, :FOX_HEADS].reshape(bsz, pairs, 2, seq)
        fcols = cols[:, :FOX_HEADS].reshape(bsz, seq, pairs, 2).transpose(0, 2, 1, 3)
        att_t = _fox_attention(fqt, fk.reshape(bsz, seq, FOX_WIDTH), fvt, frows, fcols, blk=fox_blk)

        hm_t = _mlstm(mqt, mk.reshape(bsz, seq, -1), mvt, mot, rows, cols.reshape(bsz, seq, GATE_ROWS),
                      mh_norm_g[l][:, None], chunk=chunk, nb=2 if bsz % 2 == 0 else 1)

        x2 = _tail(x2, mod3, att_t, hm_t, gat, gbt, w_pa[l].T.astype(BF16), w_pb[l].T.astype(BF16),
                   w_out[l].astype(BF16), ln1_g[l][None, :], ln1_b[l][None, :], w_ffn_in[l].astype(BF16),
                   w_ffn_down[l].astype(BF16), ln2_g[l][None, :], ln2_b[l][None, :], seq=seq, tm=tm_tail)
    return x2.reshape(bsz, seq, d)
```

```python
import functools

import jax
import jax.numpy as jnp
from jax import lax
from jax.experimental import pallas as pl
from jax.experimental.pallas import tpu as pltpu

F32 = jnp.float32
BF16 = jnp.bfloat16

FOX_HEADS = 8
FOX_HEAD_DIM = 64
FOX_WIDTH = FOX_HEADS * FOX_HEAD_DIM
MLSTM_HEADS = 4
CONV_WIDTH = 4
LN_EPS = 1e-5
DEPTH = 1
ALPHA = (2.0 * DEPTH) ** 0.25
LOG2E = 1.4426950408889634

VMEM_LIMIT_BYTES = 56 * 1024 * 1024
SUBLANES = 8
GATE_ROWS = 16
VAUG_PAD = 16
FOX_AUG = 16
FOX_SCORE_LEAD = 3


def _const_spec(shape):
    nd = len(shape)
    return pl.BlockSpec(shape, lambda *_: (0,) * nd, pipeline_mode=pl.Buffered(1))


def _ln(v):
    mu = jnp.mean(v, axis=-1, keepdims=True)
    d = v - mu
    var = jnp.mean(d * d, axis=-1, keepdims=True)
    return d * lax.rsqrt(var + LN_EPS)


def _silu(v):
    return v * jax.nn.sigmoid(v)


def _mod_kernel(c_ref, w_ref, b_ref, o_ref):
    a = _silu(c_ref[...]).astype(BF16)
    o_ref[...] = jnp.dot(a, w_ref[...].astype(BF16), preferred_element_type=F32) + b_ref[...]


def _modulation(c, w_ada, b_ada):
    bsz, d = c.shape
    n = w_ada.shape[1]
    return pl.pallas_call(
        _mod_kernel,
        grid=(n // d,),
        in_specs=[
            pl.BlockSpec((bsz, d), lambda j: (0, 0)),
            pl.BlockSpec((d, d), lambda j: (0, j)),
            pl.BlockSpec((1, d), lambda j: (0, j)),
        ],
        out_specs=pl.BlockSpec((bsz, d), lambda j: (0, j)),
        out_shape=jax.ShapeDtypeStruct((bsz, n), F32),
        name="modulation",
    )(c, w_ada, b_ada.reshape(1, n))


def _prefix_sum_lanes(v, width):
    pos = lax.broadcasted_iota(jnp.int32, v.shape, 1) & (width - 1)
    d = 1
    while d < width:
        v = v + jnp.where(pos >= d, pltpu.roll(v, d, axis=1), 0.0)
        d *= 2
    return v


def _inproj_kernel(x_ref, mod_ref, wn_ref, bn_ref, wt_ref, bt_ref, cw_ref, cb_ref, wmk_ref, wmqt_ref,
                   fk_ref, mk_ref, fqt_ref, fvt_ref, mqt_ref, mvt_ref, mot_ref, gat_ref, gbt_ref,
                   rows_ref, cols_ref, kf_ref, qf_ref, prev_ref, fcarry_ref, *, tiles_per_seq, chunk, d_model):
    tm = x_ref.shape[0]
    first = (pl.program_id(0) % tiles_per_seq) == 0
    sh1 = mod_ref[0, 0:1, :]
    sc1 = mod_ref[0, 1:2, :]
    h = (_ln(x_ref[...]) * (1.0 + sc1) + sh1).astype(BF16)

    @pl.when(first)
    def _():
        prev_ref[...] = jnp.zeros_like(prev_ref)
        fcarry_ref[...] = jnp.zeros_like(fcarry_ref)

    yn = jnp.dot(h, wn_ref[...], preferred_element_type=F32) + bn_ref[...]
    fk_ref[...] = yn[:, :FOX_WIDTH].astype(BF16)
    mu = yn[:, FOX_WIDTH:]

    def seg(r0, n):
        y = lax.dot_general(wt_ref[r0:r0 + n, :], h, (((1,), (1,)), ((), ())), preferred_element_type=F32)
        return y + bt_ref[r0:r0 + n, :]

    g = seg(wt_ref.shape[0] - GATE_ROWS, GATE_ROWS)
    ls = jax.nn.log_sigmoid(g)
    fcum = _prefix_sum_lanes(ls[0:FOX_HEADS] * LOG2E, tm) + fcarry_ref[:, 0:1]
    fcarry_ref[...] = jnp.broadcast_to(fcum[:, tm - 1:tm], fcarry_ref.shape)
    bcum = _prefix_sum_lanes(ls[FOX_HEADS:], chunk)[MLSTM_HEADS:]
    ig = g[FOX_HEADS:FOX_HEADS + MLSTM_HEADS]
    rows_ref[0] = jnp.concatenate([fcum, bcum, ig], axis=0)
    colsT = jnp.concatenate([fcum, ig - bcum, jnp.zeros_like(ig)], axis=0)
    cols_ref[...] = colsT.T

    f_hi = fcum.astype(BF16).astype(F32)
    f_mid = (fcum - f_hi).astype(BF16).astype(F32)
    f_lo = (fcum - f_hi - f_mid).astype(BF16).astype(F32)
    j = lax.broadcasted_iota(jnp.int32, (FOX_AUG, tm), 0)
    for hd in range(FOX_HEADS):
        odd = hd % 2
        ones_at = (j >= 3 + 3 * odd) & (j < 6 + 3 * odd)
        blk16 = jnp.where(j == 0, f_hi[hd:hd + 1], jnp.where(j == 1, f_mid[hd:hd + 1], jnp.where(
            j == 2, f_lo[hd:hd + 1], jnp.where(ones_at, 1.0, 0.0))))
        qf_ref[0, hd * FOX_AUG:(hd + 1) * FOX_AUG, :] = blk16.astype(BF16)
    kf_ref[...] = jnp.zeros_like(kf_ref)
    for pr in range(FOX_HEADS // 2):
        ea, eb = 2 * pr, 2 * pr + 1
        blk16 = jnp.where(j < 3, 1.0, 0.0)
        for r, src in ((3, f_hi[ea:ea + 1]), (4, f_mid[ea:ea + 1]), (5, f_lo[ea:ea + 1]),
                       (6, f_hi[eb:eb + 1]), (7, f_mid[eb:eb + 1]), (8, f_lo[eb:eb + 1])):
            blk16 = jnp.where(j == r, -src, blk16)
        kf_ref[:, pr * 2 * FOX_HEAD_DIM:pr * 2 * FOX_HEAD_DIM + FOX_AUG] = blk16.T.astype(BF16)

    r = 0
    for ref, n in ((fqt_ref, FOX_WIDTH), (fvt_ref, FOX_WIDTH), (mvt_ref, d_model), (mot_ref, d_model),
                   (gat_ref, d_model), (gbt_ref, d_model)):
        ref[0] = seg(r, n).astype(BF16)
        r += n

    full = jnp.concatenate([prev_ref[...], mu], axis=0)
    u = cb_ref[...] + cw_ref[0:1, :] * full[SUBLANES - 3:SUBLANES - 3 + tm]
    for j in range(1, CONV_WIDTH):
        off = SUBLANES - (CONV_WIDTH - 1) + j
        u = u + cw_ref[j:j + 1, :] * full[off:off + tm]
    prev_ref[...] = mu[tm - SUBLANES:, :]
    u = _silu(u).astype(BF16)

    inner = d_model // MLSTM_HEADS
    qk = wmk_ref.shape[2]
    for hd in range(MLSTM_HEADS):
        uh = u[:, hd * inner:(hd + 1) * inner]
        mk_ref[:, hd * qk:(hd + 1) * qk] = jnp.dot(uh, wmk_ref[hd], preferred_element_type=F32).astype(BF16)
        qt = lax.dot_general(wmqt_ref[hd], uh, (((1,), (1,)), ((), ())), preferred_element_type=F32)
        mqt_ref[0, hd * qk:(hd + 1) * qk, :] = (qt * (qk ** -0.5)).astype(BF16)


def _inproj(x2, mod3, wn, bn, wt, bt, conv_w, conv_b, w_mk, w_mqt, *, bsz, seq, tm, chunk):
    tok, d = x2.shape
    nt = seq // tm
    qkw = w_mk.shape[0] * w_mk.shape[2]
    tmaj = lambda width: pl.BlockSpec((tm, width), lambda t: (t, 0))
    fmaj = lambda rows: pl.BlockSpec((1, rows, tm), lambda t: (t // nt, 0, t % nt))
    out_shape = (
        jax.ShapeDtypeStruct((tok, FOX_WIDTH), BF16),
        jax.ShapeDtypeStruct((tok, qkw), BF16),
        jax.ShapeDtypeStruct((bsz, FOX_WIDTH, seq), BF16),
        jax.ShapeDtypeStruct((bsz, FOX_WIDTH, seq), BF16),
        jax.ShapeDtypeStruct((bsz, qkw, seq), BF16),
        jax.ShapeDtypeStruct((bsz, d, seq), BF16),
        jax.ShapeDtypeStruct((bsz, d, seq), BF16),
        jax.ShapeDtypeStruct((bsz, d, seq), BF16),
        jax.ShapeDtypeStruct((bsz, d, seq), BF16),
        jax.ShapeDtypeStruct((bsz, GATE_ROWS, seq), F32),
        jax.ShapeDtypeStruct((tok, GATE_ROWS), F32),
        jax.ShapeDtypeStruct((tok, FOX_WIDTH), BF16),
        jax.ShapeDtypeStruct((bsz, FOX_HEADS * FOX_AUG, seq), BF16),
    )
    out_specs = (tmaj(FOX_WIDTH), tmaj(qkw), fmaj(FOX_WIDTH), fmaj(FOX_WIDTH), fmaj(qkw),
                 fmaj(d), fmaj(d), fmaj(d), fmaj(d), fmaj(GATE_ROWS), tmaj(GATE_ROWS),
                 tmaj(FOX_WIDTH), fmaj(FOX_HEADS * FOX_AUG))
    kern = functools.partial(_inproj_kernel, tiles_per_seq=nt, chunk=chunk, d_model=d)
    return pl.pallas_call(
        kern,
        grid=(tok // tm,),
        in_specs=[
            pl.BlockSpec((tm, d), lambda t: (t, 0)),
            pl.BlockSpec((1,) + mod3.shape[1:], lambda t: (t // nt, 0, 0)),
            _const_spec(wn.shape), _const_spec(bn.shape), _const_spec(wt.shape), _const_spec(bt.shape),
            _const_spec(conv_w.shape), _const_spec(conv_b.shape), _const_spec(w_mk.shape), _const_spec(w_mqt.shape),
        ],
        out_specs=out_specs,
        out_shape=out_shape,
        scratch_shapes=[pltpu.VMEM((SUBLANES, d), F32), pltpu.VMEM((FOX_HEADS, 128), F32)],
        compiler_params=pltpu.CompilerParams(dimension_semantics=("arbitrary",), vmem_limit_bytes=VMEM_LIMIT_BYTES),
        name="inproj",
    )(x2, mod3, wn, bn, wt, bt, conv_w, conv_b, w_mk, w_mqt)


def _fox_scores(qi, a, qt_ref, k_ref, kf_ref, qf_ref, blk):
    hd = FOX_HEAD_DIM
    half = blk // 2
    q0 = qi * blk
    main = q0 + half
    keys = q0 + blk
    qt2 = qt_ref[0, :, q0:keys]
    row = lax.broadcasted_iota(jnp.int32, qt2.shape, 0)
    tri_main = (lax.broadcasted_iota(jnp.int32, (half, blk), 0)
                <= lax.broadcasted_iota(jnp.int32, (half, blk), 1))
    kb = jnp.concatenate([k_ref[0, 0:keys, :], kf_ref[0, 0:keys, :]], axis=1)
    qm = jnp.where((row >= a * hd) & (row < (a + 1) * hd), qt2, jnp.zeros_like(qt2))
    qm = jnp.concatenate([qm, qf_ref[0, a * FOX_AUG:(a + 1) * FOX_AUG, q0:keys],
                          jnp.zeros((2 * hd - FOX_AUG, blk), BF16)], axis=0)
    s_main = jnp.dot(kb[:main], qm, preferred_element_type=F32)
    s_last = jnp.dot(kb[main:], qm[:, half:], preferred_element_type=F32)
    s_edge = jnp.where(tri_main, s_main[q0:], -jnp.inf)
    s_last = jnp.where(tri_main[:, :half], s_last, -jnp.inf)
    m = jnp.max(s_edge, axis=0, keepdims=True)
    if q0:
        m = jnp.maximum(m, jnp.max(s_main[:q0], axis=0, keepdims=True))
    m_hi = jnp.maximum(m[:, half:], jnp.max(s_last, axis=0, keepdims=True))
    m = jnp.concatenate([m[:, :half], m_hi], axis=1)
    return s_main[:q0] if q0 else None, s_edge, s_last, m, m_hi


def _fox_values(qi, a, scores, vt_ref, o_ref, blk):
    hd = FOX_HEAD_DIM
    half = blk // 2
    q0 = qi * blk
    main = q0 + half
    keys = q0 + blk
    s_top, s_edge, s_last, m, m_hi = scores
    ones = (lax.broadcasted_iota(jnp.int32, (VAUG_PAD, keys), 0) == 0).astype(BF16)
    vaug = jnp.concatenate([vt_ref[0, a * hd:(a + 1) * hd, 0:keys], ones], axis=0)
    acc = jnp.dot(vaug[:, q0:main], jnp.exp2(s_edge - m).astype(BF16), preferred_element_type=F32)
    if q0:
        acc = acc + jnp.dot(vaug[:, :q0], jnp.exp2(s_top - m).astype(BF16), preferred_element_type=F32)
    acc_hi = jnp.dot(vaug[:, main:], jnp.exp2(s_last - m_hi).astype(BF16), preferred_element_type=F32)
    acc = jnp.concatenate([acc[:, :half], acc[:, half:] + acc_hi], axis=1)
    o_ref[0, a * hd:(a + 1) * hd, q0:keys] = (acc[:hd] / acc[hd:hd + 1]).astype(BF16)


def _fox_kernel(qt_ref, k_ref, vt_ref, kf_ref, qf_ref, o_ref, *, blk):
    units = [(qi, a) for qi in reversed(range(k_ref.shape[1] // blk)) for a in range(2)]
    pending = []
    for qi, a in units:
        pending.append((qi, a, _fox_scores(qi, a, qt_ref, k_ref, kf_ref, qf_ref, blk)))
        if len(pending) > FOX_SCORE_LEAD:
            _fox_values(*pending.pop(0), vt_ref, o_ref, blk)
    for unit in pending:
        _fox_values(*unit, vt_ref, o_ref, blk)


def _fox_attention(fqt, fk3, fvt, kf3, qf, *, blk):
    bsz, width, seq = fqt.shape
    pairs = FOX_HEADS // 2
    pw = 2 * FOX_HEAD_DIM
    return pl.pallas_call(
        functools.partial(_fox_kernel, blk=blk),
        grid=(bsz, pairs),
        in_specs=[
            pl.BlockSpec((1, pw, seq), lambda b, p: (b, p, 0)),
            pl.BlockSpec((1, seq, pw), lambda b, p: (b, 0, p)),
            pl.BlockSpec((1, pw, seq), lambda b, p: (b, p, 0)),
            pl.BlockSpec((1, seq, pw), lambda b, p: (b, 0, p)),
            pl.BlockSpec((1, 2 * FOX_AUG, seq), lambda b, p: (b, p, 0)),
        ],
        out_specs=pl.BlockSpec((1, pw, seq), lambda b, p: (b, p, 0)),
        out_shape=jax.ShapeDtypeStruct((bsz, width, seq), BF16),
        compiler_params=pltpu.CompilerParams(dimension_semantics=("parallel", "parallel"),
                                             vmem_limit_bytes=VMEM_LIMIT_BYTES),
        name="fox_attention",
    )(fqt, fk3, fvt, kf3, qf)


def _mlstm_kernel(qt_ref, k_ref, vt_ref, ot_ref, rows_ref, cols_ref, g_ref, h_ref, state_ref, m_ref):
    c = pl.program_id(1)
    L = qt_ref.shape[2]
    qk = k_ref.shape[2] // MLSTM_HEADS
    dv = vt_ref.shape[1] // MLSTM_HEADS

    @pl.when(c == 0)
    def _():
        state_ref[...] = jnp.zeros_like(state_ref)
        m_ref[...] = jnp.zeros_like(m_ref)

    src = lax.broadcasted_iota(jnp.int32, (L, L), 0)
    dst = lax.broadcasted_iota(jnp.int32, (L, L), 1)
    causal = src <= dst
    ones_rows = (lax.broadcasted_iota(jnp.int32, (VAUG_PAD, L), 0) == 0).astype(BF16)

    for bi, hd in [(bi, hd) for bi in range(qt_ref.shape[0]) for hd in range(MLSTM_HEADS)]:
        qt = qt_ref[bi, hd * qk:(hd + 1) * qk, :]
        kk = k_ref[bi, :, hd * qk:(hd + 1) * qk]
        vaug = jnp.concatenate([vt_ref[bi, hd * dv:(hd + 1) * dv, :], ones_rows], axis=0)
        b = rows_ref[bi, FOX_HEADS + hd:FOX_HEADS + hd + 1, :]
        ig = rows_ref[bi, FOX_HEADS + MLSTM_HEADS + hd:FOX_HEADS + MLSTM_HEADS + hd + 1, :]
        ccol = cols_ref[bi, :, FOX_HEADS + hd:FOX_HEADS + hd + 1]
        m_prev = m_ref[bi, hd]
        state = state_ref[bi, hd]

        g = jnp.broadcast_to(b[:, L - 1:L], (1, L))
        a = g - b + ig
        dmat = jnp.where(causal, b + ccol, -jnp.inf)
        inter = b + m_prev
        m_t = jnp.maximum(inter, jnp.max(dmat, axis=0, keepdims=True))
        scores = jnp.dot(kk, qt, preferred_element_type=F32) * jnp.exp(dmat - m_t)
        w_inter = jnp.exp(inter - m_t)
        ht = (w_inter * jnp.dot(state.astype(BF16), qt, preferred_element_type=F32)
              + jnp.dot(vaug, scores.astype(BF16), preferred_element_type=F32))
        den = ht[dv:dv + 1, :]
        hh = ht[:dv, :] / jnp.maximum(jnp.abs(den), jnp.exp(-m_t))
        mu = jnp.mean(hh, axis=0, keepdims=True)
        dlt = hh - mu
        var = jnp.mean(dlt * dlt, axis=0, keepdims=True)
        hn = dlt * lax.rsqrt(var + LN_EPS) * g_ref[hd * dv:(hd + 1) * dv, :]
        gate = jax.nn.sigmoid(ot_ref[bi, hd * dv:(hd + 1) * dv, :].astype(F32))
        h_ref[bi, hd * dv:(hd + 1) * dv, :] = (hn * gate).astype(BF16)

        m_new = jnp.maximum(g + m_prev, jnp.max(a, axis=1, keepdims=True))
        decay = jnp.exp(g + m_prev - m_new)
        w = jnp.exp(a - m_new)
        upd = jnp.dot((vaug.astype(F32) * w).astype(BF16), kk, preferred_element_type=F32)
        state_ref[bi, hd] = decay[:, :qk] * state + upd
        m_ref[bi, hd] = m_new


def _mlstm(mqt, mk3, mvt, mot, rows, cols3, gcol, *, chunk, nb):
    bsz, qkw, seq = mqt.shape
    d = mvt.shape[1]
    dv = d // MLSTM_HEADS
    qk = qkw // MLSTM_HEADS
    fmaj = lambda r: pl.BlockSpec((nb, r, chunk), lambda b, c: (b, 0, c))
    return pl.pallas_call(
        _mlstm_kernel,
        grid=(bsz // nb, seq // chunk),
        in_specs=[
            fmaj(qkw),
            pl.BlockSpec((nb, chunk, qkw), lambda b, c: (b, c, 0)),
            fmaj(d), fmaj(d), fmaj(GATE_ROWS),
            pl.BlockSpec((nb, chunk, GATE_ROWS), lambda b, c: (b, c, 0)),
            _const_spec(gcol.shape),
        ],
        out_specs=fmaj(d),
        out_shape=jax.ShapeDtypeStruct((bsz, d, seq), BF16),
        scratch_shapes=[pltpu.VMEM((nb, MLSTM_HEADS, dv + VAUG_PAD, qk), F32),
                        pltpu.VMEM((nb, MLSTM_HEADS, 1, chunk), F32)],
        compiler_params=pltpu.CompilerParams(dimension_semantics=("parallel", "arbitrary"),
                                             vmem_limit_bytes=VMEM_LIMIT_BYTES),
        name="mlstm",
    )(mqt, mk3, mvt, mot, rows, cols3, gcol)


def _tail_kernel(x_ref, mod_ref, att_ref, hm_ref, ga_ref, gb_ref, wpa_ref, wpb_ref, wout_ref,
                 l1g_ref, l1b_ref, wfi_ref, wfd_ref, l2g_ref, l2b_ref, o_ref, *, d_ff, nsub):
    g1 = mod_ref[0, 2:3, :]
    sh2 = mod_ref[0, 3:4, :]
    sc2 = mod_ref[0, 4:5, :]
    g2 = mod_ref[0, 5:6, :]
    sub = x_ref.shape[0] // nsub
    sl = [slice(k * sub, (k + 1) * sub) for k in range(nsub)]
    at = [jnp.dot(wpa_ref[...], att_ref[0, :, s], preferred_element_type=F32) for s in sl]
    bt = [jnp.dot(wpb_ref[...], hm_ref[0, :, s], preferred_element_type=F32) for s in sl]
    yt = [(jax.nn.sigmoid(ga_ref[0, :, s].astype(F32)) * a
           + jax.nn.sigmoid(gb_ref[0, :, s].astype(F32)) * b).astype(BF16) for s, a, b in zip(sl, at, bt)]
    z = [lax.dot_general(y, wout_ref[...], (((0,), (0,)), ((), ())), preferred_element_type=F32) for y in yt]
    x1 = [_ln(ALPHA * x_ref[s, :] + g1 * zz) * l1g_ref[...] + l1b_ref[...] for s, zz in zip(sl, z)]
    h2 = [(_ln(v) * (1.0 + sc2) + sh2).astype(BF16) for v in x1]
    gu = [jnp.dot(v, wfi_ref[...], preferred_element_type=F32) for v in h2]
    act = [(_silu(v[:, :d_ff]) * v[:, d_ff:]).astype(BF16) for v in gu]
    ff = [jnp.dot(v, wfd_ref[...], preferred_element_type=F32) for v in act]
    for s, v, f in zip(sl, x1, ff):
        o_ref[s, :] = _ln(ALPHA * v + g2 * f) * l2g_ref[...] + l2b_ref[...]


def _tail(x2, mod3, att_t, hm_t, ga_t, gb_t, wpa_t, wpb_t, w_out, l1g, l1b, wfi, wfd, l2g, l2b, *, seq, tm):
    tok, d = x2.shape
    nt = seq // tm
    d_ff = wfd.shape[0]
    fmaj = lambda rows: pl.BlockSpec((1, rows, tm), lambda t: (t // nt, 0, t % nt))
    consts = (wpa_t, wpb_t, w_out, l1g, l1b, wfi, wfd, l2g, l2b)
    return pl.pallas_call(
        functools.partial(_tail_kernel, d_ff=d_ff, nsub=2),
        grid=(tok // tm,),
        in_specs=[
            pl.BlockSpec((tm, d), lambda t: (t, 0)),
            pl.BlockSpec((1,) + mod3.shape[1:], lambda t: (t // nt, 0, 0)),
            fmaj(att_t.shape[1]), fmaj(d), fmaj(d), fmaj(d),
        ] + [_const_spec(a.shape) for a in consts],
        out_specs=pl.BlockSpec((tm, d), lambda t: (t, 0)),
        out_shape=jax.ShapeDtypeStruct((tok, d), F32),
        compiler_params=pltpu.CompilerParams(dimension_semantics=("parallel",), vmem_limit_bytes=VMEM_LIMIT_BYTES),
        name="tail",
    )(x2, mod3, att_t, hm_t, ga_t, gb_t, *consts)


def _pick(n, pref):
    while n % pref:
        pref //= 2
    return pref


def kernel(x, c, w_ada, b_ada, w_in, b_in, conv_w, conv_b, w_mq, w_mk, mh_norm_g, w_pa, w_pb, w_out,
           ln1_g, ln1_b, w_ffn_in, w_ffn_down, ln2_g, ln2_b):
    bsz, seq, d = x.shape
    tok = bsz * seq
    chunk = _pick(seq, 256)
    fox_blk = _pick(seq, 512)
    tm_in = _pick(seq, 512)
    tm_tail = _pick(seq, 512)

    x2 = x.reshape(tok, d)
    for l in range(DEPTH):
        mod3 = _modulation(c, w_ada[l], b_ada[l]).reshape(bsz, 6, d)

        splits = (FOX_WIDTH, FOX_WIDTH, FOX_WIDTH, FOX_HEADS, d, d, MLSTM_HEADS, MLSTM_HEADS, d, d, d)
        offs = [0]
        for s in splits:
            offs.append(offs[-1] + s)
        col = lambda i: (w_in[l][:, offs[i]:offs[i + 1]], b_in[l][offs[i]:offs[i + 1]])
        (wfq, bfq), (wfk, bfk), (wfv, bfv), (wff, bff), (wmu, bmu), (wmv, bmv), (wmi, bmi), (wmf, bmf), \
            (wmo, bmo), (wga, bga), (wgb, bgb) = [col(i) for i in range(len(splits))]
        scale = FOX_HEAD_DIM ** -0.5 * LOG2E
        wn = jnp.concatenate([wfk, wmu], axis=1).astype(BF16)
        bn = jnp.concatenate([bfk, bmu])[None, :]
        wt = jnp.concatenate([wfq * scale, wfv, wmv, wmo, wga, wgb, wff, wmi, wmf], axis=1).T.astype(BF16)
        bt = jnp.concatenate([bfq * scale, bfv, bmv, bmo, bga, bgb, bff, bmi, bmf])[:, None]

        fk, mk, fqt, fvt, mqt, mvt, mot, gat, gbt, rows, cols, kf, qf = _inproj(
            x2, mod3, wn, bn, wt, bt, conv_w[l], conv_b[l][None, :], w_mk[l].astype(BF16),
            jnp.swapaxes(w_mq[l], 1, 2).astype(BF16), bsz=bsz, seq=seq, tm=tm_in, chunk=chunk)

        att_t = _fox_attention(fqt, fk.reshape(bsz, seq, FOX_WIDTH), fvt, kf.reshape(bsz, seq, FOX_WIDTH), qf,
                               blk=fox_blk)

        hm_t = _mlstm(mqt, mk.reshape(bsz, seq, -1), mvt, mot, rows, cols.reshape(bsz, seq, GATE_ROWS),
                      mh_norm_g[l][:, None], chunk=chunk, nb=2 if bsz % 2 == 0 else 1)

        x2 = _tail(x2, mod3, att_t, hm_t, gat, gbt, w_pa[l].T.astype(BF16), w_pb[l].T.astype(BF16),
                   w_out[l].astype(BF16), ln1_g[l][None, :], ln1_b[l][None, :], w_ffn_in[l].astype(BF16),
                   w_ffn_down[l].astype(BF16), ln2_g[l][None, :], ln2_b[l][None, :], seq=seq, tm=tm_tail)
    return x2.reshape(bsz, seq, d)
```

```python
import functools

import jax
import jax.numpy as jnp
from jax import lax
from jax.experimental import pallas as pl
from jax.experimental.pallas import tpu as pltpu

F32 = jnp.float32
BF16 = jnp.bfloat16

FOX_HEADS = 8
FOX_HEAD_DIM = 64
FOX_WIDTH = FOX_HEADS * FOX_HEAD_DIM
MLSTM_HEADS = 4
CONV_WIDTH = 4
LN_EPS = 1e-5
DEPTH = 1
ALPHA = (2.0 * DEPTH) ** 0.25
LOG2E = 1.4426950408889634

VMEM_LIMIT_BYTES = 56 * 1024 * 1024
SUBLANES = 8
GATE_ROWS = 16
VAUG_PAD = 16
FOX_AUG = 16
FOX_SCORE_LEAD = 3


def _const_spec(shape):
    nd = len(shape)
    return pl.BlockSpec(shape, lambda *_: (0,) * nd, pipeline_mode=pl.Buffered(1))


def _ln(v):
    mu = jnp.mean(v, axis=-1, keepdims=True)
    d = v - mu
    var = jnp.mean(d * d, axis=-1, keepdims=True)
    return d * lax.rsqrt(var + LN_EPS)


def _silu(v):
    return v * jax.nn.sigmoid(v)


def _mod_kernel(c_ref, w_ref, b_ref, o_ref):
    a = _silu(c_ref[...]).astype(BF16)
    o_ref[...] = jnp.dot(a, w_ref[...].astype(BF16), preferred_element_type=F32) + b_ref[...]


def _modulation(c, w_ada, b_ada):
    bsz, d = c.shape
    n = w_ada.shape[1]
    return pl.pallas_call(
        _mod_kernel,
        grid=(n // d,),
        in_specs=[
            pl.BlockSpec((bsz, d), lambda j: (0, 0)),
            pl.BlockSpec((d, d), lambda j: (0, j)),
            pl.BlockSpec((1, d), lambda j: (0, j)),
        ],
        out_specs=pl.BlockSpec((bsz, d), lambda j: (0, j)),
        out_shape=jax.ShapeDtypeStruct((bsz, n), F32),
        name="modulation",
    )(c, w_ada, b_ada.reshape(1, n))


def _prefix_scan_lanes(v, width, op, identity):
    pos = lax.broadcasted_iota(jnp.int32, v.shape, 1) & (width - 1)
    d = 1
    while d < width:
        v = op(v, jnp.where(pos >= d, pltpu.roll(v, d, axis=1), identity))
        d *= 2
    return v


def _adaln(x_ref, mod_ref):
    return (_ln(x_ref[...]) * (1.0 + mod_ref[0, 1:2, :]) + mod_ref[0, 0:1, :]).astype(BF16)


def _inproj_kernel(x0_ref, mod0_ref, xn_ref, modn_ref, wn_ref, bn_ref, wt_ref, bt_ref, cw_ref, cb_ref,
                   wmk_ref, wmqt_ref,
                   fk_ref, mk_ref, fqt_ref, fvt_ref, mqt_ref, mvt_ref, mot_ref, gat_ref, gbt_ref,
                   rows_ref, cols_ref, kf_ref, qf_ref, prev_ref, fcarry_ref, h_ref,
                   *, tiles_per_seq, chunk, d_model):
    tm = xn_ref.shape[0]
    step = pl.program_id(0)
    first = (step % tiles_per_seq) == 0
    slot = step % 2

    @pl.when(step == 0)
    def _():
        h_ref[0] = _adaln(x0_ref, mod0_ref)

    h = h_ref[slot]

    @pl.when(first)
    def _():
        prev_ref[...] = jnp.zeros_like(prev_ref)
        fcarry_ref[...] = jnp.zeros_like(fcarry_ref)

    yn = jnp.dot(h, wn_ref[...], preferred_element_type=F32) + bn_ref[...]
    fk_ref[...] = yn[:, :FOX_WIDTH].astype(BF16)
    mu = yn[:, FOX_WIDTH:]

    def seg(r0, n):
        y = lax.dot_general(wt_ref[r0:r0 + n, :], h, (((1,), (1,)), ((), ())), preferred_element_type=F32)
        return y + bt_ref[r0:r0 + n, :]

    g = seg(wt_ref.shape[0] - GATE_ROWS, GATE_ROWS)
    ls = jax.nn.log_sigmoid(g) * LOG2E
    fcum = _prefix_scan_lanes(ls[0:FOX_HEADS], tm, jnp.add, 0.0) + fcarry_ref[:, 0:1]
    fcarry_ref[...] = jnp.broadcast_to(fcum[:, tm - 1:tm], fcarry_ref.shape)
    bcum = _prefix_scan_lanes(ls[FOX_HEADS:], chunk, jnp.add, 0.0)[MLSTM_HEADS:]
    ig = g[FOX_HEADS:FOX_HEADS + MLSTM_HEADS] * LOG2E
    src = ig - bcum
    smax = _prefix_scan_lanes(jnp.concatenate([src, src], axis=0), chunk, jnp.maximum, -jnp.inf)[MLSTM_HEADS:]
    zeros4 = jnp.zeros_like(ig)
    rows_ref[0] = jnp.concatenate([smax, zeros4, bcum, ig], axis=0)
    cols_ref[...] = jnp.concatenate([zeros4, zeros4, src, zeros4], axis=0).T

    f_hi = fcum.astype(BF16).astype(F32)
    f_mid = (fcum - f_hi).astype(BF16).astype(F32)
    f_lo = (fcum - f_hi - f_mid).astype(BF16).astype(F32)
    j = lax.broadcasted_iota(jnp.int32, (FOX_AUG, tm), 0)
    for hd in range(FOX_HEADS):
        odd = hd % 2
        ones_at = (j >= 3 + 3 * odd) & (j < 6 + 3 * odd)
        blk16 = jnp.where(j == 0, f_hi[hd:hd + 1], jnp.where(j == 1, f_mid[hd:hd + 1], jnp.where(
            j == 2, f_lo[hd:hd + 1], jnp.where(ones_at, 1.0, 0.0))))
        qf_ref[0, hd * FOX_AUG:(hd + 1) * FOX_AUG, :] = blk16.astype(BF16)
    kf_ref[...] = jnp.zeros_like(kf_ref)
    for pr in range(FOX_HEADS // 2):
        ea, eb = 2 * pr, 2 * pr + 1
        blk16 = jnp.where(j < 3, 1.0, 0.0)
        for r, src in ((3, f_hi[ea:ea + 1]), (4, f_mid[ea:ea + 1]), (5, f_lo[ea:ea + 1]),
                       (6, f_hi[eb:eb + 1]), (7, f_mid[eb:eb + 1]), (8, f_lo[eb:eb + 1])):
            blk16 = jnp.where(j == r, -src, blk16)
        kf_ref[:, pr * 2 * FOX_HEAD_DIM:pr * 2 * FOX_HEAD_DIM + FOX_AUG] = blk16.T.astype(BF16)

    r = 0
    for ref, n in ((fqt_ref, FOX_WIDTH), (fvt_ref, FOX_WIDTH), (mvt_ref, d_model), (mot_ref, d_model),
                   (gat_ref, d_model), (gbt_ref, d_model)):
        ref[0] = seg(r, n).astype(BF16)
        r += n

    full = jnp.concatenate([prev_ref[...], mu], axis=0)
    u = cb_ref[...] + cw_ref[0:1, :] * full[SUBLANES - 3:SUBLANES - 3 + tm]
    for j in range(1, CONV_WIDTH):
        off = SUBLANES - (CONV_WIDTH - 1) + j
        u = u + cw_ref[j:j + 1, :] * full[off:off + tm]
    prev_ref[...] = mu[tm - SUBLANES:, :]
    u = _silu(u).astype(BF16)

    inner = d_model // MLSTM_HEADS
    qk = wmk_ref.shape[2]
    for hd in range(MLSTM_HEADS):
        uh = u[:, hd * inner:(hd + 1) * inner]
        mk_ref[:, hd * qk:(hd + 1) * qk] = jnp.dot(uh, wmk_ref[hd], preferred_element_type=F32).astype(BF16)
        qt = lax.dot_general(wmqt_ref[hd], uh, (((1,), (1,)), ((), ())), preferred_element_type=F32)
        mqt_ref[0, hd * qk:(hd + 1) * qk, :] = (qt * (qk ** -0.5)).astype(BF16)

    h_ref[1 - slot] = _adaln(xn_ref, modn_ref)


def _inproj(x2, mod3, wn, bn, wt, bt, conv_w, conv_b, w_mk, w_mqt, *, bsz, seq, tm, chunk):
    tok, d = x2.shape
    nt = seq // tm
    nsteps = tok // tm
    qkw = w_mk.shape[0] * w_mk.shape[2]
    tmaj = lambda width: pl.BlockSpec((tm, width), lambda t: (t, 0))
    fmaj = lambda rows: pl.BlockSpec((1, rows, tm), lambda t: (t // nt, 0, t % nt))
    out_shape = (
        jax.ShapeDtypeStruct((tok, FOX_WIDTH), BF16),
        jax.ShapeDtypeStruct((tok, qkw), BF16),
        jax.ShapeDtypeStruct((bsz, FOX_WIDTH, seq), BF16),
        jax.ShapeDtypeStruct((bsz, FOX_WIDTH, seq), BF16),
        jax.ShapeDtypeStruct((bsz, qkw, seq), BF16),
        jax.ShapeDtypeStruct((bsz, d, seq), BF16),
        jax.ShapeDtypeStruct((bsz, d, seq), BF16),
        jax.ShapeDtypeStruct((bsz, d, seq), BF16),
        jax.ShapeDtypeStruct((bsz, d, seq), BF16),
        jax.ShapeDtypeStruct((bsz, GATE_ROWS, seq), F32),
        jax.ShapeDtypeStruct((tok, GATE_ROWS), F32),
        jax.ShapeDtypeStruct((tok, FOX_WIDTH), BF16),
        jax.ShapeDtypeStruct((bsz, FOX_HEADS * FOX_AUG, seq), BF16),
    )
    out_specs = (tmaj(FOX_WIDTH), tmaj(qkw), fmaj(FOX_WIDTH), fmaj(FOX_WIDTH), fmaj(qkw),
                 fmaj(d), fmaj(d), fmaj(d), fmaj(d), fmaj(GATE_ROWS), tmaj(GATE_ROWS),
                 tmaj(FOX_WIDTH), fmaj(FOX_HEADS * FOX_AUG))
    kern = functools.partial(_inproj_kernel, tiles_per_seq=nt, chunk=chunk, d_model=d)
    return pl.pallas_call(
        kern,
        grid=(tok // tm,),
        in_specs=[
            pl.BlockSpec((tm, d), lambda t: (0, 0)),
            pl.BlockSpec((1,) + mod3.shape[1:], lambda t: (0, 0, 0)),
            pl.BlockSpec((tm, d), lambda t: (jnp.minimum(t + 1, nsteps - 1), 0)),
            pl.BlockSpec((1,) + mod3.shape[1:], lambda t: (jnp.minimum(t + 1, nsteps - 1) // nt, 0, 0)),
            _const_spec(wn.shape), _const_spec(bn.shape), _const_spec(wt.shape), _const_spec(bt.shape),
            _const_spec(conv_w.shape), _const_spec(conv_b.shape), _const_spec(w_mk.shape), _const_spec(w_mqt.shape),
        ],
        out_specs=out_specs,
        out_shape=out_shape,
        scratch_shapes=[pltpu.VMEM((SUBLANES, d), F32), pltpu.VMEM((FOX_HEADS, 128), F32),
                        pltpu.VMEM((2, tm, d), BF16)],
        compiler_params=pltpu.CompilerParams(dimension_semantics=("arbitrary",), vmem_limit_bytes=VMEM_LIMIT_BYTES),
        name="inproj",
    )(x2, mod3, x2, mod3, wn, bn, wt, bt, conv_w, conv_b, w_mk, w_mqt)


def _fox_scores(qi, a, qt_ref, k_ref, kf_ref, qf_ref, blk):
    hd = FOX_HEAD_DIM
    half = blk // 2
    q0 = qi * blk
    main = q0 + half
    keys = q0 + blk
    qt2 = qt_ref[0, :, q0:keys]
    row = lax.broadcasted_iota(jnp.int32, qt2.shape, 0)
    tri_main = (lax.broadcasted_iota(jnp.int32, (half, blk), 0)
                <= lax.broadcasted_iota(jnp.int32, (half, blk), 1))
    kb = jnp.concatenate([k_ref[0, 0:keys, :], kf_ref[0, 0:keys, :]], axis=1)
    qm = jnp.where((row >= a * hd) & (row < (a + 1) * hd), qt2, jnp.zeros_like(qt2))
    qm = jnp.concatenate([qm, qf_ref[0, a * FOX_AUG:(a + 1) * FOX_AUG, q0:keys],
                          jnp.zeros((2 * hd - FOX_AUG, blk), BF16)], axis=0)
    s_main = jnp.dot(kb[:main], qm, preferred_element_type=F32)
    s_last = jnp.dot(kb[main:], qm[:, half:], preferred_element_type=F32)
    s_edge = jnp.where(tri_main, s_main[q0:], -jnp.inf)
    s_last = jnp.where(tri_main[:, :half], s_last, -jnp.inf)
    m = jnp.max(s_edge, axis=0, keepdims=True)
    if q0:
        m = jnp.maximum(m, jnp.max(s_main[:q0], axis=0, keepdims=True))
    m_hi = jnp.maximum(m[:, half:], jnp.max(s_last, axis=0, keepdims=True))
    m = jnp.concatenate([m[:, :half], m_hi], axis=1)
    return s_main[:q0] if q0 else None, s_edge, s_last, m, m_hi


def _fox_values(qi, a, scores, vt_ref, o_ref, blk):
    hd = FOX_HEAD_DIM
    half = blk // 2
    q0 = qi * blk
    main = q0 + half
    keys = q0 + blk
    s_top, s_edge, s_last, m, m_hi = scores
    ones = (lax.broadcasted_iota(jnp.int32, (VAUG_PAD, keys), 0) == 0).astype(BF16)
    vaug = jnp.concatenate([vt_ref[0, a * hd:(a + 1) * hd, 0:keys], ones], axis=0)
    acc = jnp.dot(vaug[:, q0:main], jnp.exp2(s_edge - m).astype(BF16), preferred_element_type=F32)
    if q0:
        acc = acc + jnp.dot(vaug[:, :q0], jnp.exp2(s_top - m).astype(BF16), preferred_element_type=F32)
    acc_hi = jnp.dot(vaug[:, main:], jnp.exp2(s_last - m_hi).astype(BF16), preferred_element_type=F32)
    acc = jnp.concatenate([acc[:, :half], acc[:, half:] + acc_hi], axis=1)
    o_ref[0, a * hd:(a + 1) * hd, q0:keys] = (acc[:hd] / acc[hd:hd + 1]).astype(BF16)


def _fox_kernel(qt_ref, k_ref, vt_ref, kf_ref, qf_ref, o_ref, *, blk):
    units = [(qi, a) for qi in reversed(range(k_ref.shape[1] // blk)) for a in range(2)]
    pending = []
    for qi, a in units:
        pending.append((qi, a, _fox_scores(qi, a, qt_ref, k_ref, kf_ref, qf_ref, blk)))
        if len(pending) > FOX_SCORE_LEAD:
            _fox_values(*pending.pop(0), vt_ref, o_ref, blk)
    for unit in pending:
        _fox_values(*unit, vt_ref, o_ref, blk)


def _fox_attention(fqt, fk3, fvt, kf3, qf, *, blk):
    bsz, width, seq = fqt.shape
    pairs = FOX_HEADS // 2
    pw = 2 * FOX_HEAD_DIM
    return pl.pallas_call(
        functools.partial(_fox_kernel, blk=blk),
        grid=(bsz, pairs),
        in_specs=[
            pl.BlockSpec((1, pw, seq), lambda b, p: (b, p, 0)),
            pl.BlockSpec((1, seq, pw), lambda b, p: (b, 0, p)),
            pl.BlockSpec((1, pw, seq), lambda b, p: (b, p, 0)),
            pl.BlockSpec((1, seq, pw), lambda b, p: (b, 0, p)),
            pl.BlockSpec((1, 2 * FOX_AUG, seq), lambda b, p: (b, p, 0)),
        ],
        out_specs=pl.BlockSpec((1, pw, seq), lambda b, p: (b, p, 0)),
        out_shape=jax.ShapeDtypeStruct((bsz, width, seq), BF16),
        compiler_params=pltpu.CompilerParams(dimension_semantics=("parallel", "parallel"),
                                             vmem_limit_bytes=VMEM_LIMIT_BYTES),
        name="fox_attention",
    )(fqt, fk3, fvt, kf3, qf)


def _mlstm_kernel(qt_ref, k_ref, vt_ref, ot_ref, rows_ref, cols_ref, g_ref, h_ref, state_ref, m_ref):
    c = pl.program_id(1)
    L = qt_ref.shape[2]
    qk = k_ref.shape[2] // MLSTM_HEADS
    dv = vt_ref.shape[1] // MLSTM_HEADS

    @pl.when(c == 0)
    def _():
        state_ref[...] = jnp.zeros_like(state_ref)
        m_ref[...] = jnp.zeros_like(m_ref)

    src = lax.broadcasted_iota(jnp.int32, (L, L), 0)
    dst = lax.broadcasted_iota(jnp.int32, (L, L), 1)
    causal = src <= dst
    ones_rows = (lax.broadcasted_iota(jnp.int32, (VAUG_PAD, L), 0) == 0).astype(BF16)

    for bi, hd in [(bi, hd) for bi in range(qt_ref.shape[0]) for hd in range(MLSTM_HEADS)]:
        qt = qt_ref[bi, hd * qk:(hd + 1) * qk, :]
        kk = k_ref[bi, :, hd * qk:(hd + 1) * qk]
        vaug = jnp.concatenate([vt_ref[bi, hd * dv:(hd + 1) * dv, :], ones_rows], axis=0)
        smax = rows_ref[bi, hd:hd + 1, :]
        b = rows_ref[bi, FOX_HEADS + hd:FOX_HEADS + hd + 1, :]
        scol = cols_ref[bi, :, FOX_HEADS + hd:FOX_HEADS + hd + 1]
        m_prev = m_ref[bi, hd]
        state = state_ref[bi, hd]

        g = jnp.broadcast_to(b[:, L - 1:L], (1, L))
        inter = b + m_prev
        m_t = jnp.maximum(inter, b + smax)
        pmat = jnp.exp2(jnp.where(causal, (b - m_t) + scol, -jnp.inf))
        scores = jnp.dot(kk, qt, preferred_element_type=F32) * pmat
        w_inter = jnp.exp2(inter - m_t)
        ht = (w_inter * jnp.dot(state.astype(BF16), qt, preferred_element_type=F32)
              + jnp.dot(vaug, scores.astype(BF16), preferred_element_type=F32))
        den = ht[dv:dv + 1, :]
        hh = ht[:dv, :] * (1.0 / jnp.maximum(jnp.abs(den), jnp.exp2(-m_t)))
        mu = jnp.mean(hh, axis=0, keepdims=True)
        dlt = hh - mu
        var = jnp.mean(dlt * dlt, axis=0, keepdims=True)
        hn = dlt * lax.rsqrt(var + LN_EPS) * g_ref[hd * dv:(hd + 1) * dv, :]
        th = jnp.tanh(ot_ref[bi, hd * dv:(hd + 1) * dv, :].astype(F32))
        h_ref[bi, hd * dv:(hd + 1) * dv, :] = (hn + hn * th).astype(BF16)

        m_new = jnp.maximum(g + m_prev, g + jnp.broadcast_to(smax[:, L - 1:L], (1, L)))
        decay = jnp.exp2(g + m_prev - m_new)
        wcol = jnp.exp2((g - m_new)[:, :1] + scol)
        upd = jnp.dot(vaug, (kk.astype(F32) * wcol).astype(BF16), preferred_element_type=F32)
        state_ref[bi, hd] = decay[:, :qk] * state + upd
        m_ref[bi, hd] = m_new


def _mlstm(mqt, mk3, mvt, mot, rows, cols3, gcol, *, chunk, nb):
    bsz, qkw, seq = mqt.shape
    d = mvt.shape[1]
    dv = d // MLSTM_HEADS
    qk = qkw // MLSTM_HEADS
    fmaj = lambda r: pl.BlockSpec((nb, r, chunk), lambda b, c: (b, 0, c))
    return pl.pallas_call(
        _mlstm_kernel,
        grid=(bsz // nb, seq // chunk),
        in_specs=[
            fmaj(qkw),
            pl.BlockSpec((nb, chunk, qkw), lambda b, c: (b, c, 0)),
            fmaj(d), fmaj(d), fmaj(GATE_ROWS),
            pl.BlockSpec((nb, chunk, GATE_ROWS), lambda b, c: (b, c, 0)),
            _const_spec(gcol.shape),
        ],
        out_specs=fmaj(d),
        out_shape=jax.ShapeDtypeStruct((bsz, d, seq), BF16),
        scratch_shapes=[pltpu.VMEM((nb, MLSTM_HEADS, dv + VAUG_PAD, qk), F32),
                        pltpu.VMEM((nb, MLSTM_HEADS, 1, chunk), F32)],
        compiler_params=pltpu.CompilerParams(dimension_semantics=("parallel", "arbitrary"),
                                             vmem_limit_bytes=VMEM_LIMIT_BYTES),
        name="mlstm",
    )(mqt, mk3, mvt, mot, rows, cols3, gcol)


def _tail_kernel(x_ref, mod_ref, att_ref, hm_ref, ga_ref, gb_ref, wpa_ref, wpb_ref, wout_ref,
                 l1g_ref, l1b_ref, wfi_ref, wfd_ref, l2g_ref, l2b_ref, o_ref, *, d_ff, nsub):
    g1 = mod_ref[0, 2:3, :]
    sh2 = mod_ref[0, 3:4, :]
    sc2 = mod_ref[0, 4:5, :]
    g2 = mod_ref[0, 5:6, :]
    sub = x_ref.shape[0] // nsub
    sl = [slice(k * sub, (k + 1) * sub) for k in range(nsub)]
    at = [jnp.dot(wpa_ref[...], att_ref[0, :, s], preferred_element_type=F32) for s in sl]
    bt = [jnp.dot(wpb_ref[...], hm_ref[0, :, s], preferred_element_type=F32) for s in sl]
    yt = [(jax.nn.sigmoid(ga_ref[0, :, s].astype(F32)) * a
           + jax.nn.sigmoid(gb_ref[0, :, s].astype(F32)) * b).astype(BF16) for s, a, b in zip(sl, at, bt)]
    z = [lax.dot_general(y, wout_ref[...], (((0,), (0,)), ((), ())), preferred_element_type=F32) for y in yt]
    x1 = [_ln(ALPHA * x_ref[s, :] + g1 * zz) * l1g_ref[...] + l1b_ref[...] for s, zz in zip(sl, z)]
    h2 = [(_ln(v) * (1.0 + sc2) + sh2).astype(BF16) for v in x1]
    gu = [jnp.dot(v, wfi_ref[...], preferred_element_type=F32) for v in h2]
    act = [(_silu(v[:, :d_ff]) * v[:, d_ff:]).astype(BF16) for v in gu]
    ff = [jnp.dot(v, wfd_ref[...], preferred_element_type=F32) for v in act]
    for s, v, f in zip(sl, x1, ff):
        o_ref[s, :] = _ln(ALPHA * v + g2 * f) * l2g_ref[...] + l2b_ref[...]


def _tail(x2, mod3, att_t, hm_t, ga_t, gb_t, wpa_t, wpb_t, w_out, l1g, l1b, wfi, wfd, l2g, l2b, *, seq, tm):
    tok, d = x2.shape
    nt = seq // tm
    d_ff = wfd.shape[0]
    fmaj = lambda rows: pl.BlockSpec((1, rows, tm), lambda t: (t // nt, 0, t % nt))
    consts = (wpa_t, wpb_t, w_out, l1g, l1b, wfi, wfd, l2g, l2b)
    return pl.pallas_call(
        functools.partial(_tail_kernel, d_ff=d_ff, nsub=2),
        grid=(tok // tm,),
        in_specs=[
            pl.BlockSpec((tm, d), lambda t: (t, 0)),
            pl.BlockSpec((1,) + mod3.shape[1:], lambda t: (t // nt, 0, 0)),
            fmaj(att_t.shape[1]), fmaj(d), fmaj(d), fmaj(d),
        ] + [_const_spec(a.shape) for a in consts],
        out_specs=pl.BlockSpec((tm, d), lambda t: (t, 0)),
        out_shape=jax.ShapeDtypeStruct((tok, d), F32),
        compiler_params=pltpu.CompilerParams(dimension_semantics=("parallel",), vmem_limit_bytes=VMEM_LIMIT_BYTES),
        name="tail",
    )(x2, mod3, att_t, hm_t, ga_t, gb_t, *consts)


def _pick(n, pref):
    while n % pref:
        pref //= 2
    return pref


def kernel(x, c, w_ada, b_ada, w_in, b_in, conv_w, conv_b, w_mq, w_mk, mh_norm_g, w_pa, w_pb, w_out,
           ln1_g, ln1_b, w_ffn_in, w_ffn_down, ln2_g, ln2_b):
    bsz, seq, d = x.shape
    tok = bsz * seq
    chunk = _pick(seq, 256)
    fox_blk = _pick(seq, 512)
    tm_in = _pick(seq, 512)
    tm_tail = _pick(seq, 512)

    x2 = x.reshape(tok, d)
    for l in range(DEPTH):
        mod3 = _modulation(c, w_ada[l], b_ada[l]).reshape(bsz, 6, d)

        splits = (FOX_WIDTH, FOX_WIDTH, FOX_WIDTH, FOX_HEADS, d, d, MLSTM_HEADS, MLSTM_HEADS, d, d, d)
        offs = [0]
        for s in splits:
            offs.append(offs[-1] + s)
        col = lambda i: (w_in[l][:, offs[i]:offs[i + 1]], b_in[l][offs[i]:offs[i + 1]])
        (wfq, bfq), (wfk, bfk), (wfv, bfv), (wff, bff), (wmu, bmu), (wmv, bmv), (wmi, bmi), (wmf, bmf), \
            (wmo, bmo), (wga, bga), (wgb, bgb) = [col(i) for i in range(len(splits))]
        scale = FOX_HEAD_DIM ** -0.5 * LOG2E
        wn = jnp.concatenate([wfk, wmu], axis=1).astype(BF16)
        bn = jnp.concatenate([bfk, bmu])[None, :]
        wt = jnp.concatenate([wfq * scale, wfv, wmv, wmo * 0.5, wga, wgb, wff, wmi, wmf], axis=1).T.astype(BF16)
        bt = jnp.concatenate([bfq * scale, bfv, bmv, bmo * 0.5, bga, bgb, bff, bmi, bmf])[:, None]

        fk, mk, fqt, fvt, mqt, mvt, mot, gat, gbt, rows, cols, kf, qf = _inproj(
            x2, mod3, wn, bn, wt, bt, conv_w[l], conv_b[l][None, :], w_mk[l].astype(BF16),
            jnp.swapaxes(w_mq[l], 1, 2).astype(BF16), bsz=bsz, seq=seq, tm=tm_in, chunk=chunk)

        att_t = _fox_attention(fqt, fk.reshape(bsz, seq, FOX_WIDTH), fvt, kf.reshape(bsz, seq, FOX_WIDTH), qf,
                               blk=fox_blk)

        hm_t = _mlstm(mqt, mk.reshape(bsz, seq, -1), mvt, mot, rows, cols.reshape(bsz, seq, GATE_ROWS),
                      jnp.broadcast_to(0.5 * mh_norm_g[l][:, None], (d, chunk)), chunk=chunk, nb=_pick(bsz, 4))

        x2 = _tail(x2, mod3, att_t, hm_t, gat, gbt, w_pa[l].T.astype(BF16), w_pb[l].T.astype(BF16),
                   w_out[l].astype(BF16), ln1_g[l][None, :], ln1_b[l][None, :], w_ffn_in[l].astype(BF16),
                   w_ffn_down[l].astype(BF16), ln2_g[l][None, :], ln2_b[l][None, :], seq=seq, tm=tm_tail)
    return x2.reshape(bsz, seq, d)
```

```python
import functools

import jax
import jax.numpy as jnp
from jax import lax
from jax.experimental import pallas as pl
from jax.experimental.pallas import tpu as pltpu

F32 = jnp.float32
BF16 = jnp.bfloat16

FOX_HEADS = 8
FOX_HEAD_DIM = 64
FOX_WIDTH = FOX_HEADS * FOX_HEAD_DIM
MLSTM_HEADS = 4
CONV_WIDTH = 4
LN_EPS = 1e-5
DEPTH = 1
ALPHA = (2.0 * DEPTH) ** 0.25
LOG2E = 1.4426950408889634

VMEM_LIMIT_BYTES = 56 * 1024 * 1024
SUBLANES = 8
BF16_ROWS = 16
GATE_ROWS = 16
VAUG_PAD = 16
FOX_AUG = 16
FOX_SCORE_LEAD = 3

def _const_spec(shape):
    nd = len(shape)
    return pl.BlockSpec(shape, lambda *_: (0,) * nd, pipeline_mode=pl.Buffered(1))


def _ln(v):
    mu = jnp.mean(v, axis=-1, keepdims=True)
    d = v - mu
    var = jnp.mean(d * d, axis=-1, keepdims=True)
    return d * lax.rsqrt(var + LN_EPS)


def _silu(v):
    return v * jax.nn.sigmoid(v)


def _mod_kernel(c_ref, w_ref, b_ref, o_ref):
    a = _silu(c_ref[...]).astype(BF16)
    o_ref[...] = jnp.dot(a, w_ref[...].astype(BF16), preferred_element_type=F32) + b_ref[...]


def _modulation(c, w_ada, b_ada):
    bsz, d = c.shape
    n = w_ada.shape[1]
    return pl.pallas_call(
        _mod_kernel,
        grid=(n // d,),
        in_specs=[
            pl.BlockSpec((bsz, d), lambda j: (0, 0)),
            pl.BlockSpec((d, d), lambda j: (0, j)),
            pl.BlockSpec((1, d), lambda j: (0, j)),
        ],
        out_specs=pl.BlockSpec((bsz, d), lambda j: (0, j)),
        out_shape=jax.ShapeDtypeStruct((bsz, n), F32),
        name="modulation",
    )(c, w_ada, b_ada.reshape(1, n))


def _prefix_sum_lanes(v, width):
    pos = lax.broadcasted_iota(jnp.int32, v.shape, 1) & (width - 1)
    d = 1
    while d < width:
        v = v + jnp.where(pos >= d, pltpu.roll(v, d, axis=1), 0.0)
        d *= 2
    return v


INPROJ_INPUTS = 10
INPROJ_OUTPUTS = 13


def _inproj_kernel(*refs, tiles_per_seq, chunk, d_model):
    (x_ref, mod_ref, wn_ref, bn_ref, wt_ref, bt_ref, cw_ref, cb_ref, wmk_ref, wmqt_ref) = refs[:INPROJ_INPUTS]
    n_side = (len(refs) - INPROJ_INPUTS - INPROJ_OUTPUTS - 2) // 2
    side_in = refs[INPROJ_INPUTS:INPROJ_INPUTS + n_side]
    outs = refs[INPROJ_INPUTS + n_side:]
    (fk_ref, mk_ref, fqt_ref, fvt_ref, mqt_ref, mvt_ref, mot_ref, gat_ref, gbt_ref,
     rows_ref, cols_ref, kf_ref, qf_ref) = outs[:INPROJ_OUTPUTS]
    side_out = outs[INPROJ_OUTPUTS:INPROJ_OUTPUTS + n_side]
    prev_ref, fcarry_ref = outs[INPROJ_OUTPUTS + n_side:]

    for src, dst in zip(side_in, side_out):
        dst[...] = src[...].astype(BF16)

    tm = x_ref.shape[0]
    first = (pl.program_id(0) % tiles_per_seq) == 0
    sh1 = mod_ref[0, 0:1, :]
    sc1 = mod_ref[0, 1:2, :]
    h = (_ln(x_ref[...]) * (1.0 + sc1) + sh1).astype(BF16)

    @pl.when(first)
    def _():
        prev_ref[...] = jnp.zeros_like(prev_ref)
        fcarry_ref[...] = jnp.zeros_like(fcarry_ref)

    def seg(r0, n):
        y = lax.dot_general(wt_ref[r0:r0 + n, :], h, (((1,), (1,)), ((), ())), preferred_element_type=F32)
        return y + bt_ref[r0:r0 + n, :]

    yn = jnp.dot(h, wn_ref[...], preferred_element_type=F32) + bn_ref[...]
    fk_ref[...] = yn[:, :FOX_WIDTH].astype(BF16)
    mu = yn[:, FOX_WIDTH:]

    g = seg(wt_ref.shape[0] - GATE_ROWS, GATE_ROWS)
    ls = jax.nn.log_sigmoid(g)
    fcum = _prefix_sum_lanes(ls[0:FOX_HEADS] * LOG2E, tm) + fcarry_ref[:, 0:1]
    fcarry_ref[...] = jnp.broadcast_to(fcum[:, tm - 1:tm], fcarry_ref.shape)
    bcum = _prefix_sum_lanes(ls[FOX_HEADS:], chunk)[MLSTM_HEADS:]
    ig = g[FOX_HEADS:FOX_HEADS + MLSTM_HEADS]
    rows_ref[0] = jnp.concatenate([fcum, bcum, ig], axis=0)
    colsT = jnp.concatenate([fcum, ig - bcum, jnp.zeros_like(ig)], axis=0)
    cols_ref[...] = colsT.T

    f_hi = fcum.astype(BF16).astype(F32)
    f_mid = (fcum - f_hi).astype(BF16).astype(F32)
    f_lo = (fcum - f_hi - f_mid).astype(BF16).astype(F32)
    j = lax.broadcasted_iota(jnp.int32, (FOX_AUG, tm), 0)
    for hd in range(FOX_HEADS):
        odd = hd % 2
        ones_at = (j >= 3 + 3 * odd) & (j < 6 + 3 * odd)
        blk16 = jnp.where(j == 0, f_hi[hd:hd + 1], jnp.where(j == 1, f_mid[hd:hd + 1], jnp.where(
            j == 2, f_lo[hd:hd + 1], jnp.where(ones_at, 1.0, 0.0))))
        qf_ref[0, hd * FOX_AUG:(hd + 1) * FOX_AUG, :] = blk16.astype(BF16)
    kf_ref[...] = jnp.zeros_like(kf_ref)
    for pr in range(FOX_HEADS // 2):
        ea, eb = 2 * pr, 2 * pr + 1
        blk16 = jnp.where(j < 3, 1.0, 0.0)
        for entry, term in ((3, f_hi[ea:ea + 1]), (4, f_mid[ea:ea + 1]), (5, f_lo[ea:ea + 1]),
                            (6, f_hi[eb:eb + 1]), (7, f_mid[eb:eb + 1]), (8, f_lo[eb:eb + 1])):
            blk16 = jnp.where(j == entry, -term, blk16)
        kf_ref[:, pr * 2 * FOX_HEAD_DIM:pr * 2 * FOX_HEAD_DIM + FOX_AUG] = blk16.T.astype(BF16)

    r = 0
    for ref, n in ((fqt_ref, FOX_WIDTH), (fvt_ref, FOX_WIDTH), (mvt_ref, d_model), (mot_ref, d_model),
                   (gat_ref, d_model), (gbt_ref, d_model)):
        ref[0] = seg(r, n).astype(BF16)
        r += n

    u = cb_ref[...] + cw_ref[CONV_WIDTH - 1:CONV_WIDTH, :] * mu
    for k in range(1, CONV_WIDTH):
        u = u + cw_ref[CONV_WIDTH - 1 - k:CONV_WIDTH - k, :] * pltpu.roll(mu, k, axis=0)
    head = jnp.concatenate([prev_ref[...], mu[:SUBLANES]], axis=0)
    u_head = cb_ref[...] + cw_ref[0:1, :] * head[SUBLANES - 3:2 * SUBLANES - 3]
    for j in range(1, CONV_WIDTH):
        off = SUBLANES - (CONV_WIDTH - 1) + j
        u_head = u_head + cw_ref[j:j + 1, :] * head[off:off + SUBLANES]
    u = jnp.concatenate([u_head, u[SUBLANES:]], axis=0)
    prev_ref[...] = mu[tm - SUBLANES:, :]
    u = _silu(u).astype(BF16)

    inner = d_model // MLSTM_HEADS
    qk = wmk_ref.shape[2]
    for hd in range(MLSTM_HEADS):
        uh = u[:, hd * inner:(hd + 1) * inner]
        mk_ref[:, hd * qk:(hd + 1) * qk] = jnp.dot(uh, wmk_ref[hd], preferred_element_type=F32).astype(BF16)
        qt = lax.dot_general(wmqt_ref[hd], uh, (((1,), (1,)), ((), ())), preferred_element_type=F32)
        mqt_ref[0, hd * qk:(hd + 1) * qk, :] = (qt * (qk ** -0.5)).astype(BF16)


def _inproj(x2, mod3, wn, bn, wt, bt, conv_w, conv_b, w_mk, w_mqt, side, *, bsz, seq, tm, chunk):
    tok, d = x2.shape
    nt = seq // tm
    nsteps = tok // tm
    qkw = w_mk.shape[0] * w_mk.shape[2]
    tmaj = lambda width: pl.BlockSpec((tm, width), lambda t: (t, 0))
    fmaj = lambda rows: pl.BlockSpec((1, rows, tm), lambda t: (t // nt, 0, t % nt))
    slab = lambda a: pl.BlockSpec((a.shape[0] // nsteps, a.shape[1]), lambda t: (t, 0))
    out_shape = (
        jax.ShapeDtypeStruct((tok, FOX_WIDTH), BF16),
        jax.ShapeDtypeStruct((tok, qkw), BF16),
        jax.ShapeDtypeStruct((bsz, FOX_WIDTH, seq), BF16),
        jax.ShapeDtypeStruct((bsz, FOX_WIDTH, seq), BF16),
        jax.ShapeDtypeStruct((bsz, qkw, seq), BF16),
        jax.ShapeDtypeStruct((bsz, d, seq), BF16),
        jax.ShapeDtypeStruct((bsz, d, seq), BF16),
        jax.ShapeDtypeStruct((bsz, d, seq), BF16),
        jax.ShapeDtypeStruct((bsz, d, seq), BF16),
        jax.ShapeDtypeStruct((bsz, GATE_ROWS, seq), F32),
        jax.ShapeDtypeStruct((tok, GATE_ROWS), F32),
        jax.ShapeDtypeStruct((tok, FOX_WIDTH), BF16),
        jax.ShapeDtypeStruct((bsz, FOX_HEADS * FOX_AUG, seq), BF16),
    ) + tuple(jax.ShapeDtypeStruct(a.shape, BF16) for a in side)
    out_specs = (tmaj(FOX_WIDTH), tmaj(qkw), fmaj(FOX_WIDTH), fmaj(FOX_WIDTH), fmaj(qkw),
                 fmaj(d), fmaj(d), fmaj(d), fmaj(d), fmaj(GATE_ROWS), tmaj(GATE_ROWS),
                 tmaj(FOX_WIDTH), fmaj(FOX_HEADS * FOX_AUG)) + tuple(slab(a) for a in side)
    kern = functools.partial(_inproj_kernel, tiles_per_seq=nt, chunk=chunk, d_model=d)
    return pl.pallas_call(
        kern,
        grid=(tok // tm,),
        in_specs=[
            pl.BlockSpec((tm, d), lambda t: (t, 0)),
            pl.BlockSpec((1,) + mod3.shape[1:], lambda t: (t // nt, 0, 0)),
            _const_spec(wn.shape), _const_spec(bn.shape), _const_spec(wt.shape), _const_spec(bt.shape),
            _const_spec(conv_w.shape), _const_spec(conv_b.shape), _const_spec(w_mk.shape), _const_spec(w_mqt.shape),
        ] + [slab(a) for a in side],
        out_specs=out_specs,
        out_shape=out_shape,
        scratch_shapes=[pltpu.VMEM((SUBLANES, d), F32), pltpu.VMEM((FOX_HEADS, 128), F32)],
        compiler_params=pltpu.CompilerParams(dimension_semantics=("arbitrary",), vmem_limit_bytes=VMEM_LIMIT_BYTES),
        name="inproj",
    )(x2, mod3, wn, bn, wt, bt, conv_w, conv_b, w_mk, w_mqt, *side)


def _fox_scores(qi, a, qt_ref, k_ref, kf_ref, qf_ref, blk):
    hd = FOX_HEAD_DIM
    half = blk // 2
    q0 = qi * blk
    main = q0 + half
    keys = q0 + blk
    qt2 = qt_ref[0, :, q0:keys]
    row = lax.broadcasted_iota(jnp.int32, qt2.shape, 0)
    tri_main = (lax.broadcasted_iota(jnp.int32, (half, blk), 0)
                <= lax.broadcasted_iota(jnp.int32, (half, blk), 1))
    kb = jnp.concatenate([k_ref[0, 0:keys, :], kf_ref[0, 0:keys, :]], axis=1)
    qm = jnp.where((row >= a * hd) & (row < (a + 1) * hd), qt2, jnp.zeros_like(qt2))
    qm = jnp.concatenate([qm, qf_ref[0, a * FOX_AUG:(a + 1) * FOX_AUG, q0:keys],
                          jnp.zeros((2 * hd - FOX_AUG, blk), BF16)], axis=0)
    s_main = jnp.dot(kb[:main], qm, preferred_element_type=F32)
    s_last = jnp.dot(kb[main:], qm[:, half:], preferred_element_type=F32)
    s_edge = jnp.where(tri_main, s_main[q0:], -jnp.inf)
    s_last = jnp.where(tri_main[:, :half], s_last, -jnp.inf)
    m = jnp.max(s_edge, axis=0, keepdims=True)
    if q0:
        m = jnp.maximum(m, jnp.max(s_main[:q0], axis=0, keepdims=True))
    m_hi = jnp.maximum(m[:, half:], jnp.max(s_last, axis=0, keepdims=True))
    m = jnp.concatenate([m[:, :half], m_hi], axis=1)
    return s_main[:q0] if q0 else None, s_edge, s_last, m, m_hi


def _fox_values(qi, a, scores, vt_ref, o_ref, blk):
    hd = FOX_HEAD_DIM
    half = blk // 2
    q0 = qi * blk
    main = q0 + half
    keys = q0 + blk
    s_top, s_edge, s_last, m, m_hi = scores
    ones = (lax.broadcasted_iota(jnp.int32, (VAUG_PAD, keys), 0) == 0).astype(BF16)
    vaug = jnp.concatenate([vt_ref[0, a * hd:(a + 1) * hd, 0:keys], ones], axis=0)
    acc = jnp.dot(vaug[:, q0:main], jnp.exp2(s_edge - m).astype(BF16), preferred_element_type=F32)
    if q0:
        acc = acc + jnp.dot(vaug[:, :q0], jnp.exp2(s_top - m).astype(BF16), preferred_element_type=F32)
    acc_hi = jnp.dot(vaug[:, main:], jnp.exp2(s_last - m_hi).astype(BF16), preferred_element_type=F32)
    acc = jnp.concatenate([acc[:, :half], acc[:, half:] + acc_hi], axis=1)
    o_ref[0, a * hd:(a + 1) * hd, q0:keys] = (acc[:hd] / acc[hd:hd + 1]).astype(BF16)


def _fox_kernel(qt_ref, k_ref, vt_ref, kf_ref, qf_ref, o_ref, *, blk):
    units = [(qi, a) for qi in reversed(range(k_ref.shape[1] // blk)) for a in range(2)]
    pending = []
    for qi, a in units:
        pending.append((qi, a, _fox_scores(qi, a, qt_ref, k_ref, kf_ref, qf_ref, blk)))
        if len(pending) > FOX_SCORE_LEAD:
            _fox_values(*pending.pop(0), vt_ref, o_ref, blk)
    for unit in pending:
        _fox_values(*unit, vt_ref, o_ref, blk)


def _fox_attention(fqt, fk3, fvt, kf3, qf, *, blk):
    bsz, width, seq = fqt.shape
    pairs = FOX_HEADS // 2
    pw = 2 * FOX_HEAD_DIM
    return pl.pallas_call(
        functools.partial(_fox_kernel, blk=blk),
        grid=(bsz, pairs),
        in_specs=[
            pl.BlockSpec((1, pw, seq), lambda b, p: (b, p, 0)),
            pl.BlockSpec((1, seq, pw), lambda b, p: (b, 0, p)),
            pl.BlockSpec((1, pw, seq), lambda b, p: (b, p, 0)),
            pl.BlockSpec((1, seq, pw), lambda b, p: (b, 0, p)),
            pl.BlockSpec((1, 2 * FOX_AUG, seq), lambda b, p: (b, p, 0)),
        ],
        out_specs=pl.BlockSpec((1, pw, seq), lambda b, p: (b, p, 0)),
        out_shape=jax.ShapeDtypeStruct((bsz, width, seq), BF16),
        compiler_params=pltpu.CompilerParams(dimension_semantics=("parallel", "parallel"),
                                             vmem_limit_bytes=VMEM_LIMIT_BYTES),
        name="fox_attention",
    )(fqt, fk3, fvt, kf3, qf)


def _mlstm_kernel(qt_ref, k_ref, vt_ref, ot_ref, rows_ref, cols_ref, g_ref, h_ref, state_ref, m_ref):
    c = pl.program_id(1)
    L = qt_ref.shape[2]
    qk = k_ref.shape[2] // MLSTM_HEADS
    dv = vt_ref.shape[1] // MLSTM_HEADS

    @pl.when(c == 0)
    def _():
        state_ref[...] = jnp.zeros_like(state_ref)
        m_ref[...] = jnp.zeros_like(m_ref)

    src = lax.broadcasted_iota(jnp.int32, (L, L), 0)
    dst = lax.broadcasted_iota(jnp.int32, (L, L), 1)
    causal = src <= dst
    ones_rows = (lax.broadcasted_iota(jnp.int32, (VAUG_PAD, L), 0) == 0).astype(BF16)

    for bi, hd in [(bi, hd) for bi in range(qt_ref.shape[0]) for hd in range(MLSTM_HEADS)]:
        qt = qt_ref[bi, hd * qk:(hd + 1) * qk, :]
        kk = k_ref[bi, :, hd * qk:(hd + 1) * qk]
        vaug = jnp.concatenate([vt_ref[bi, hd * dv:(hd + 1) * dv, :], ones_rows], axis=0)
        b = rows_ref[bi, FOX_HEADS + hd:FOX_HEADS + hd + 1, :]
        ig = rows_ref[bi, FOX_HEADS + MLSTM_HEADS + hd:FOX_HEADS + MLSTM_HEADS + hd + 1, :]
        ccol = cols_ref[bi, :, FOX_HEADS + hd:FOX_HEADS + hd + 1]
        m_prev = m_ref[bi, hd]
        state = state_ref[bi, hd]

        g = jnp.broadcast_to(b[:, L - 1:L], (1, L))
        a = g - b + ig
        dmat = jnp.where(causal, b + ccol, -jnp.inf)
        inter = b + m_prev
        m_t = jnp.maximum(inter, jnp.max(dmat, axis=0, keepdims=True))
        scores = jnp.dot(kk, qt, preferred_element_type=F32) * jnp.exp(dmat - m_t)
        w_inter = jnp.exp(inter - m_t)
        ht = (w_inter * jnp.dot(state.astype(BF16), qt, preferred_element_type=F32)
              + jnp.dot(vaug, scores.astype(BF16), preferred_element_type=F32))
        den = ht[dv:dv + 1, :]
        hh = ht[:dv, :] / jnp.maximum(jnp.abs(den), jnp.exp(-m_t))
        mu = jnp.mean(hh, axis=0, keepdims=True)
        dlt = hh - mu
        var = jnp.mean(dlt * dlt, axis=0, keepdims=True)
        hn = dlt * lax.rsqrt(var + LN_EPS) * g_ref[hd * dv:(hd + 1) * dv, :]
        gate = jax.nn.sigmoid(ot_ref[bi, hd * dv:(hd + 1) * dv, :].astype(F32))
        h_ref[bi, hd * dv:(hd + 1) * dv, :] = (hn * gate).astype(BF16)

        m_new = jnp.maximum(g + m_prev, jnp.max(a, axis=1, keepdims=True))
        decay = jnp.exp(g + m_prev - m_new)
        w = jnp.exp(a - m_new)
        upd = jnp.dot((vaug.astype(F32) * w).astype(BF16), kk, preferred_element_type=F32)
        state_ref[bi, hd] = decay[:, :qk] * state + upd
        m_ref[bi, hd] = m_new


def _mlstm(mqt, mk3, mvt, mot, rows, cols3, gcol, *, chunk, nb):
    bsz, qkw, seq = mqt.shape
    d = mvt.shape[1]
    dv = d // MLSTM_HEADS
    qk = qkw // MLSTM_HEADS
    fmaj = lambda r: pl.BlockSpec((nb, r, chunk), lambda b, c: (b, 0, c))
    return pl.pallas_call(
        _mlstm_kernel,
        grid=(bsz // nb, seq // chunk),
        in_specs=[
            fmaj(qkw),
            pl.BlockSpec((nb, chunk, qkw), lambda b, c: (b, c, 0)),
            fmaj(d), fmaj(d), fmaj(GATE_ROWS),
            pl.BlockSpec((nb, chunk, GATE_ROWS), lambda b, c: (b, c, 0)),
            _const_spec(gcol.shape),
        ],
        out_specs=fmaj(d),
        out_shape=jax.ShapeDtypeStruct((bsz, d, seq), BF16),
        scratch_shapes=[pltpu.VMEM((nb, MLSTM_HEADS, dv + VAUG_PAD, qk), F32),
                        pltpu.VMEM((nb, MLSTM_HEADS, 1, chunk), F32)],
        compiler_params=pltpu.CompilerParams(dimension_semantics=("parallel", "arbitrary"),
                                             vmem_limit_bytes=VMEM_LIMIT_BYTES),
        name="mlstm",
    )(mqt, mk3, mvt, mot, rows, cols3, gcol)


def _tail_kernel(x_ref, mod_ref, att_ref, hm_ref, ga_ref, gb_ref, wpa_ref, wpb_ref, wout_ref,
                 l1g_ref, l1b_ref, wfi_ref, wfd_ref, l2g_ref, l2b_ref, o_ref, *, d_ff, nsub):
    g1 = mod_ref[0, 2:3, :]
    sh2 = mod_ref[0, 3:4, :]
    sc2 = mod_ref[0, 4:5, :]
    g2 = mod_ref[0, 5:6, :]
    sub = x_ref.shape[0] // nsub
    sl = [slice(k * sub, (k + 1) * sub) for k in range(nsub)]
    at = [jnp.dot(wpa_ref[...], att_ref[0, :, s], preferred_element_type=F32) for s in sl]
    bt = [jnp.dot(wpb_ref[...], hm_ref[0, :, s], preferred_element_type=F32) for s in sl]
    yt = [(jax.nn.sigmoid(ga_ref[0, :, s].astype(F32)) * a
           + jax.nn.sigmoid(gb_ref[0, :, s].astype(F32)) * b).astype(BF16) for s, a, b in zip(sl, at, bt)]
    z = [lax.dot_general(y, wout_ref[...], (((0,), (0,)), ((), ())), preferred_element_type=F32) for y in yt]
    x1 = [_ln(ALPHA * x_ref[s, :] + g1 * zz) * l1g_ref[...] + l1b_ref[...] for s, zz in zip(sl, z)]
    h2 = [(_ln(v) * (1.0 + sc2) + sh2).astype(BF16) for v in x1]
    gu = [jnp.dot(v, wfi_ref[...], preferred_element_type=F32) for v in h2]
    act = [(_silu(v[:, :d_ff]) * v[:, d_ff:]).astype(BF16) for v in gu]
    ff = [jnp.dot(v, wfd_ref[...], preferred_element_type=F32) for v in act]
    for s, v, f in zip(sl, x1, ff):
        o_ref[s, :] = _ln(ALPHA * v + g2 * f) * l2g_ref[...] + l2b_ref[...]


def _tail(x2, mod3, att_t, hm_t, ga_t, gb_t, wpa_t, wpb_t, w_out, l1g, l1b, wfi, wfd, l2g, l2b, *, seq, tm):
    tok, d = x2.shape
    nt = seq // tm
    d_ff = wfd.shape[0]
    fmaj = lambda rows: pl.BlockSpec((1, rows, tm), lambda t: (t // nt, 0, t % nt))
    consts = (wpa_t, wpb_t, w_out, l1g, l1b, wfi, wfd, l2g, l2b)
    return pl.pallas_call(
        functools.partial(_tail_kernel, d_ff=d_ff, nsub=2),
        grid=(tok // tm,),
        in_specs=[
            pl.BlockSpec((tm, d), lambda t: (t, 0)),
            pl.BlockSpec((1,) + mod3.shape[1:], lambda t: (t // nt, 0, 0)),
            fmaj(att_t.shape[1]), fmaj(d), fmaj(d), fmaj(d),
        ] + [_const_spec(a.shape) for a in consts],
        out_specs=pl.BlockSpec((tm, d), lambda t: (t, 0)),
        out_shape=jax.ShapeDtypeStruct((tok, d), F32),
        compiler_params=pltpu.CompilerParams(dimension_semantics=("parallel",), vmem_limit_bytes=VMEM_LIMIT_BYTES),
        name="tail",
    )(x2, mod3, att_t, hm_t, ga_t, gb_t, *consts)


def _pick(n, pref):
    while n % pref:
        pref //= 2
    return pref


def kernel(x, c, w_ada, b_ada, w_in, b_in, conv_w, conv_b, w_mq, w_mk, mh_norm_g, w_pa, w_pb, w_out,
           ln1_g, ln1_b, w_ffn_in, w_ffn_down, ln2_g, ln2_b):
    bsz, seq, d = x.shape
    tok = bsz * seq
    chunk = _pick(seq, 256)
    fox_blk = _pick(seq, 512)
    tm_in = _pick(seq, 512)
    tm_tail = _pick(seq, 512)

    x2 = x.reshape(tok, d)
    for l in range(DEPTH):
        mod3 = _modulation(c, w_ada[l], b_ada[l]).reshape(bsz, 6, d)

        splits = (FOX_WIDTH, FOX_WIDTH, FOX_WIDTH, FOX_HEADS, d, d, MLSTM_HEADS, MLSTM_HEADS, d, d, d)
        offs = [0]
        for s in splits:
            offs.append(offs[-1] + s)
        col = lambda i: (w_in[l][:, offs[i]:offs[i + 1]], b_in[l][offs[i]:offs[i + 1]])
        (wfq, bfq), (wfk, bfk), (wfv, bfv), (wff, bff), (wmu, bmu), (wmv, bmv), (wmi, bmi), (wmf, bmf), \
            (wmo, bmo), (wga, bga), (wgb, bgb) = [col(i) for i in range(len(splits))]
        scale = FOX_HEAD_DIM ** -0.5 * LOG2E
        wn = jnp.concatenate([wfk, wmu], axis=1).astype(BF16)
        bn = jnp.concatenate([bfk, bmu])[None, :]
        wt = jnp.concatenate([wfq * scale, wfv, wmv, wmo, wga, wgb, wff, wmi, wmf], axis=1).T.astype(BF16)
        bt = jnp.concatenate([bfq * scale, bfv, bmv, bmo, bga, bgb, bff, bmi, bmf])[:, None]

        d_ff = w_ffn_down.shape[1]
        tail_w = (w_ffn_in[l], w_ffn_down[l].reshape(d, d_ff), w_out[l])
        nsteps = tok // tm_in
        in_call = all(a.shape[0] % nsteps == 0 and (a.shape[0] // nsteps) % BF16_ROWS == 0 for a in tail_w)
        fk, mk, fqt, fvt, mqt, mvt, mot, gat, gbt, rows, cols, kf, qf, *cast = _inproj(
            x2, mod3, wn, bn, wt, bt, conv_w[l], conv_b[l][None, :], w_mk[l].astype(BF16),
            jnp.swapaxes(w_mq[l], 1, 2).astype(BF16), tail_w if in_call else (),
            bsz=bsz, seq=seq, tm=tm_in, chunk=chunk)
        wfi, wfd, wo = cast if in_call else [a.astype(BF16) for a in tail_w]
        wfd = wfd.reshape(d_ff, d)

        att_t = _fox_attention(fqt, fk.reshape(bsz, seq, FOX_WIDTH), fvt, kf.reshape(bsz, seq, FOX_WIDTH), qf,
                               blk=fox_blk)

        hm_t = _mlstm(mqt, mk.reshape(bsz, seq, -1), mvt, mot, rows, cols.reshape(bsz, seq, GATE_ROWS),
                      mh_norm_g[l][:, None], chunk=chunk, nb=2 if bsz % 2 == 0 else 1)

        x2 = _tail(x2, mod3, att_t, hm_t, gat, gbt, w_pa[l].T.astype(BF16), w_pb[l].T.astype(BF16),
                   wo, ln1_g[l][None, :], ln1_b[l][None, :], wfi, wfd, ln2_g[l][None, :], ln2_b[l][None, :],
                   seq=seq, tm=tm_tail)
    return x2.reshape(bsz, seq, d)
```

```python
import functools

import jax
import jax.numpy as jnp
from jax import lax
from jax.experimental import pallas as pl
from jax.experimental.pallas import tpu as pltpu

F32 = jnp.float32
BF16 = jnp.bfloat16

FOX_HEADS = 8
FOX_HEAD_DIM = 64
FOX_WIDTH = FOX_HEADS * FOX_HEAD_DIM
MLSTM_HEADS = 4
CONV_WIDTH = 4
LN_EPS = 1e-5
DEPTH = 1
ALPHA = (2.0 * DEPTH) ** 0.25
LOG2E = 1.4426950408889634

VMEM_LIMIT_BYTES = 56 * 1024 * 1024
SUBLANES = 8
BF16_ROWS = 16
GATE_ROWS = 16
VAUG_PAD = 16
FOX_AUG = 16
FOX_SCORE_LEAD = 3

def _const_spec(shape):
    nd = len(shape)
    return pl.BlockSpec(shape, lambda *_: (0,) * nd, pipeline_mode=pl.Buffered(1))


def _ln(v):
    mu = jnp.mean(v, axis=-1, keepdims=True)
    d = v - mu
    var = jnp.mean(d * d, axis=-1, keepdims=True)
    return d * lax.rsqrt(var + LN_EPS)


def _silu(v):
    return v * jax.nn.sigmoid(v)


def _mod_kernel(c_ref, w_ref, b_ref, o_ref):
    a = _silu(c_ref[...]).astype(BF16)
    o_ref[...] = jnp.dot(a, w_ref[...].astype(BF16), preferred_element_type=F32) + b_ref[...]


def _modulation(c, w_ada, b_ada):
    bsz, d = c.shape
    n = w_ada.shape[1]
    return pl.pallas_call(
        _mod_kernel,
        grid=(n // d,),
        in_specs=[
            pl.BlockSpec((bsz, d), lambda j: (0, 0)),
            pl.BlockSpec((d, d), lambda j: (0, j)),
            pl.BlockSpec((1, d), lambda j: (0, j)),
        ],
        out_specs=pl.BlockSpec((bsz, d), lambda j: (0, j)),
        out_shape=jax.ShapeDtypeStruct((bsz, n), F32),
        name="modulation",
    )(c, w_ada, b_ada.reshape(1, n))


def _prefix_sum_lanes(v, width):
    pos = lax.broadcasted_iota(jnp.int32, v.shape, 1) & (width - 1)
    d = 1
    while d < width:
        v = v + jnp.where(pos >= d, pltpu.roll(v, d, axis=1), 0.0)
        d *= 2
    return v


INPROJ_INPUTS = 10
INPROJ_OUTPUTS = 13


def _inproj_kernel(*refs, tiles_per_seq, chunk, d_model):
    (x_ref, mod_ref, wn_ref, bn_ref, wt_ref, bt_ref, cw_ref, cb_ref, wmk_ref, wmqt_ref) = refs[:INPROJ_INPUTS]
    n_side = (len(refs) - INPROJ_INPUTS - INPROJ_OUTPUTS - 2) // 2
    side_in = refs[INPROJ_INPUTS:INPROJ_INPUTS + n_side]
    outs = refs[INPROJ_INPUTS + n_side:]
    (fk_ref, mk_ref, fqt_ref, fvt_ref, mqt_ref, mvt_ref, mot_ref, gat_ref, gbt_ref,
     rows_ref, cols_ref, kf_ref, qf_ref) = outs[:INPROJ_OUTPUTS]
    side_out = outs[INPROJ_OUTPUTS:INPROJ_OUTPUTS + n_side]
    prev_ref, fcarry_ref = outs[INPROJ_OUTPUTS + n_side:]

    for src, dst in zip(side_in, side_out):
        dst[...] = src[0].astype(BF16)

    tm = x_ref.shape[0]
    first = (pl.program_id(0) % tiles_per_seq) == 0
    sh1 = mod_ref[0, 0:1, :]
    sc1 = mod_ref[0, 1:2, :]
    h = (_ln(x_ref[...]) * (1.0 + sc1) + sh1).astype(BF16)

    @pl.when(first)
    def _():
        prev_ref[...] = jnp.zeros_like(prev_ref)
        fcarry_ref[...] = jnp.zeros_like(fcarry_ref)

    def seg(r0, n):
        y = lax.dot_general(wt_ref[r0:r0 + n, :], h, (((1,), (1,)), ((), ())), preferred_element_type=F32)
        return y + bt_ref[r0:r0 + n, :]

    yn = jnp.dot(h, wn_ref[...], preferred_element_type=F32) + bn_ref[...]
    fk_ref[...] = yn[:, :FOX_WIDTH].astype(BF16)
    mu = yn[:, FOX_WIDTH:]

    g = seg(wt_ref.shape[0] - GATE_ROWS, GATE_ROWS)
    ls = jax.nn.log_sigmoid(g)
    fcum = _prefix_sum_lanes(ls[0:FOX_HEADS] * LOG2E, tm) + fcarry_ref[:, 0:1]
    fcarry_ref[...] = jnp.broadcast_to(fcum[:, tm - 1:tm], fcarry_ref.shape)
    bcum = _prefix_sum_lanes(ls[FOX_HEADS:], chunk)[MLSTM_HEADS:]
    ig = g[FOX_HEADS:FOX_HEADS + MLSTM_HEADS]
    rows_ref[0] = jnp.concatenate([fcum, bcum, ig], axis=0)
    colsT = jnp.concatenate([fcum, ig - bcum, jnp.zeros_like(ig)], axis=0)
    cols_ref[...] = colsT.T

    f_hi = fcum.astype(BF16).astype(F32)
    f_mid = (fcum - f_hi).astype(BF16).astype(F32)
    f_lo = (fcum - f_hi - f_mid).astype(BF16).astype(F32)
    j = lax.broadcasted_iota(jnp.int32, (FOX_AUG, tm), 0)
    for hd in range(FOX_HEADS):
        odd = hd % 2
        ones_at = (j >= 3 + 3 * odd) & (j < 6 + 3 * odd)
        blk16 = jnp.where(j == 0, f_hi[hd:hd + 1], jnp.where(j == 1, f_mid[hd:hd + 1], jnp.where(
            j == 2, f_lo[hd:hd + 1], jnp.where(ones_at, 1.0, 0.0))))
        qf_ref[0, hd * FOX_AUG:(hd + 1) * FOX_AUG, :] = blk16.astype(BF16)
    kf_ref[...] = jnp.zeros_like(kf_ref)
    for pr in range(FOX_HEADS // 2):
        ea, eb = 2 * pr, 2 * pr + 1
        blk16 = jnp.where(j < 3, 1.0, 0.0)
        for entry, term in ((3, f_hi[ea:ea + 1]), (4, f_mid[ea:ea + 1]), (5, f_lo[ea:ea + 1]),
                            (6, f_hi[eb:eb + 1]), (7, f_mid[eb:eb + 1]), (8, f_lo[eb:eb + 1])):
            blk16 = jnp.where(j == entry, -term, blk16)
        kf_ref[:, pr * 2 * FOX_HEAD_DIM:pr * 2 * FOX_HEAD_DIM + FOX_AUG] = blk16.T.astype(BF16)

    r = 0
    for ref, n in ((fqt_ref, FOX_WIDTH), (fvt_ref, FOX_WIDTH), (mvt_ref, d_model), (mot_ref, d_model),
                   (gat_ref, d_model), (gbt_ref, d_model)):
        ref[0] = seg(r, n).astype(BF16)
        r += n

    u = cb_ref[...] + cw_ref[CONV_WIDTH - 1:CONV_WIDTH, :] * mu
    for k in range(1, CONV_WIDTH):
        u = u + cw_ref[CONV_WIDTH - 1 - k:CONV_WIDTH - k, :] * pltpu.roll(mu, k, axis=0)
    head = jnp.concatenate([prev_ref[...], mu[:SUBLANES]], axis=0)
    u_head = cb_ref[...] + cw_ref[0:1, :] * head[SUBLANES - 3:2 * SUBLANES - 3]
    for j in range(1, CONV_WIDTH):
        off = SUBLANES - (CONV_WIDTH - 1) + j
        u_head = u_head + cw_ref[j:j + 1, :] * head[off:off + SUBLANES]
    u = jnp.concatenate([u_head, u[SUBLANES:]], axis=0)
    prev_ref[...] = mu[tm - SUBLANES:, :]
    u = _silu(u).astype(BF16)

    inner = d_model // MLSTM_HEADS
    qk = wmk_ref.shape[2]
    for hd in range(MLSTM_HEADS):
        uh = u[:, hd * inner:(hd + 1) * inner]
        mk_ref[:, hd * qk:(hd + 1) * qk] = jnp.dot(uh, wmk_ref[hd], preferred_element_type=F32).astype(BF16)
        qt = lax.dot_general(wmqt_ref[hd], uh, (((1,), (1,)), ((), ())), preferred_element_type=F32)
        mqt_ref[0, hd * qk:(hd + 1) * qk, :] = (qt * (qk ** -0.5)).astype(BF16)


def _slab_rows(rows, nsteps):
    steps = 1
    while steps <= nsteps:
        if nsteps % steps == 0 and (rows * steps) % nsteps == 0 and (rows * steps // nsteps) % BF16_ROWS == 0:
            return rows * steps // nsteps, steps
        steps *= 2
    return None


def _inproj(x2, mod3, wn, bn, wt, bt, conv_w, conv_b, w_mk, w_mqt, side, layer, *, bsz, seq, tm, chunk):
    tok, d = x2.shape
    nt = seq // tm
    nsteps = tok // tm
    qkw = w_mk.shape[0] * w_mk.shape[2]
    tmaj = lambda width: pl.BlockSpec((tm, width), lambda t: (t, 0))
    fmaj = lambda rows: pl.BlockSpec((1, rows, tm), lambda t: (t // nt, 0, t % nt))

    def slab_in(a):
        rows, steps = _slab_rows(a.shape[1], nsteps)
        return pl.BlockSpec((1, rows, a.shape[2]), lambda t: (layer, t // steps, 0))

    def slab_out(a):
        rows, steps = _slab_rows(a.shape[1], nsteps)
        return pl.BlockSpec((rows, a.shape[2]), lambda t: (t // steps, 0))

    out_shape = (
        jax.ShapeDtypeStruct((tok, FOX_WIDTH), BF16),
        jax.ShapeDtypeStruct((tok, qkw), BF16),
        jax.ShapeDtypeStruct((bsz, FOX_WIDTH, seq), BF16),
        jax.ShapeDtypeStruct((bsz, FOX_WIDTH, seq), BF16),
        jax.ShapeDtypeStruct((bsz, qkw, seq), BF16),
        jax.ShapeDtypeStruct((bsz, d, seq), BF16),
        jax.ShapeDtypeStruct((bsz, d, seq), BF16),
        jax.ShapeDtypeStruct((bsz, d, seq), BF16),
        jax.ShapeDtypeStruct((bsz, d, seq), BF16),
        jax.ShapeDtypeStruct((bsz, GATE_ROWS, seq), F32),
        jax.ShapeDtypeStruct((tok, GATE_ROWS), F32),
        jax.ShapeDtypeStruct((tok, FOX_WIDTH), BF16),
        jax.ShapeDtypeStruct((bsz, FOX_HEADS * FOX_AUG, seq), BF16),
    ) + tuple(jax.ShapeDtypeStruct(a.shape[1:], BF16) for a in side)
    out_specs = (tmaj(FOX_WIDTH), tmaj(qkw), fmaj(FOX_WIDTH), fmaj(FOX_WIDTH), fmaj(qkw),
                 fmaj(d), fmaj(d), fmaj(d), fmaj(d), fmaj(GATE_ROWS), tmaj(GATE_ROWS),
                 tmaj(FOX_WIDTH), fmaj(FOX_HEADS * FOX_AUG)) + tuple(slab_out(a) for a in side)
    kern = functools.partial(_inproj_kernel, tiles_per_seq=nt, chunk=chunk, d_model=d)
    return pl.pallas_call(
        kern,
        grid=(tok // tm,),
        in_specs=[
            pl.BlockSpec((tm, d), lambda t: (t, 0)),
            pl.BlockSpec((1,) + mod3.shape[1:], lambda t: (t // nt, 0, 0)),
            _const_spec(wn.shape), _const_spec(bn.shape), _const_spec(wt.shape), _const_spec(bt.shape),
            _const_spec(conv_w.shape), _const_spec(conv_b.shape), _const_spec(w_mk.shape), _const_spec(w_mqt.shape),
        ] + [slab_in(a) for a in side],
        out_specs=out_specs,
        out_shape=out_shape,
        scratch_shapes=[pltpu.VMEM((SUBLANES, d), F32), pltpu.VMEM((FOX_HEADS, 128), F32)],
        compiler_params=pltpu.CompilerParams(dimension_semantics=("arbitrary",), vmem_limit_bytes=VMEM_LIMIT_BYTES),
        name="inproj",
    )(x2, mod3, wn, bn, wt, bt, conv_w, conv_b, w_mk, w_mqt, *side)


def _fox_scores(qi, a, qt_ref, k_ref, kf_ref, qf_ref, blk):
    hd = FOX_HEAD_DIM
    half = blk // 2
    q0 = qi * blk
    main = q0 + half
    keys = q0 + blk
    qt2 = qt_ref[0, :, q0:keys]
    row = lax.broadcasted_iota(jnp.int32, qt2.shape, 0)
    tri_main = (lax.broadcasted_iota(jnp.int32, (half, blk), 0)
                <= lax.broadcasted_iota(jnp.int32, (half, blk), 1))
    kb = jnp.concatenate([k_ref[0, 0:keys, :], kf_ref[0, 0:keys, :]], axis=1)
    qm = jnp.where((row >= a * hd) & (row < (a + 1) * hd), qt2, jnp.zeros_like(qt2))
    qm = jnp.concatenate([qm, qf_ref[0, a * FOX_AUG:(a + 1) * FOX_AUG, q0:keys],
                          jnp.zeros((2 * hd - FOX_AUG, blk), BF16)], axis=0)
    s_main = jnp.dot(kb[:main], qm, preferred_element_type=F32)
    s_last = jnp.dot(kb[main:], qm[:, half:], preferred_element_type=F32)
    s_edge = jnp.where(tri_main, s_main[q0:], -jnp.inf)
    s_last = jnp.where(tri_main[:, :half], s_last, -jnp.inf)
    m = jnp.max(s_edge, axis=0, keepdims=True)
    if q0:
        m = jnp.maximum(m, jnp.max(s_main[:q0], axis=0, keepdims=True))
    m_hi = jnp.maximum(m[:, half:], jnp.max(s_last, axis=0, keepdims=True))
    m = jnp.concatenate([m[:, :half], m_hi], axis=1)
    return s_main[:q0] if q0 else None, s_edge, s_last, m, m_hi


def _fox_values(qi, a, scores, vt_ref, o_ref, blk):
    hd = FOX_HEAD_DIM
    half = blk // 2
    q0 = qi * blk
    main = q0 + half
    keys = q0 + blk
    s_top, s_edge, s_last, m, m_hi = scores
    ones = (lax.broadcasted_iota(jnp.int32, (VAUG_PAD, keys), 0) == 0).astype(BF16)
    vaug = jnp.concatenate([vt_ref[0, a * hd:(a + 1) * hd, 0:keys], ones], axis=0)
    acc = jnp.dot(vaug[:, q0:main], jnp.exp2(s_edge - m).astype(BF16), preferred_element_type=F32)
    if q0:
        acc = acc + jnp.dot(vaug[:, :q0], jnp.exp2(s_top - m).astype(BF16), preferred_element_type=F32)
    acc_hi = jnp.dot(vaug[:, main:], jnp.exp2(s_last - m_hi).astype(BF16), preferred_element_type=F32)
    acc = jnp.concatenate([acc[:, :half], acc[:, half:] + acc_hi], axis=1)
    o_ref[0, a * hd:(a + 1) * hd, q0:keys] = (acc[:hd] / acc[hd:hd + 1]).astype(BF16)


def _fox_kernel(qt_ref, k_ref, vt_ref, kf_ref, qf_ref, o_ref, *, blk):
    units = [(qi, a) for qi in reversed(range(k_ref.shape[1] // blk)) for a in range(2)]
    pending = []
    for qi, a in units:
        pending.append((qi, a, _fox_scores(qi, a, qt_ref, k_ref, kf_ref, qf_ref, blk)))
        if len(pending) > FOX_SCORE_LEAD:
            _fox_values(*pending.pop(0), vt_ref, o_ref, blk)
    for unit in pending:
        _fox_values(*unit, vt_ref, o_ref, blk)


def _fox_attention(fqt, fk3, fvt, kf3, qf, *, blk):
    bsz, width, seq = fqt.shape
    pairs = FOX_HEADS // 2
    pw = 2 * FOX_HEAD_DIM
    return pl.pallas_call(
        functools.partial(_fox_kernel, blk=blk),
        grid=(bsz, pairs),
        in_specs=[
            pl.BlockSpec((1, pw, seq), lambda b, p: (b, p, 0)),
            pl.BlockSpec((1, seq, pw), lambda b, p: (b, 0, p)),
            pl.BlockSpec((1, pw, seq), lambda b, p: (b, p, 0)),
            pl.BlockSpec((1, seq, pw), lambda b, p: (b, 0, p)),
            pl.BlockSpec((1, 2 * FOX_AUG, seq), lambda b, p: (b, p, 0)),
        ],
        out_specs=pl.BlockSpec((1, pw, seq), lambda b, p: (b, p, 0)),
        out_shape=jax.ShapeDtypeStruct((bsz, width, seq), BF16),
        compiler_params=pltpu.CompilerParams(dimension_semantics=("parallel", "parallel"),
                                             vmem_limit_bytes=VMEM_LIMIT_BYTES),
        name="fox_attention",
    )(fqt, fk3, fvt, kf3, qf)


def _mlstm_kernel(qt_ref, k_ref, vt_ref, ot_ref, rows_ref, cols_ref, g_ref, h_ref, state_ref, m_ref):
    c = pl.program_id(1)
    L = qt_ref.shape[2]
    qk = k_ref.shape[2] // MLSTM_HEADS
    dv = vt_ref.shape[1] // MLSTM_HEADS

    @pl.when(c == 0)
    def _():
        state_ref[...] = jnp.zeros_like(state_ref)
        m_ref[...] = jnp.zeros_like(m_ref)

    src = lax.broadcasted_iota(jnp.int32, (L, L), 0)
    dst = lax.broadcasted_iota(jnp.int32, (L, L), 1)
    causal = src <= dst
    ones_rows = (lax.broadcasted_iota(jnp.int32, (VAUG_PAD, L), 0) == 0).astype(BF16)

    for bi, hd in [(bi, hd) for bi in range(qt_ref.shape[0]) for hd in range(MLSTM_HEADS)]:
        qt = qt_ref[bi, hd * qk:(hd + 1) * qk, :]
        kk = k_ref[bi, :, hd * qk:(hd + 1) * qk]
        vaug = jnp.concatenate([vt_ref[bi, hd * dv:(hd + 1) * dv, :], ones_rows], axis=0)
        b = rows_ref[bi, FOX_HEADS + hd:FOX_HEADS + hd + 1, :]
        ig = rows_ref[bi, FOX_HEADS + MLSTM_HEADS + hd:FOX_HEADS + MLSTM_HEADS + hd + 1, :]
        ccol = cols_ref[bi, :, FOX_HEADS + hd:FOX_HEADS + hd + 1]
        m_prev = m_ref[bi, hd]
        state = state_ref[bi, hd]

        g = jnp.broadcast_to(b[:, L - 1:L], (1, L))
        a = g - b + ig
        dmat = jnp.where(causal, b + ccol, -jnp.inf)
        inter = b + m_prev
        m_t = jnp.maximum(inter, jnp.max(dmat, axis=0, keepdims=True))
        scores = jnp.dot(kk, qt, preferred_element_type=F32) * jnp.exp(dmat - m_t)
        w_inter = jnp.exp(inter - m_t)
        ht = (w_inter * jnp.dot(state.astype(BF16), qt, preferred_element_type=F32)
              + jnp.dot(vaug, scores.astype(BF16), preferred_element_type=F32))
        den = ht[dv:dv + 1, :]
        hh = ht[:dv, :] / jnp.maximum(jnp.abs(den), jnp.exp(-m_t))
        mu = jnp.mean(hh, axis=0, keepdims=True)
        dlt = hh - mu
        var = jnp.mean(dlt * dlt, axis=0, keepdims=True)
        hn = dlt * lax.rsqrt(var + LN_EPS) * g_ref[hd * dv:(hd + 1) * dv, :]
        gate = jax.nn.sigmoid(ot_ref[bi, hd * dv:(hd + 1) * dv, :].astype(F32))
        h_ref[bi, hd * dv:(hd + 1) * dv, :] = (hn * gate).astype(BF16)

        m_new = jnp.maximum(g + m_prev, jnp.max(a, axis=1, keepdims=True))
        decay = jnp.exp(g + m_prev - m_new)
        w = jnp.exp(a - m_new)
        upd = jnp.dot((vaug.astype(F32) * w).astype(BF16), kk, preferred_element_type=F32)
        state_ref[bi, hd] = decay[:, :qk] * state + upd
        m_ref[bi, hd] = m_new


def _mlstm(mqt, mk3, mvt, mot, rows, cols3, gcol, *, chunk, nb):
    bsz, qkw, seq = mqt.shape
    d = mvt.shape[1]
    dv = d // MLSTM_HEADS
    qk = qkw // MLSTM_HEADS
    fmaj = lambda r: pl.BlockSpec((nb, r, chunk), lambda b, c: (b, 0, c))
    return pl.pallas_call(
        _mlstm_kernel,
        grid=(bsz // nb, seq // chunk),
        in_specs=[
            fmaj(qkw),
            pl.BlockSpec((nb, chunk, qkw), lambda b, c: (b, c, 0)),
            fmaj(d), fmaj(d), fmaj(GATE_ROWS),
            pl.BlockSpec((nb, chunk, GATE_ROWS), lambda b, c: (b, c, 0)),
            _const_spec(gcol.shape),
        ],
        out_specs=fmaj(d),
        out_shape=jax.ShapeDtypeStruct((bsz, d, seq), BF16),
        scratch_shapes=[pltpu.VMEM((nb, MLSTM_HEADS, dv + VAUG_PAD, qk), F32),
                        pltpu.VMEM((nb, MLSTM_HEADS, 1, chunk), F32)],
        compiler_params=pltpu.CompilerParams(dimension_semantics=("parallel", "arbitrary"),
                                             vmem_limit_bytes=VMEM_LIMIT_BYTES),
        name="mlstm",
    )(mqt, mk3, mvt, mot, rows, cols3, gcol)


def _tail_kernel(x_ref, mod_ref, att_ref, hm_ref, ga_ref, gb_ref, wpa_ref, wpb_ref, wout_ref,
                 l1g_ref, l1b_ref, wfi_ref, wfd_ref, l2g_ref, l2b_ref, o_ref, *, d_ff, nsub):
    g1 = mod_ref[0, 2:3, :]
    sh2 = mod_ref[0, 3:4, :]
    sc2 = mod_ref[0, 4:5, :]
    g2 = mod_ref[0, 5:6, :]
    sub = x_ref.shape[0] // nsub
    sl = [slice(k * sub, (k + 1) * sub) for k in range(nsub)]
    at = [jnp.dot(wpa_ref[...], att_ref[0, :, s], preferred_element_type=F32) for s in sl]
    bt = [jnp.dot(wpb_ref[...], hm_ref[0, :, s], preferred_element_type=F32) for s in sl]
    yt = [(jax.nn.sigmoid(ga_ref[0, :, s].astype(F32)) * a
           + jax.nn.sigmoid(gb_ref[0, :, s].astype(F32)) * b).astype(BF16) for s, a, b in zip(sl, at, bt)]
    z = [lax.dot_general(y, wout_ref[...], (((0,), (0,)), ((), ())), preferred_element_type=F32) for y in yt]
    x1 = [_ln(ALPHA * x_ref[s, :] + g1 * zz) * l1g_ref[...] + l1b_ref[...] for s, zz in zip(sl, z)]
    h2 = [(_ln(v) * (1.0 + sc2) + sh2).astype(BF16) for v in x1]
    gu = [jnp.dot(v, wfi_ref[...], preferred_element_type=F32) for v in h2]
    act = [(_silu(v[:, :d_ff]) * v[:, d_ff:]).astype(BF16) for v in gu]
    ff = [jnp.dot(v, wfd_ref[...], preferred_element_type=F32) for v in act]
    for s, v, f in zip(sl, x1, ff):
        o_ref[s, :] = _ln(ALPHA * v + g2 * f) * l2g_ref[...] + l2b_ref[...]


def _tail(x2, mod3, att_t, hm_t, ga_t, gb_t, wpa_t, wpb_t, w_out, l1g, l1b, wfi, wfd, l2g, l2b, *, seq, tm):
    tok, d = x2.shape
    nt = seq // tm
    d_ff = wfd.shape[0]
    fmaj = lambda rows: pl.BlockSpec((1, rows, tm), lambda t: (t // nt, 0, t % nt))
    consts = (wpa_t, wpb_t, w_out, l1g, l1b, wfi, wfd, l2g, l2b)
    return pl.pallas_call(
        functools.partial(_tail_kernel, d_ff=d_ff, nsub=2),
        grid=(tok // tm,),
        in_specs=[
            pl.BlockSpec((tm, d), lambda t: (t, 0)),
            pl.BlockSpec((1,) + mod3.shape[1:], lambda t: (t // nt, 0, 0)),
            fmaj(att_t.shape[1]), fmaj(d), fmaj(d), fmaj(d),
        ] + [_const_spec(a.shape) for a in consts],
        out_specs=pl.BlockSpec((tm, d), lambda t: (t, 0)),
        out_shape=jax.ShapeDtypeStruct((tok, d), F32),
        compiler_params=pltpu.CompilerParams(dimension_semantics=("parallel",), vmem_limit_bytes=VMEM_LIMIT_BYTES),
        name="tail",
    )(x2, mod3, att_t, hm_t, ga_t, gb_t, *consts)


def _pick(n, pref):
    while n % pref:
        pref //= 2
    return pref


def kernel(x, c, w_ada, b_ada, w_in, b_in, conv_w, conv_b, w_mq, w_mk, mh_norm_g, w_pa, w_pb, w_out,
           ln1_g, ln1_b, w_ffn_in, w_ffn_down, ln2_g, ln2_b):
    bsz, seq, d = x.shape
    tok = bsz * seq
    chunk = _pick(seq, 256)
    fox_blk = _pick(seq, 512)
    tm_in = _pick(seq, 512)
    tm_tail = _pick(seq, 512)

    x2 = x.reshape(tok, d)
    for l in range(DEPTH):
        mod3 = _modulation(c, w_ada[l], b_ada[l]).reshape(bsz, 6, d)

        splits = (FOX_WIDTH, FOX_WIDTH, FOX_WIDTH, FOX_HEADS, d, d, MLSTM_HEADS, MLSTM_HEADS, d, d, d)
        offs = [0]
        for s in splits:
            offs.append(offs[-1] + s)
        col = lambda i: (w_in[l][:, offs[i]:offs[i + 1]], b_in[l][offs[i]:offs[i + 1]])
        (wfq, bfq), (wfk, bfk), (wfv, bfv), (wff, bff), (wmu, bmu), (wmv, bmv), (wmi, bmi), (wmf, bmf), \
            (wmo, bmo), (wga, bga), (wgb, bgb) = [col(i) for i in range(len(splits))]
        scale = FOX_HEAD_DIM ** -0.5 * LOG2E
        wn = jnp.concatenate([wfk, wmu], axis=1).astype(BF16)
        bn = jnp.concatenate([bfk, bmu])[None, :]
        wt = jnp.concatenate([wfq * scale, wfv, wmv, wmo, wga, wgb, wff, wmi, wmf], axis=1).T.astype(BF16)
        bt = jnp.concatenate([bfq * scale, bfv, bmv, bmo, bga, bgb, bff, bmi, bmf])[:, None]

        tail_w = (w_ffn_in, w_ffn_down, w_out)
        in_call = all(_slab_rows(a.shape[1], tok // tm_in) is not None for a in tail_w)
        fk, mk, fqt, fvt, mqt, mvt, mot, gat, gbt, rows, cols, kf, qf, *cast = _inproj(
            x2, mod3, wn, bn, wt, bt, conv_w[l], conv_b[l][None, :], w_mk[l].astype(BF16),
            jnp.swapaxes(w_mq[l], 1, 2).astype(BF16), tail_w if in_call else (), l,
            bsz=bsz, seq=seq, tm=tm_in, chunk=chunk)
        wfi, wfd, wo = cast if in_call else [a[l].astype(BF16) for a in tail_w]

        att_t = _fox_attention(fqt, fk.reshape(bsz, seq, FOX_WIDTH), fvt, kf.reshape(bsz, seq, FOX_WIDTH), qf,
                               blk=fox_blk)

        hm_t = _mlstm(mqt, mk.reshape(bsz, seq, -1), mvt, mot, rows, cols.reshape(bsz, seq, GATE_ROWS),
                      mh_norm_g[l][:, None], chunk=chunk, nb=2 if bsz % 2 == 0 else 1)

        x2 = _tail(x2, mod3, att_t, hm_t, gat, gbt, w_pa[l].T.astype(BF16), w_pb[l].T.astype(BF16),
                   wo, ln1_g[l][None, :], ln1_b[l][None, :], wfi, wfd, ln2_g[l][None, :], ln2_b[l][None, :],
                   seq=seq, tm=tm_tail)
    return x2.reshape(bsz, seq, d)
```

```python
import functools

import jax
import jax.numpy as jnp
from jax import lax
from jax.experimental import pallas as pl
from jax.experimental.pallas import tpu as pltpu

F32 = jnp.float32
BF16 = jnp.bfloat16

FOX_HEADS = 8
FOX_HEAD_DIM = 64
FOX_WIDTH = FOX_HEADS * FOX_HEAD_DIM
MLSTM_HEADS = 4
CONV_WIDTH = 4
LN_EPS = 1e-5
DEPTH = 1
ALPHA = (2.0 * DEPTH) ** 0.25
LOG2E = 1.4426950408889634

VMEM_LIMIT_BYTES = 56 * 1024 * 1024
SUBLANES = 8
BF16_ROWS = 16
GATE_ROWS = 16
VAUG_PAD = 16
FOX_AUG = 16
FOX_SCORE_LEAD = 3

def _const_spec(shape):
    nd = len(shape)
    return pl.BlockSpec(shape, lambda *_: (0,) * nd, pipeline_mode=pl.Buffered(1))


def _ln(v):
    mu = jnp.mean(v, axis=-1, keepdims=True)
    d = v - mu
    var = jnp.mean(d * d, axis=-1, keepdims=True)
    return d * lax.rsqrt(var + LN_EPS)


def _silu(v):
    return v * jax.nn.sigmoid(v)


def _mod_kernel(c_ref, w_ref, b_ref, o_ref):
    a = _silu(c_ref[...]).astype(BF16)
    o_ref[...] = jnp.dot(a, w_ref[...].astype(BF16), preferred_element_type=F32) + b_ref[...]


def _modulation(c, w_ada, b_ada):
    bsz, d = c.shape
    n = w_ada.shape[1]
    return pl.pallas_call(
        _mod_kernel,
        grid=(n // d,),
        in_specs=[
            pl.BlockSpec((bsz, d), lambda j: (0, 0)),
            pl.BlockSpec((d, d), lambda j: (0, j)),
            pl.BlockSpec((1, d), lambda j: (0, j)),
        ],
        out_specs=pl.BlockSpec((bsz, d), lambda j: (0, j)),
        out_shape=jax.ShapeDtypeStruct((bsz, n), F32),
        name="modulation",
    )(c, w_ada, b_ada.reshape(1, n))


def _prefix_sum_lanes(v, width):
    pos = lax.broadcasted_iota(jnp.int32, v.shape, 1) & (width - 1)
    d = 1
    while d < width:
        v = v + jnp.where(pos >= d, pltpu.roll(v, d, axis=1), 0.0)
        d *= 2
    return v


INPROJ_INPUTS = 10
INPROJ_OUTPUTS = 13


def _inproj_kernel(*refs, tiles_per_seq, chunk, d_model):
    (x_ref, mod_ref, wn_ref, bn_ref, wt_ref, bt_ref, cw_ref, cb_ref, wmk_ref, wmqt_ref) = refs[:INPROJ_INPUTS]
    n_side = (len(refs) - INPROJ_INPUTS - INPROJ_OUTPUTS - 2) // 2
    side_in = refs[INPROJ_INPUTS:INPROJ_INPUTS + n_side]
    outs = refs[INPROJ_INPUTS + n_side:]
    (fk_ref, mk_ref, fqt_ref, fvt_ref, mqt_ref, mvt_ref, mot_ref, gat_ref, gbt_ref,
     rows_ref, cols_ref, kf_ref, qf_ref) = outs[:INPROJ_OUTPUTS]
    side_out = outs[INPROJ_OUTPUTS:INPROJ_OUTPUTS + n_side]
    prev_ref, fcarry_ref = outs[INPROJ_OUTPUTS + n_side:]

    for src, dst in zip(side_in, side_out):
        dst[...] = src[0].astype(BF16)

    tm = x_ref.shape[0]
    first = (pl.program_id(0) % tiles_per_seq) == 0
    sh1 = mod_ref[0, 0:1, :]
    sc1 = mod_ref[0, 1:2, :]
    @pl.when(first)
    def _():
        prev_ref[...] = jnp.zeros_like(prev_ref)
        fcarry_ref[...] = jnp.zeros_like(fcarry_ref)

    halves = [slice(k * (tm // 2), (k + 1) * (tm // 2)) for k in range(2)]
    h_parts, yn_parts = [], []
    for rows in halves:
        h_parts.append((_ln(x_ref[rows, :]) * (1.0 + sc1) + sh1).astype(BF16))
        yn_parts.append(jnp.dot(h_parts[-1], wn_ref[...], preferred_element_type=F32) + bn_ref[...])
    h = jnp.concatenate(h_parts, axis=0)
    yn = jnp.concatenate(yn_parts, axis=0)

    def seg(r0, n):
        y = lax.dot_general(wt_ref[r0:r0 + n, :], h, (((1,), (1,)), ((), ())), preferred_element_type=F32)
        return y + bt_ref[r0:r0 + n, :]

    fk_ref[...] = yn[:, :FOX_WIDTH].astype(BF16)
    mu = yn[:, FOX_WIDTH:]

    gq = seg(0, GATE_ROWS + FOX_WIDTH)
    fqt_ref[0] = gq[GATE_ROWS:].astype(BF16)
    g = gq[:GATE_ROWS]
    ls = jax.nn.log_sigmoid(g)
    fcum = _prefix_sum_lanes(ls[0:FOX_HEADS] * LOG2E, tm) + fcarry_ref[:, 0:1]
    fcarry_ref[...] = jnp.broadcast_to(fcum[:, tm - 1:tm], fcarry_ref.shape)
    bcum = _prefix_sum_lanes(ls[FOX_HEADS:], chunk)[MLSTM_HEADS:]
    ig = g[FOX_HEADS:FOX_HEADS + MLSTM_HEADS]
    rows_ref[0] = jnp.concatenate([fcum, bcum, ig], axis=0)
    colsT = jnp.concatenate([fcum, ig - bcum, jnp.zeros_like(ig)], axis=0)
    cols_ref[...] = colsT.T

    f_hi = fcum.astype(BF16).astype(F32)
    f_mid = (fcum - f_hi).astype(BF16).astype(F32)
    f_lo = (fcum - f_hi - f_mid).astype(BF16).astype(F32)
    j = lax.broadcasted_iota(jnp.int32, (FOX_AUG, tm), 0)
    for hd in range(FOX_HEADS):
        odd = hd % 2
        ones_at = (j >= 3 + 3 * odd) & (j < 6 + 3 * odd)
        blk16 = jnp.where(j == 0, f_hi[hd:hd + 1], jnp.where(j == 1, f_mid[hd:hd + 1], jnp.where(
            j == 2, f_lo[hd:hd + 1], jnp.where(ones_at, 1.0, 0.0))))
        qf_ref[0, hd * FOX_AUG:(hd + 1) * FOX_AUG, :] = blk16.astype(BF16)
    kf_ref[...] = jnp.zeros_like(kf_ref)
    for pr in range(FOX_HEADS // 2):
        ea, eb = 2 * pr, 2 * pr + 1
        blk16 = jnp.where(j < 3, 1.0, 0.0)
        for entry, term in ((3, f_hi[ea:ea + 1]), (4, f_mid[ea:ea + 1]), (5, f_lo[ea:ea + 1]),
                            (6, f_hi[eb:eb + 1]), (7, f_mid[eb:eb + 1]), (8, f_lo[eb:eb + 1])):
            blk16 = jnp.where(j == entry, -term, blk16)
        kf_ref[:, pr * 2 * FOX_HEAD_DIM:pr * 2 * FOX_HEAD_DIM + FOX_AUG] = blk16.T.astype(BF16)

    r = GATE_ROWS + FOX_WIDTH
    for ref, n in ((fvt_ref, FOX_WIDTH), (mvt_ref, d_model), (mot_ref, d_model),
                   (gat_ref, d_model), (gbt_ref, d_model)):
        ref[0] = seg(r, n).astype(BF16)
        r += n

    u = cb_ref[...] + cw_ref[CONV_WIDTH - 1:CONV_WIDTH, :] * mu
    for k in range(1, CONV_WIDTH):
        u = u + cw_ref[CONV_WIDTH - 1 - k:CONV_WIDTH - k, :] * pltpu.roll(mu, k, axis=0)
    head = jnp.concatenate([prev_ref[...], mu[:SUBLANES]], axis=0)
    u_head = cb_ref[...] + cw_ref[0:1, :] * head[SUBLANES - 3:2 * SUBLANES - 3]
    for j in range(1, CONV_WIDTH):
        off = SUBLANES - (CONV_WIDTH - 1) + j
        u_head = u_head + cw_ref[j:j + 1, :] * head[off:off + SUBLANES]
    u = jnp.concatenate([u_head, u[SUBLANES:]], axis=0)
    prev_ref[...] = mu[tm - SUBLANES:, :]
    u = _silu(u).astype(BF16)

    inner = d_model // MLSTM_HEADS
    qk = wmk_ref.shape[2]
    for hd in range(MLSTM_HEADS):
        uh = u[:, hd * inner:(hd + 1) * inner]
        mk_ref[:, hd * qk:(hd + 1) * qk] = jnp.dot(uh, wmk_ref[hd], preferred_element_type=F32).astype(BF16)
        qt = lax.dot_general(wmqt_ref[hd], uh, (((1,), (1,)), ((), ())), preferred_element_type=F32)
        mqt_ref[0, hd * qk:(hd + 1) * qk, :] = (qt * (qk ** -0.5)).astype(BF16)


def _slab_rows(rows, nsteps):
    steps = 1
    while steps <= nsteps:
        if nsteps % steps == 0 and (rows * steps) % nsteps == 0 and (rows * steps // nsteps) % BF16_ROWS == 0:
            return rows * steps // nsteps, steps
        steps *= 2
    return None


def _inproj(x2, mod3, wn, bn, wt, bt, conv_w, conv_b, w_mk, w_mqt, side, layer, *, bsz, seq, tm, chunk):
    tok, d = x2.shape
    nt = seq // tm
    nsteps = tok // tm
    qkw = w_mk.shape[0] * w_mk.shape[2]
    tmaj = lambda width: pl.BlockSpec((tm, width), lambda t: (t, 0))
    fmaj = lambda rows: pl.BlockSpec((1, rows, tm), lambda t: (t // nt, 0, t % nt))

    def slab_in(a):
        rows, steps = _slab_rows(a.shape[1], nsteps)
        return pl.BlockSpec((1, rows, a.shape[2]), lambda t: (layer, t // steps, 0))

    def slab_out(a):
        rows, steps = _slab_rows(a.shape[1], nsteps)
        return pl.BlockSpec((rows, a.shape[2]), lambda t: (t // steps, 0))

    out_shape = (
        jax.ShapeDtypeStruct((tok, FOX_WIDTH), BF16),
        jax.ShapeDtypeStruct((tok, qkw), BF16),
        jax.ShapeDtypeStruct((bsz, FOX_WIDTH, seq), BF16),
        jax.ShapeDtypeStruct((bsz, FOX_WIDTH, seq), BF16),
        jax.ShapeDtypeStruct((bsz, qkw, seq), BF16),
        jax.ShapeDtypeStruct((bsz, d, seq), BF16),
        jax.ShapeDtypeStruct((bsz, d, seq), BF16),
        jax.ShapeDtypeStruct((bsz, d, seq), BF16),
        jax.ShapeDtypeStruct((bsz, d, seq), BF16),
        jax.ShapeDtypeStruct((bsz, GATE_ROWS, seq), F32),
        jax.ShapeDtypeStruct((tok, GATE_ROWS), F32),
        jax.ShapeDtypeStruct((tok, FOX_WIDTH), BF16),
        jax.ShapeDtypeStruct((bsz, FOX_HEADS * FOX_AUG, seq), BF16),
    ) + tuple(jax.ShapeDtypeStruct(a.shape[1:], BF16) for a in side)
    out_specs = (tmaj(FOX_WIDTH), tmaj(qkw), fmaj(FOX_WIDTH), fmaj(FOX_WIDTH), fmaj(qkw),
                 fmaj(d), fmaj(d), fmaj(d), fmaj(d), fmaj(GATE_ROWS), tmaj(GATE_ROWS),
                 tmaj(FOX_WIDTH), fmaj(FOX_HEADS * FOX_AUG)) + tuple(slab_out(a) for a in side)
    kern = functools.partial(_inproj_kernel, tiles_per_seq=nt, chunk=chunk, d_model=d)
    return pl.pallas_call(
        kern,
        grid=(tok // tm,),
        in_specs=[
            pl.BlockSpec((tm, d), lambda t: (t, 0)),
            pl.BlockSpec((1,) + mod3.shape[1:], lambda t: (t // nt, 0, 0)),
            _const_spec(wn.shape), _const_spec(bn.shape), _const_spec(wt.shape), _const_spec(bt.shape),
            _const_spec(conv_w.shape), _const_spec(conv_b.shape), _const_spec(w_mk.shape), _const_spec(w_mqt.shape),
        ] + [slab_in(a) for a in side],
        out_specs=out_specs,
        out_shape=out_shape,
        scratch_shapes=[pltpu.VMEM((SUBLANES, d), F32), pltpu.VMEM((FOX_HEADS, 128), F32)],
        compiler_params=pltpu.CompilerParams(dimension_semantics=("arbitrary",), vmem_limit_bytes=VMEM_LIMIT_BYTES),
        name="inproj",
    )(x2, mod3, wn, bn, wt, bt, conv_w, conv_b, w_mk, w_mqt, *side)


def _fox_scores(qi, a, qt_ref, k_ref, kf_ref, qf_ref, blk):
    hd = FOX_HEAD_DIM
    half = blk // 2
    q0 = qi * blk
    main = q0 + half
    keys = q0 + blk
    qt2 = qt_ref[0, :, q0:keys]
    row = lax.broadcasted_iota(jnp.int32, qt2.shape, 0)
    tri_main = (lax.broadcasted_iota(jnp.int32, (half, blk), 0)
                <= lax.broadcasted_iota(jnp.int32, (half, blk), 1))
    kb = jnp.concatenate([k_ref[0, 0:keys, :], kf_ref[0, 0:keys, :]], axis=1)
    qm = jnp.where((row >= a * hd) & (row < (a + 1) * hd), qt2, jnp.zeros_like(qt2))
    qm = jnp.concatenate([qm, qf_ref[0, a * FOX_AUG:(a + 1) * FOX_AUG, q0:keys],
                          jnp.zeros((2 * hd - FOX_AUG, blk), BF16)], axis=0)
    s_main = jnp.dot(kb[:main], qm, preferred_element_type=F32)
    s_last = jnp.dot(kb[main:], qm[:, half:], preferred_element_type=F32)
    s_edge = jnp.where(tri_main, s_main[q0:], -jnp.inf)
    s_last = jnp.where(tri_main[:, :half], s_last, -jnp.inf)
    m = jnp.max(s_edge, axis=0, keepdims=True)
    if q0:
        m = jnp.maximum(m, jnp.max(s_main[:q0], axis=0, keepdims=True))
    m_hi = jnp.maximum(m[:, half:], jnp.max(s_last, axis=0, keepdims=True))
    m = jnp.concatenate([m[:, :half], m_hi], axis=1)
    return s_main[:q0] if q0 else None, s_edge, s_last, m, m_hi


def _fox_values(qi, a, scores, vt_ref, o_ref, blk):
    hd = FOX_HEAD_DIM
    half = blk // 2
    q0 = qi * blk
    main = q0 + half
    keys = q0 + blk
    s_top, s_edge, s_last, m, m_hi = scores
    ones = (lax.broadcasted_iota(jnp.int32, (VAUG_PAD, keys), 0) == 0).astype(BF16)
    vaug = jnp.concatenate([vt_ref[0, a * hd:(a + 1) * hd, 0:keys], ones], axis=0)
    acc = jnp.dot(vaug[:, q0:main], jnp.exp2(s_edge - m).astype(BF16), preferred_element_type=F32)
    if q0:
        acc = acc + jnp.dot(vaug[:, :q0], jnp.exp2(s_top - m).astype(BF16), preferred_element_type=F32)
    acc_hi = jnp.dot(vaug[:, main:], jnp.exp2(s_last - m_hi).astype(BF16), preferred_element_type=F32)
    acc = jnp.concatenate([acc[:, :half], acc[:, half:] + acc_hi], axis=1)
    o_ref[0, a * hd:(a + 1) * hd, q0:keys] = (acc[:hd] / acc[hd:hd + 1]).astype(BF16)


def _fox_kernel(qt_ref, k_ref, vt_ref, kf_ref, qf_ref, o_ref, *, blk):
    units = [(qi, a) for qi in reversed(range(k_ref.shape[1] // blk)) for a in range(2)]
    pending = []
    for qi, a in units:
        pending.append((qi, a, _fox_scores(qi, a, qt_ref, k_ref, kf_ref, qf_ref, blk)))
        if len(pending) > FOX_SCORE_LEAD:
            _fox_values(*pending.pop(0), vt_ref, o_ref, blk)
    for unit in pending:
        _fox_values(*unit, vt_ref, o_ref, blk)


def _fox_attention(fqt, fk3, fvt, kf3, qf, *, blk):
    bsz, width, seq = fqt.shape
    pairs = FOX_HEADS // 2
    pw = 2 * FOX_HEAD_DIM
    return pl.pallas_call(
        functools.partial(_fox_kernel, blk=blk),
        grid=(bsz, pairs),
        in_specs=[
            pl.BlockSpec((1, pw, seq), lambda b, p: (b, p, 0)),
            pl.BlockSpec((1, seq, pw), lambda b, p: (b, 0, p)),
            pl.BlockSpec((1, pw, seq), lambda b, p: (b, p, 0)),
            pl.BlockSpec((1, seq, pw), lambda b, p: (b, 0, p)),
            pl.BlockSpec((1, 2 * FOX_AUG, seq), lambda b, p: (b, p, 0)),
        ],
        out_specs=pl.BlockSpec((1, pw, seq), lambda b, p: (b, p, 0)),
        out_shape=jax.ShapeDtypeStruct((bsz, width, seq), BF16),
        compiler_params=pltpu.CompilerParams(dimension_semantics=("parallel", "parallel"),
                                             vmem_limit_bytes=VMEM_LIMIT_BYTES),
        name="fox_attention",
    )(fqt, fk3, fvt, kf3, qf)


def _mlstm_kernel(qt_ref, k_ref, vt_ref, ot_ref, rows_ref, cols_ref, g_ref, h_ref, state_ref, m_ref):
    c = pl.program_id(1)
    L = qt_ref.shape[2]
    qk = k_ref.shape[2] // MLSTM_HEADS
    dv = vt_ref.shape[1] // MLSTM_HEADS

    @pl.when(c == 0)
    def _():
        state_ref[...] = jnp.zeros_like(state_ref)
        m_ref[...] = jnp.zeros_like(m_ref)

    src = lax.broadcasted_iota(jnp.int32, (L, L), 0)
    dst = lax.broadcasted_iota(jnp.int32, (L, L), 1)
    causal = src <= dst
    ones_rows = (lax.broadcasted_iota(jnp.int32, (VAUG_PAD, L), 0) == 0).astype(BF16)

    for bi, hd in [(bi, hd) for bi in range(qt_ref.shape[0]) for hd in range(MLSTM_HEADS)]:
        qt = qt_ref[bi, hd * qk:(hd + 1) * qk, :]
        kk = k_ref[bi, :, hd * qk:(hd + 1) * qk]
        vaug = jnp.concatenate([vt_ref[bi, hd * dv:(hd + 1) * dv, :], ones_rows], axis=0)
        b = rows_ref[bi, FOX_HEADS + hd:FOX_HEADS + hd + 1, :]
        ig = rows_ref[bi, FOX_HEADS + MLSTM_HEADS + hd:FOX_HEADS + MLSTM_HEADS + hd + 1, :]
        ccol = cols_ref[bi, :, FOX_HEADS + hd:FOX_HEADS + hd + 1]
        m_prev = m_ref[bi, hd]
        state = state_ref[bi, hd]

        g = jnp.broadcast_to(b[:, L - 1:L], (1, L))
        a = g - b + ig
        dmat = jnp.where(causal, b + ccol, -jnp.inf)
        inter = b + m_prev
        m_t = jnp.maximum(inter, jnp.max(dmat, axis=0, keepdims=True))
        scores = jnp.dot(kk, qt, preferred_element_type=F32) * jnp.exp(dmat - m_t)
        w_inter = jnp.exp(inter - m_t)
        ht = (w_inter * jnp.dot(state.astype(BF16), qt, preferred_element_type=F32)
              + jnp.dot(vaug, scores.astype(BF16), preferred_element_type=F32))
        den = ht[dv:dv + 1, :]
        hh = ht[:dv, :] / jnp.maximum(jnp.abs(den), jnp.exp(-m_t))
        mu = jnp.mean(hh, axis=0, keepdims=True)
        dlt = hh - mu
        var = jnp.mean(dlt * dlt, axis=0, keepdims=True)
        hn = dlt * lax.rsqrt(var + LN_EPS) * g_ref[hd * dv:(hd + 1) * dv, :]
        gate = jax.nn.sigmoid(ot_ref[bi, hd * dv:(hd + 1) * dv, :].astype(F32))
        h_ref[bi, hd * dv:(hd + 1) * dv, :] = (hn * gate).astype(BF16)

        m_new = jnp.maximum(g + m_prev, jnp.max(a, axis=1, keepdims=True))
        decay = jnp.exp(g + m_prev - m_new)
        w = jnp.exp(a - m_new)
        upd = jnp.dot((vaug.astype(F32) * w).astype(BF16), kk, preferred_element_type=F32)
        state_ref[bi, hd] = decay[:, :qk] * state + upd
        m_ref[bi, hd] = m_new


def _mlstm(mqt, mk3, mvt, mot, rows, cols3, gcol, *, chunk, nb):
    bsz, qkw, seq = mqt.shape
    d = mvt.shape[1]
    dv = d // MLSTM_HEADS
    qk = qkw // MLSTM_HEADS
    fmaj = lambda r: pl.BlockSpec((nb, r, chunk), lambda b, c: (b, 0, c))
    return pl.pallas_call(
        _mlstm_kernel,
        grid=(bsz // nb, seq // chunk),
        in_specs=[
            fmaj(qkw),
            pl.BlockSpec((nb, chunk, qkw), lambda b, c: (b, c, 0)),
            fmaj(d), fmaj(d), fmaj(GATE_ROWS),
            pl.BlockSpec((nb, chunk, GATE_ROWS), lambda b, c: (b, c, 0)),
            _const_spec(gcol.shape),
        ],
        out_specs=fmaj(d),
        out_shape=jax.ShapeDtypeStruct((bsz, d, seq), BF16),
        scratch_shapes=[pltpu.VMEM((nb, MLSTM_HEADS, dv + VAUG_PAD, qk), F32),
                        pltpu.VMEM((nb, MLSTM_HEADS, 1, chunk), F32)],
        compiler_params=pltpu.CompilerParams(dimension_semantics=("parallel", "arbitrary"),
                                             vmem_limit_bytes=VMEM_LIMIT_BYTES),
        name="mlstm",
    )(mqt, mk3, mvt, mot, rows, cols3, gcol)


def _tail_kernel(x_ref, mod_ref, att_ref, hm_ref, ga_ref, gb_ref, wpa_ref, wpb_ref, wout_ref,
                 l1g_ref, l1b_ref, wfi_ref, wfd_ref, l2g_ref, l2b_ref, o_ref, *, d_ff, nsub):
    g1 = mod_ref[0, 2:3, :]
    sh2 = mod_ref[0, 3:4, :]
    sc2 = mod_ref[0, 4:5, :]
    g2 = mod_ref[0, 5:6, :]
    sub = x_ref.shape[0] // nsub
    sl = [slice(k * sub, (k + 1) * sub) for k in range(nsub)]
    at = [jnp.dot(wpa_ref[...], att_ref[0, :, s], preferred_element_type=F32) for s in sl]
    bt = [jnp.dot(wpb_ref[...], hm_ref[0, :, s], preferred_element_type=F32) for s in sl]
    yt = [(jax.nn.sigmoid(ga_ref[0, :, s].astype(F32)) * a
           + jax.nn.sigmoid(gb_ref[0, :, s].astype(F32)) * b).astype(BF16) for s, a, b in zip(sl, at, bt)]
    z = [lax.dot_general(y, wout_ref[...], (((0,), (0,)), ((), ())), preferred_element_type=F32) for y in yt]
    x1 = [_ln(ALPHA * x_ref[s, :] + g1 * zz) * l1g_ref[...] + l1b_ref[...] for s, zz in zip(sl, z)]
    h2 = [(_ln(v) * (1.0 + sc2) + sh2).astype(BF16) for v in x1]
    gu = [jnp.dot(v, wfi_ref[...], preferred_element_type=F32) for v in h2]
    act = [(_silu(v[:, :d_ff]) * v[:, d_ff:]).astype(BF16) for v in gu]
    ff = [jnp.dot(v, wfd_ref[...], preferred_element_type=F32) for v in act]
    for s, v, f in zip(sl, x1, ff):
        o_ref[s, :] = _ln(ALPHA * v + g2 * f) * l2g_ref[...] + l2b_ref[...]


def _tail(x2, mod3, att_t, hm_t, ga_t, gb_t, wpa_t, wpb_t, w_out, l1g, l1b, wfi, wfd, l2g, l2b, *, seq, tm):
    tok, d = x2.shape
    nt = seq // tm
    d_ff = wfd.shape[0]
    fmaj = lambda rows: pl.BlockSpec((1, rows, tm), lambda t: (t // nt, 0, t % nt))
    consts = (wpa_t, wpb_t, w_out, l1g, l1b, wfi, wfd, l2g, l2b)
    return pl.pallas_call(
        functools.partial(_tail_kernel, d_ff=d_ff, nsub=2),
        grid=(tok // tm,),
        in_specs=[
            pl.BlockSpec((tm, d), lambda t: (t, 0)),
            pl.BlockSpec((1,) + mod3.shape[1:], lambda t: (t // nt, 0, 0)),
            fmaj(att_t.shape[1]), fmaj(d), fmaj(d), fmaj(d),
        ] + [_const_spec(a.shape) for a in consts],
        out_specs=pl.BlockSpec((tm, d), lambda t: (t, 0)),
        out_shape=jax.ShapeDtypeStruct((tok, d), F32),
        compiler_params=pltpu.CompilerParams(dimension_semantics=("parallel",), vmem_limit_bytes=VMEM_LIMIT_BYTES),
        name="tail",
    )(x2, mod3, att_t, hm_t, ga_t, gb_t, *consts)


def _pick(n, pref):
    while n % pref:
        pref //= 2
    return pref


def kernel(x, c, w_ada, b_ada, w_in, b_in, conv_w, conv_b, w_mq, w_mk, mh_norm_g, w_pa, w_pb, w_out,
           ln1_g, ln1_b, w_ffn_in, w_ffn_down, ln2_g, ln2_b):
    bsz, seq, d = x.shape
    tok = bsz * seq
    chunk = _pick(seq, 256)
    fox_blk = _pick(seq, 512)
    tm_in = _pick(seq, 512)
    tm_tail = _pick(seq, 512)

    x2 = x.reshape(tok, d)
    for l in range(DEPTH):
        mod3 = _modulation(c, w_ada[l], b_ada[l]).reshape(bsz, 6, d)

        splits = (FOX_WIDTH, FOX_WIDTH, FOX_WIDTH, FOX_HEADS, d, d, MLSTM_HEADS, MLSTM_HEADS, d, d, d)
        offs = [0]
        for s in splits:
            offs.append(offs[-1] + s)
        col = lambda i: (w_in[l][:, offs[i]:offs[i + 1]], b_in[l][offs[i]:offs[i + 1]])
        (wfq, bfq), (wfk, bfk), (wfv, bfv), (wff, bff), (wmu, bmu), (wmv, bmv), (wmi, bmi), (wmf, bmf), \
            (wmo, bmo), (wga, bga), (wgb, bgb) = [col(i) for i in range(len(splits))]
        scale = FOX_HEAD_DIM ** -0.5 * LOG2E
        wn = jnp.concatenate([wfk, wmu], axis=1).astype(BF16)
        bn = jnp.concatenate([bfk, bmu])[None, :]
        wt = jnp.concatenate([wff, wmi, wmf, wfq * scale, wfv, wmv, wmo, wga, wgb], axis=1).T.astype(BF16)
        bt = jnp.concatenate([bff, bmi, bmf, bfq * scale, bfv, bmv, bmo, bga, bgb])[:, None]

        tail_w = (w_ffn_in, w_ffn_down, w_out)
        in_call = all(_slab_rows(a.shape[1], tok // tm_in) is not None for a in tail_w)
        fk, mk, fqt, fvt, mqt, mvt, mot, gat, gbt, rows, cols, kf, qf, *cast = _inproj(
            x2, mod3, wn, bn, wt, bt, conv_w[l], conv_b[l][None, :], w_mk[l].astype(BF16),
            jnp.swapaxes(w_mq[l], 1, 2).astype(BF16), tail_w if in_call else (), l,
            bsz=bsz, seq=seq, tm=tm_in, chunk=chunk)
        wfi, wfd, wo = cast if in_call else [a[l].astype(BF16) for a in tail_w]

        att_t = _fox_attention(fqt, fk.reshape(bsz, seq, FOX_WIDTH), fvt, kf.reshape(bsz, seq, FOX_WIDTH), qf,
                               blk=fox_blk)

        hm_t = _mlstm(mqt, mk.reshape(bsz, seq, -1), mvt, mot, rows, cols.reshape(bsz, seq, GATE_ROWS),
                      mh_norm_g[l][:, None], chunk=chunk, nb=2 if bsz % 2 == 0 else 1)

        x2 = _tail(x2, mod3, att_t, hm_t, gat, gbt, w_pa[l].T.astype(BF16), w_pb[l].T.astype(BF16),
                   wo, ln1_g[l][None, :], ln1_b[l][None, :], wfi, wfd, ln2_g[l][None, :], ln2_b[l][None, :],
                   seq=seq, tm=tm_tail)
    return x2.reshape(bsz, seq, d)
```

```python
import functools

import jax
import jax.numpy as jnp
from jax import lax
from jax.experimental import pallas as pl
from jax.experimental.pallas import tpu as pltpu

F32 = jnp.float32
BF16 = jnp.bfloat16

FOX_HEADS = 8
FOX_HEAD_DIM = 64
FOX_WIDTH = FOX_HEADS * FOX_HEAD_DIM
MLSTM_HEADS = 4
CONV_WIDTH = 4
LN_EPS = 1e-5
DEPTH = 1
ALPHA = (2.0 * DEPTH) ** 0.25
LOG2E = 1.4426950408889634

VMEM_LIMIT_BYTES = 56 * 1024 * 1024
SUBLANES = 8
BF16_ROWS = 16
GATE_ROWS = 16
VAUG_PAD = 16
FOX_AUG = 16
FOX_SCORE_LEAD = 3

def _const_spec(shape):
    nd = len(shape)
    return pl.BlockSpec(shape, lambda *_: (0,) * nd, pipeline_mode=pl.Buffered(1))


def _ln(v):
    mu = jnp.mean(v, axis=-1, keepdims=True)
    d = v - mu
    var = jnp.mean(d * d, axis=-1, keepdims=True)
    return d * lax.rsqrt(var + LN_EPS)


def _silu(v):
    return v * jax.nn.sigmoid(v)


def _mod_kernel(c_ref, w_ref, b_ref, o_ref):
    a = _silu(c_ref[...]).astype(BF16)
    o_ref[...] = jnp.dot(a, w_ref[...].astype(BF16), preferred_element_type=F32) + b_ref[...]


def _modulation(c, w_ada, b_ada):
    bsz, d = c.shape
    n = w_ada.shape[1]
    return pl.pallas_call(
        _mod_kernel,
        grid=(n // d,),
        in_specs=[
            pl.BlockSpec((bsz, d), lambda j: (0, 0)),
            pl.BlockSpec((d, d), lambda j: (0, j)),
            pl.BlockSpec((1, d), lambda j: (0, j)),
        ],
        out_specs=pl.BlockSpec((bsz, d), lambda j: (0, j)),
        out_shape=jax.ShapeDtypeStruct((bsz, n), F32),
        name="modulation",
    )(c, w_ada, b_ada.reshape(1, n))


def _prefix_sum_lanes(v, width):
    pos = lax.broadcasted_iota(jnp.int32, v.shape, 1) & (width - 1)
    d = 1
    while d < width:
        v = v + jnp.where(pos >= d, pltpu.roll(v, d, axis=1), 0.0)
        d *= 2
    return v


INPROJ_INPUTS = 10
INPROJ_OUTPUTS = 13


def _inproj_kernel(*refs, tiles_per_seq, chunk, d_model):
    (x_ref, mod_ref, wn_ref, bn_ref, wt_ref, bt_ref, cw_ref, cb_ref, wmk_ref, wmqt_ref) = refs[:INPROJ_INPUTS]
    n_side = (len(refs) - INPROJ_INPUTS - INPROJ_OUTPUTS - 2) // 2
    side_in = refs[INPROJ_INPUTS:INPROJ_INPUTS + n_side]
    outs = refs[INPROJ_INPUTS + n_side:]
    (fk_ref, mk_ref, fqt_ref, fvt_ref, mqt_ref, mvt_ref, mot_ref, gat_ref, gbt_ref,
     rows_ref, cols_ref, kf_ref, qf_ref) = outs[:INPROJ_OUTPUTS]
    side_out = outs[INPROJ_OUTPUTS:INPROJ_OUTPUTS + n_side]
    prev_ref, fcarry_ref = outs[INPROJ_OUTPUTS + n_side:]

    for src, dst in zip(side_in, side_out):
        dst[...] = src[0].astype(BF16)

    tm = x_ref.shape[0]
    first = (pl.program_id(0) % tiles_per_seq) == 0
    sh1 = mod_ref[0, 0:1, :]
    sc1 = mod_ref[0, 1:2, :]
    @pl.when(first)
    def _():
        prev_ref[...] = jnp.zeros_like(prev_ref)
        fcarry_ref[...] = jnp.zeros_like(fcarry_ref)

    halves = [slice(k * (tm // 2), (k + 1) * (tm // 2)) for k in range(2)]
    h_parts, yn_parts = [], []
    for rows in halves:
        h_parts.append((_ln(x_ref[rows, :]) * (1.0 + sc1) + sh1).astype(BF16))
        yn_parts.append(jnp.dot(h_parts[-1], wn_ref[...], preferred_element_type=F32) + bn_ref[...])
    h = jnp.concatenate(h_parts, axis=0)
    yn = jnp.concatenate(yn_parts, axis=0)

    def seg(r0, n):
        y = lax.dot_general(wt_ref[r0:r0 + n, :], h, (((1,), (1,)), ((), ())), preferred_element_type=F32)
        return y + bt_ref[r0:r0 + n, :]

    fk_ref[...] = yn[:, :FOX_WIDTH].astype(BF16)
    mu = yn[:, FOX_WIDTH:]

    gq = seg(0, GATE_ROWS + FOX_WIDTH)
    fqt_ref[0] = gq[GATE_ROWS:].astype(BF16)
    g = gq[:GATE_ROWS]
    ls = jax.nn.log_sigmoid(g)
    fcum = _prefix_sum_lanes(ls[0:FOX_HEADS] * LOG2E, tm) + fcarry_ref[:, 0:1]
    fcarry_ref[...] = jnp.broadcast_to(fcum[:, tm - 1:tm], fcarry_ref.shape)
    bcum = _prefix_sum_lanes(ls[FOX_HEADS:], chunk)[MLSTM_HEADS:]
    ig = g[FOX_HEADS:FOX_HEADS + MLSTM_HEADS]
    rows_ref[0] = jnp.concatenate([fcum, bcum, ig], axis=0)
    colsT = jnp.concatenate([fcum, ig - bcum, jnp.zeros_like(ig)], axis=0)
    cols_ref[...] = colsT.T

    f_hi = fcum.astype(BF16).astype(F32)
    f_mid = (fcum - f_hi).astype(BF16).astype(F32)
    f_lo = (fcum - f_hi - f_mid).astype(BF16).astype(F32)
    j = lax.broadcasted_iota(jnp.int32, (FOX_AUG, tm), 0)
    for hd in range(FOX_HEADS):
        odd = hd % 2
        ones_at = (j >= 3 + 3 * odd) & (j < 6 + 3 * odd)
        blk16 = jnp.where(j == 0, f_hi[hd:hd + 1], jnp.where(j == 1, f_mid[hd:hd + 1], jnp.where(
            j == 2, f_lo[hd:hd + 1], jnp.where(ones_at, 1.0, 0.0))))
        qf_ref[0, hd * FOX_AUG:(hd + 1) * FOX_AUG, :] = blk16.astype(BF16)
    kf_ref[...] = jnp.zeros_like(kf_ref)
    for pr in range(FOX_HEADS // 2):
        ea, eb = 2 * pr, 2 * pr + 1
        blk16 = jnp.where(j < 3, 1.0, 0.0)
        for entry, term in ((3, f_hi[ea:ea + 1]), (4, f_mid[ea:ea + 1]), (5, f_lo[ea:ea + 1]),
                            (6, f_hi[eb:eb + 1]), (7, f_mid[eb:eb + 1]), (8, f_lo[eb:eb + 1])):
            blk16 = jnp.where(j == entry, -term, blk16)
        kf_ref[:, pr * 2 * FOX_HEAD_DIM:pr * 2 * FOX_HEAD_DIM + FOX_AUG] = blk16.T.astype(BF16)

    r = GATE_ROWS + FOX_WIDTH
    for ref, n in ((fvt_ref, FOX_WIDTH), (mvt_ref, d_model), (mot_ref, d_model),
                   (gat_ref, d_model), (gbt_ref, d_model)):
        ref[0] = seg(r, n).astype(BF16)
        r += n

    u = cb_ref[...] + cw_ref[CONV_WIDTH - 1:CONV_WIDTH, :] * mu
    for k in range(1, CONV_WIDTH):
        u = u + cw_ref[CONV_WIDTH - 1 - k:CONV_WIDTH - k, :] * pltpu.roll(mu, k, axis=0)
    head = jnp.concatenate([prev_ref[...], mu[:SUBLANES]], axis=0)
    u_head = cb_ref[...] + cw_ref[0:1, :] * head[SUBLANES - 3:2 * SUBLANES - 3]
    for j in range(1, CONV_WIDTH):
        off = SUBLANES - (CONV_WIDTH - 1) + j
        u_head = u_head + cw_ref[j:j + 1, :] * head[off:off + SUBLANES]
    u = jnp.concatenate([u_head, u[SUBLANES:]], axis=0)
    prev_ref[...] = mu[tm - SUBLANES:, :]
    u = _silu(u).astype(BF16)

    inner = d_model // MLSTM_HEADS
    qk = wmk_ref.shape[2]
    for hd in range(MLSTM_HEADS):
        uh = u[:, hd * inner:(hd + 1) * inner]
        mk_ref[:, hd * qk:(hd + 1) * qk] = jnp.dot(uh, wmk_ref[hd], preferred_element_type=F32).astype(BF16)
        qt = lax.dot_general(wmqt_ref[hd], uh, (((1,), (1,)), ((), ())), preferred_element_type=F32)
        mqt_ref[0, hd * qk:(hd + 1) * qk, :] = (qt * (qk ** -0.5)).astype(BF16)


def _slab_rows(rows, nsteps):
    steps = 1
    while steps <= nsteps:
        if nsteps % steps == 0 and (rows * steps) % nsteps == 0 and (rows * steps // nsteps) % BF16_ROWS == 0:
            return rows * steps // nsteps, steps
        steps *= 2
    return None


def _inproj(x2, mod3, wn, bn, wt, bt, conv_w, conv_b, w_mk, w_mqt, side, layer, *, bsz, seq, tm, chunk):
    tok, d = x2.shape
    nt = seq // tm
    nsteps = tok // tm
    qkw = w_mk.shape[0] * w_mk.shape[2]
    tmaj = lambda width: pl.BlockSpec((tm, width), lambda t: (t, 0))
    fmaj = lambda rows: pl.BlockSpec((1, rows, tm), lambda t: (t // nt, 0, t % nt))

    def slab_in(a):
        rows, steps = _slab_rows(a.shape[1], nsteps)
        return pl.BlockSpec((1, rows, a.shape[2]), lambda t: (layer, t // steps, 0))

    def slab_out(a):
        rows, steps = _slab_rows(a.shape[1], nsteps)
        return pl.BlockSpec((rows, a.shape[2]), lambda t: (t // steps, 0))

    out_shape = (
        jax.ShapeDtypeStruct((tok, FOX_WIDTH), BF16),
        jax.ShapeDtypeStruct((tok, qkw), BF16),
        jax.ShapeDtypeStruct((bsz, FOX_WIDTH, seq), BF16),
        jax.ShapeDtypeStruct((bsz, FOX_WIDTH, seq), BF16),
        jax.ShapeDtypeStruct((bsz, qkw, seq), BF16),
        jax.ShapeDtypeStruct((bsz, d, seq), BF16),
        jax.ShapeDtypeStruct((bsz, d, seq), BF16),
        jax.ShapeDtypeStruct((bsz, d, seq), BF16),
        jax.ShapeDtypeStruct((bsz, d, seq), BF16),
        jax.ShapeDtypeStruct((bsz, GATE_ROWS, seq), F32),
        jax.ShapeDtypeStruct((tok, GATE_ROWS), F32),
        jax.ShapeDtypeStruct((tok, FOX_WIDTH), BF16),
        jax.ShapeDtypeStruct((bsz, FOX_HEADS * FOX_AUG, seq), BF16),
    ) + tuple(jax.ShapeDtypeStruct(a.shape[1:], BF16) for a in side)
    out_specs = (tmaj(FOX_WIDTH), tmaj(qkw), fmaj(FOX_WIDTH), fmaj(FOX_WIDTH), fmaj(qkw),
                 fmaj(d), fmaj(d), fmaj(d), fmaj(d), fmaj(GATE_ROWS), tmaj(GATE_ROWS),
                 tmaj(FOX_WIDTH), fmaj(FOX_HEADS * FOX_AUG)) + tuple(slab_out(a) for a in side)
    kern = functools.partial(_inproj_kernel, tiles_per_seq=nt, chunk=chunk, d_model=d)
    return pl.pallas_call(
        kern,
        grid=(tok // tm,),
        in_specs=[
            pl.BlockSpec((tm, d), lambda t: (t, 0)),
            pl.BlockSpec((1,) + mod3.shape[1:], lambda t: (t // nt, 0, 0)),
            _const_spec(wn.shape), _const_spec(bn.shape), _const_spec(wt.shape), _const_spec(bt.shape),
            _const_spec(conv_w.shape), _const_spec(conv_b.shape), _const_spec(w_mk.shape), _const_spec(w_mqt.shape),
        ] + [slab_in(a) for a in side],
        out_specs=out_specs,
        out_shape=out_shape,
        scratch_shapes=[pltpu.VMEM((SUBLANES, d), F32), pltpu.VMEM((FOX_HEADS, 128), F32)],
        compiler_params=pltpu.CompilerParams(dimension_semantics=("arbitrary",), vmem_limit_bytes=VMEM_LIMIT_BYTES),
        name="inproj",
    )(x2, mod3, wn, bn, wt, bt, conv_w, conv_b, w_mk, w_mqt, *side)


def _fox_scores(qi, a, qt_ref, k_ref, kf_ref, qf_ref, blk):
    hd = FOX_HEAD_DIM
    half = blk // 2
    q0 = qi * blk
    main = q0 + half
    keys = q0 + blk
    qt2 = qt_ref[0, :, q0:keys]
    row = lax.broadcasted_iota(jnp.int32, qt2.shape, 0)
    tri_main = (lax.broadcasted_iota(jnp.int32, (half, blk), 0)
                <= lax.broadcasted_iota(jnp.int32, (half, blk), 1))
    kb = jnp.concatenate([k_ref[0, 0:keys, :], kf_ref[0, 0:keys, :]], axis=1)
    qm = jnp.where((row >= a * hd) & (row < (a + 1) * hd), qt2, jnp.zeros_like(qt2))
    qm = jnp.concatenate([qm, qf_ref[0, a * FOX_AUG:(a + 1) * FOX_AUG, q0:keys],
                          jnp.zeros((2 * hd - FOX_AUG, blk), BF16)], axis=0)
    s_main = jnp.dot(kb[:main], qm, preferred_element_type=F32)
    s_last = jnp.dot(kb[main:], qm[:, half:], preferred_element_type=F32)
    s_edge = jnp.where(tri_main, s_main[q0:], -jnp.inf)
    s_last = jnp.where(tri_main[:, :half], s_last, -jnp.inf)
    m = jnp.max(s_edge, axis=0, keepdims=True)
    if q0:
        m = jnp.maximum(m, jnp.max(s_main[:q0], axis=0, keepdims=True))
    m_hi = jnp.maximum(m[:, half:], jnp.max(s_last, axis=0, keepdims=True))
    m = jnp.concatenate([m[:, :half], m_hi], axis=1)
    return s_main[:q0] if q0 else None, s_edge, s_last, m, m_hi


def _fox_values(qi, a, scores, vt_ref, o_ref, blk):
    hd = FOX_HEAD_DIM
    half = blk // 2
    q0 = qi * blk
    main = q0 + half
    keys = q0 + blk
    s_top, s_edge, s_last, m, m_hi = scores
    ones = (lax.broadcasted_iota(jnp.int32, (VAUG_PAD, keys), 0) == 0).astype(BF16)
    vaug = jnp.concatenate([vt_ref[0, a * hd:(a + 1) * hd, 0:keys], ones], axis=0)
    acc = jnp.dot(vaug[:, q0:main], jnp.exp2(s_edge - m).astype(BF16), preferred_element_type=F32)
    if q0:
        acc = acc + jnp.dot(vaug[:, :q0], jnp.exp2(s_top - m).astype(BF16), preferred_element_type=F32)
    acc_hi = jnp.dot(vaug[:, main:], jnp.exp2(s_last - m_hi).astype(BF16), preferred_element_type=F32)
    acc = jnp.concatenate([acc[:, :half], acc[:, half:] + acc_hi], axis=1)
    o_ref[0, a * hd:(a + 1) * hd, q0:keys] = (acc[:hd] / acc[hd:hd + 1]).astype(BF16)


def _fox_kernel(qt_ref, k_ref, vt_ref, kf_ref, qf_ref, o_ref, *, blk):
    units = [(qi, a) for qi in reversed(range(k_ref.shape[1] // blk)) for a in range(2)]
    pending = []
    for qi, a in units:
        pending.append((qi, a, _fox_scores(qi, a, qt_ref, k_ref, kf_ref, qf_ref, blk)))
        if len(pending) > FOX_SCORE_LEAD:
            _fox_values(*pending.pop(0), vt_ref, o_ref, blk)
    for unit in pending:
        _fox_values(*unit, vt_ref, o_ref, blk)


def _fox_attention(fqt, fk3, fvt, kf3, qf, *, blk):
    bsz, width, seq = fqt.shape
    pairs = FOX_HEADS // 2
    pw = 2 * FOX_HEAD_DIM
    return pl.pallas_call(
        functools.partial(_fox_kernel, blk=blk),
        grid=(bsz, pairs),
        in_specs=[
            pl.BlockSpec((1, pw, seq), lambda b, p: (b, p, 0)),
            pl.BlockSpec((1, seq, pw), lambda b, p: (b, 0, p)),
            pl.BlockSpec((1, pw, seq), lambda b, p: (b, p, 0)),
            pl.BlockSpec((1, seq, pw), lambda b, p: (b, 0, p)),
            pl.BlockSpec((1, 2 * FOX_AUG, seq), lambda b, p: (b, p, 0)),
        ],
        out_specs=pl.BlockSpec((1, pw, seq), lambda b, p: (b, p, 0)),
        out_shape=jax.ShapeDtypeStruct((bsz, width, seq), BF16),
        compiler_params=pltpu.CompilerParams(dimension_semantics=("parallel", "parallel"),
                                             vmem_limit_bytes=VMEM_LIMIT_BYTES),
        name="fox_attention",
    )(fqt, fk3, fvt, kf3, qf)


def _mlstm_kernel(qt_ref, k_ref, vt_ref, ot_ref, rows_ref, cols_ref, g_ref, h_ref, state_ref, m_ref):
    c = pl.program_id(1)
    L = qt_ref.shape[2]
    qk = k_ref.shape[2] // MLSTM_HEADS
    dv = vt_ref.shape[1] // MLSTM_HEADS

    @pl.when(c == 0)
    def _():
        state_ref[...] = jnp.zeros_like(state_ref)
        m_ref[...] = jnp.zeros_like(m_ref)

    src = lax.broadcasted_iota(jnp.int32, (L, L), 0)
    dst = lax.broadcasted_iota(jnp.int32, (L, L), 1)
    causal = src <= dst
    ones_rows = (lax.broadcasted_iota(jnp.int32, (VAUG_PAD, L), 0) == 0).astype(BF16)

    for bi, hd in [(bi, hd) for bi in range(qt_ref.shape[0]) for hd in range(MLSTM_HEADS)]:
        qt = qt_ref[bi, hd * qk:(hd + 1) * qk, :]
        kk = k_ref[bi, :, hd * qk:(hd + 1) * qk]
        vaug = jnp.concatenate([vt_ref[bi, hd * dv:(hd + 1) * dv, :], ones_rows], axis=0)
        b = rows_ref[bi, FOX_HEADS + hd:FOX_HEADS + hd + 1, :]
        ig = rows_ref[bi, FOX_HEADS + MLSTM_HEADS + hd:FOX_HEADS + MLSTM_HEADS + hd + 1, :]
        ccol = cols_ref[bi, :, FOX_HEADS + hd:FOX_HEADS + hd + 1]
        m_prev = m_ref[bi, hd]
        state = state_ref[bi, hd]

        g = jnp.broadcast_to(b[:, L - 1:L], (1, L))
        a = g - b + ig
        dmat = jnp.where(causal, b + ccol, -jnp.inf)
        inter = b + m_prev
        m_t = jnp.maximum(inter, jnp.max(dmat, axis=0, keepdims=True))
        scores = jnp.dot(kk, qt, preferred_element_type=F32) * jnp.exp(dmat - m_t)
        w_inter = jnp.exp(inter - m_t)
        ht = (w_inter * jnp.dot(state.astype(BF16), qt, preferred_element_type=F32)
              + jnp.dot(vaug, scores.astype(BF16), preferred_element_type=F32))
        den = ht[dv:dv + 1, :]
        hh = ht[:dv, :] / jnp.maximum(jnp.abs(den), jnp.exp(-m_t))
        mu = jnp.mean(hh, axis=0, keepdims=True)
        dlt = hh - mu
        var = jnp.mean(dlt * dlt, axis=0, keepdims=True)
        hn = dlt * lax.rsqrt(var + LN_EPS) * g_ref[hd * dv:(hd + 1) * dv, :]
        gate = jax.nn.sigmoid(ot_ref[bi, hd * dv:(hd + 1) * dv, :].astype(F32))
        h_ref[bi, hd * dv:(hd + 1) * dv, :] = (hn * gate).astype(BF16)

        m_new = jnp.maximum(g + m_prev, jnp.max(a, axis=1, keepdims=True))
        decay = jnp.exp(g + m_prev - m_new)
        w = jnp.exp(a - m_new)
        upd = jnp.dot((vaug.astype(F32) * w).astype(BF16), kk, preferred_element_type=F32)
        state_ref[bi, hd] = decay[:, :qk] * state + upd
        m_ref[bi, hd] = m_new


def _mlstm(mqt, mk3, mvt, mot, rows, cols3, gcol, *, chunk, nb):
    bsz, qkw, seq = mqt.shape
    d = mvt.shape[1]
    dv = d // MLSTM_HEADS
    qk = qkw // MLSTM_HEADS
    fmaj = lambda r: pl.BlockSpec((nb, r, chunk), lambda b, c: (b, 0, c))
    return pl.pallas_call(
        _mlstm_kernel,
        grid=(bsz // nb, seq // chunk),
        in_specs=[
            fmaj(qkw),
            pl.BlockSpec((nb, chunk, qkw), lambda b, c: (b, c, 0)),
            fmaj(d), fmaj(d), fmaj(GATE_ROWS),
            pl.BlockSpec((nb, chunk, GATE_ROWS), lambda b, c: (b, c, 0)),
            _const_spec(gcol.shape),
        ],
        out_specs=fmaj(d),
        out_shape=jax.ShapeDtypeStruct((bsz, d, seq), BF16),
        scratch_shapes=[pltpu.VMEM((nb, MLSTM_HEADS, dv + VAUG_PAD, qk), F32),
                        pltpu.VMEM((nb, MLSTM_HEADS, 1, chunk), F32)],
        compiler_params=pltpu.CompilerParams(dimension_semantics=("parallel", "arbitrary"),
                                             vmem_limit_bytes=VMEM_LIMIT_BYTES),
        name="mlstm",
    )(mqt, mk3, mvt, mot, rows, cols3, gcol)


def _tail_kernel(x_ref, mod_ref, att_ref, hm_ref, ga_ref, gb_ref, wpa_ref, wpb_ref, wout_ref,
                 l1g_ref, l1b_ref, wfi_ref, wfd_ref, l2g_ref, l2b_ref, o_ref, *, d_ff, nsub):
    g1 = mod_ref[0, 2:3, :]
    sh2 = mod_ref[0, 3:4, :]
    sc2 = mod_ref[0, 4:5, :]
    g2 = mod_ref[0, 5:6, :]
    sub = x_ref.shape[0] // nsub
    sl = [slice(k * sub, (k + 1) * sub) for k in range(nsub)]
    at = [jnp.dot(wpa_ref[...], att_ref[0, :, s], preferred_element_type=F32) for s in sl]
    bt = [jnp.dot(wpb_ref[...], hm_ref[0, :, s], preferred_element_type=F32) for s in sl]
    yt = [(jax.nn.sigmoid(ga_ref[0, :, s].astype(F32)) * a
           + jax.nn.sigmoid(gb_ref[0, :, s].astype(F32)) * b).astype(BF16) for s, a, b in zip(sl, at, bt)]
    z = [lax.dot_general(y, wout_ref[...], (((0,), (0,)), ((), ())), preferred_element_type=F32) for y in yt]
    x1 = [_ln(ALPHA * x_ref[s, :] + g1 * zz) * l1g_ref[...] + l1b_ref[...] for s, zz in zip(sl, z)]
    h2 = [(_ln(v) * (1.0 + sc2) + sh2).astype(BF16) for v in x1]
    gu = [jnp.dot(v, wfi_ref[...], preferred_element_type=F32) for v in h2]
    act = [(_silu(v[:, :d_ff]) * v[:, d_ff:]).astype(BF16) for v in gu]
    ff = [jnp.dot(v, wfd_ref[...], preferred_element_type=F32) for v in act]
    for s, v, f in zip(sl, x1, ff):
        o_ref[s, :] = _ln(ALPHA * v + g2 * f) * l2g_ref[...] + l2b_ref[...]


def _tail(x2, mod3, att_t, hm_t, ga_t, gb_t, wpa_t, wpb_t, w_out, l1g, l1b, wfi, wfd, l2g, l2b, *, seq, tm):
    tok, d = x2.shape
    nt = seq // tm
    d_ff = wfd.shape[0]
    fmaj = lambda rows: pl.BlockSpec((1, rows, tm), lambda t: (t // nt, 0, t % nt))
    consts = (wpa_t, wpb_t, w_out, l1g, l1b, wfi, wfd, l2g, l2b)
    return pl.pallas_call(
        functools.partial(_tail_kernel, d_ff=d_ff, nsub=2),
        grid=(tok // tm,),
        in_specs=[
            pl.BlockSpec((tm, d), lambda t: (t, 0)),
            pl.BlockSpec((1,) + mod3.shape[1:], lambda t: (t // nt, 0, 0)),
            fmaj(att_t.shape[1]), fmaj(d), fmaj(d), fmaj(d),
        ] + [_const_spec(a.shape) for a in consts],
        out_specs=pl.BlockSpec((tm, d), lambda t: (t, 0)),
        out_shape=jax.ShapeDtypeStruct((tok, d), F32),
        compiler_params=pltpu.CompilerParams(dimension_semantics=("parallel",), vmem_limit_bytes=VMEM_LIMIT_BYTES),
        name="tail",
    )(x2, mod3, att_t, hm_t, ga_t, gb_t, *consts)


def _pick(n, pref):
    while n % pref:
        pref //= 2
    return pref


def kernel(x, c, w_ada, b_ada, w_in, b_in, conv_w, conv_b, w_mq, w_mk, mh_norm_g, w_pa, w_pb, w_out,
           ln1_g, ln1_b, w_ffn_in, w_ffn_down, ln2_g, ln2_b):
    bsz, seq, d = x.shape
    tok = bsz * seq
    chunk = _pick(seq, 256)
    fox_blk = _pick(seq, 512)
    tm_in = _pick(seq, 512)
    tm_tail = _pick(seq, 512)

    x2 = x.reshape(tok, d)
    for l in range(DEPTH):
        mod3 = _modulation(c, w_ada[l], b_ada[l]).reshape(bsz, 6, d)

        splits = (FOX_WIDTH, FOX_WIDTH, FOX_WIDTH, FOX_HEADS, d, d, MLSTM_HEADS, MLSTM_HEADS, d, d, d)
        offs = [0]
        for s in splits:
            offs.append(offs[-1] + s)
        col = lambda i: (w_in[l][:, offs[i]:offs[i + 1]], b_in[l][offs[i]:offs[i + 1]])
        (wfq, bfq), (wfk, bfk), (wfv, bfv), (wff, bff), (wmu, bmu), (wmv, bmv), (wmi, bmi), (wmf, bmf), \
            (wmo, bmo), (wga, bga), (wgb, bgb) = [col(i) for i in range(len(splits))]
        scale = FOX_HEAD_DIM ** -0.5 * LOG2E
        wn = jnp.concatenate([wfk, wmu], axis=1).astype(BF16)
        bn = jnp.concatenate([bfk, bmu])[None, :]
        wt = jnp.concatenate([wff, wmi, wmf, wfq * scale, wfv, wmv, wmo, wga, wgb], axis=1).T.astype(BF16)
        bt = jnp.concatenate([bff, bmi, bmf, bfq * scale, bfv, bmv, bmo, bga, bgb])[:, None]

        tail_w = (w_ffn_in, w_ffn_down, w_out)
        in_call = all(_slab_rows(a.shape[1], tok // tm_in) is not None for a in tail_w)
        fk, mk, fqt, fvt, mqt, mvt, mot, gat, gbt, rows, cols, kf, qf, *cast = _inproj(
            x2, mod3, wn, bn, wt, bt, conv_w[l], conv_b[l][None, :], w_mk[l].astype(BF16),
            jnp.swapaxes(w_mq[l], 1, 2).astype(BF16), tail_w if in_call else (), l,
            bsz=bsz, seq=seq, tm=tm_in, chunk=chunk)
        wfi, wfd, wo = cast if in_call else [a[l].astype(BF16) for a in tail_w]

        att_t = _fox_attention(fqt, fk.reshape(bsz, seq, FOX_WIDTH), fvt, kf.reshape(bsz, seq, FOX_WIDTH), qf,
                               blk=fox_blk)

        hm_t = _mlstm(mqt, mk.reshape(bsz, seq, -1), mvt, mot, rows, cols.reshape(bsz, seq, GATE_ROWS),
                      mh_norm_g[l][:, None], chunk=chunk, nb=_pick(bsz, 4))

        x2 = _tail(x2, mod3, att_t, hm_t, gat, gbt, w_pa[l].T.astype(BF16), w_pb[l].T.astype(BF16),
                   wo, ln1_g[l][None, :], ln1_b[l][None, :], wfi, wfd, ln2_g[l][None, :], ln2_b[l][None, :],
                   seq=seq, tm=tm_tail)
    return x2.reshape(bsz, seq, d)
```

```python
import functools

import jax
import jax.numpy as jnp
from jax import lax
from jax.experimental import pallas as pl
from jax.experimental.pallas import tpu as pltpu

F32 = jnp.float32
BF16 = jnp.bfloat16

FOX_HEADS = 8
FOX_HEAD_DIM = 64
FOX_WIDTH = FOX_HEADS * FOX_HEAD_DIM
MLSTM_HEADS = 4
CONV_WIDTH = 4
LN_EPS = 1e-5
DEPTH = 1
ALPHA = (2.0 * DEPTH) ** 0.25
LOG2E = 1.4426950408889634

VMEM_LIMIT_BYTES = 56 * 1024 * 1024
SUBLANES = 8
BF16_ROWS = 16
GATE_ROWS = 16
VAUG_PAD = 16
FOX_AUG = 16
FOX_SCORE_LEAD = 3

def _const_spec(shape):
    nd = len(shape)
    return pl.BlockSpec(shape, lambda *_: (0,) * nd, pipeline_mode=pl.Buffered(1))


def _ln(v):
    mu = jnp.mean(v, axis=-1, keepdims=True)
    d = v - mu
    var = jnp.mean(d * d, axis=-1, keepdims=True)
    return d * lax.rsqrt(var + LN_EPS)


def _silu(v):
    return v * jax.nn.sigmoid(v)


def _mod_kernel(c_ref, w_ref, b_ref, o_ref):
    a = _silu(c_ref[...]).astype(BF16)
    o_ref[...] = jnp.dot(a, w_ref[...].astype(BF16), preferred_element_type=F32) + b_ref[...]


def _modulation(c, w_ada, b_ada):
    bsz, d = c.shape
    n = w_ada.shape[1]
    return pl.pallas_call(
        _mod_kernel,
        grid=(n // d,),
        in_specs=[
            pl.BlockSpec((bsz, d), lambda j: (0, 0)),
            pl.BlockSpec((d, d), lambda j: (0, j)),
            pl.BlockSpec((1, d), lambda j: (0, j)),
        ],
        out_specs=pl.BlockSpec((bsz, d), lambda j: (0, j)),
        out_shape=jax.ShapeDtypeStruct((bsz, n), F32),
        name="modulation",
    )(c, w_ada, b_ada.reshape(1, n))


def _prefix_scan_lanes(v, width, op, identity):
    pos = lax.broadcasted_iota(jnp.int32, v.shape, 1) & (width - 1)
    d = 1
    while d < width:
        v = op(v, jnp.where(pos >= d, pltpu.roll(v, d, axis=1), identity))
        d *= 2
    return v


INPROJ_INPUTS = 10
INPROJ_OUTPUTS = 13


def _inproj_kernel(*refs, tiles_per_seq, chunk, d_model):
    (x_ref, mod_ref, wn_ref, bn_ref, wt_ref, bt_ref, cw_ref, cb_ref, wmk_ref, wmqt_ref) = refs[:INPROJ_INPUTS]
    n_side = (len(refs) - INPROJ_INPUTS - INPROJ_OUTPUTS - 2) // 2
    side_in = refs[INPROJ_INPUTS:INPROJ_INPUTS + n_side]
    outs = refs[INPROJ_INPUTS + n_side:]
    (fk_ref, mk_ref, fqt_ref, fvt_ref, mqt_ref, mvt_ref, mot_ref, gat_ref, gbt_ref,
     rows_ref, cols_ref, kf_ref, qf_ref) = outs[:INPROJ_OUTPUTS]
    side_out = outs[INPROJ_OUTPUTS:INPROJ_OUTPUTS + n_side]
    prev_ref, fcarry_ref = outs[INPROJ_OUTPUTS + n_side:]

    for src, dst in zip(side_in, side_out):
        dst[...] = src[0].astype(BF16)

    tm = x_ref.shape[0]
    first = (pl.program_id(0) % tiles_per_seq) == 0
    sh1 = mod_ref[0, 0:1, :]
    sc1 = mod_ref[0, 1:2, :]
    @pl.when(first)
    def _():
        prev_ref[...] = jnp.zeros_like(prev_ref)
        fcarry_ref[...] = jnp.zeros_like(fcarry_ref)

    halves = [slice(k * (tm // 2), (k + 1) * (tm // 2)) for k in range(2)]
    h_parts, yn_parts = [], []
    for rows in halves:
        h_parts.append((_ln(x_ref[rows, :]) * (1.0 + sc1) + sh1).astype(BF16))
        yn_parts.append(jnp.dot(h_parts[-1], wn_ref[...], preferred_element_type=F32) + bn_ref[...])
    h = jnp.concatenate(h_parts, axis=0)
    yn = jnp.concatenate(yn_parts, axis=0)

    def seg(r0, n):
        y = lax.dot_general(wt_ref[r0:r0 + n, :], h, (((1,), (1,)), ((), ())), preferred_element_type=F32)
        return y + bt_ref[r0:r0 + n, :]

    fk_ref[...] = yn[:, :FOX_WIDTH].astype(BF16)
    mu = yn[:, FOX_WIDTH:]

    gq = seg(0, GATE_ROWS + FOX_WIDTH)
    fqt_ref[0] = gq[GATE_ROWS:].astype(BF16)
    g = gq[:GATE_ROWS]
    ls = jax.nn.log_sigmoid(g) * LOG2E
    fcum = _prefix_scan_lanes(ls[0:FOX_HEADS], tm, jnp.add, 0.0) + fcarry_ref[:, 0:1]
    fcarry_ref[...] = jnp.broadcast_to(fcum[:, tm - 1:tm], fcarry_ref.shape)
    bcum = _prefix_scan_lanes(ls[FOX_HEADS:], chunk, jnp.add, 0.0)[MLSTM_HEADS:]
    ig = g[FOX_HEADS:FOX_HEADS + MLSTM_HEADS] * LOG2E
    src = ig - bcum
    smax = _prefix_scan_lanes(jnp.concatenate([src, src], axis=0), chunk, jnp.maximum, -jnp.inf)[MLSTM_HEADS:]
    zeros4 = jnp.zeros_like(ig)
    rows_ref[0] = jnp.concatenate([smax, zeros4, bcum, ig], axis=0)
    cols_ref[...] = jnp.concatenate([zeros4, zeros4, src, zeros4], axis=0).T

    f_hi = fcum.astype(BF16).astype(F32)
    f_mid = (fcum - f_hi).astype(BF16).astype(F32)
    f_lo = (fcum - f_hi - f_mid).astype(BF16).astype(F32)
    j = lax.broadcasted_iota(jnp.int32, (FOX_AUG, tm), 0)
    for hd in range(FOX_HEADS):
        odd = hd % 2
        ones_at = (j >= 3 + 3 * odd) & (j < 6 + 3 * odd)
        blk16 = jnp.where(j == 0, f_hi[hd:hd + 1], jnp.where(j == 1, f_mid[hd:hd + 1], jnp.where(
            j == 2, f_lo[hd:hd + 1], jnp.where(ones_at, 1.0, 0.0))))
        qf_ref[0, hd * FOX_AUG:(hd + 1) * FOX_AUG, :] = blk16.astype(BF16)
    kf_ref[...] = jnp.zeros_like(kf_ref)
    for pr in range(FOX_HEADS // 2):
        ea, eb = 2 * pr, 2 * pr + 1
        blk16 = jnp.where(j < 3, 1.0, 0.0)
        for entry, term in ((3, f_hi[ea:ea + 1]), (4, f_mid[ea:ea + 1]), (5, f_lo[ea:ea + 1]),
                            (6, f_hi[eb:eb + 1]), (7, f_mid[eb:eb + 1]), (8, f_lo[eb:eb + 1])):
            blk16 = jnp.where(j == entry, -term, blk16)
        kf_ref[:, pr * 2 * FOX_HEAD_DIM:pr * 2 * FOX_HEAD_DIM + FOX_AUG] = blk16.T.astype(BF16)

    r = GATE_ROWS + FOX_WIDTH
    for ref, n in ((fvt_ref, FOX_WIDTH), (mvt_ref, d_model), (mot_ref, d_model),
                   (gat_ref, d_model), (gbt_ref, d_model)):
        ref[0] = seg(r, n).astype(BF16)
        r += n

    u = cb_ref[...] + cw_ref[CONV_WIDTH - 1:CONV_WIDTH, :] * mu
    for k in range(1, CONV_WIDTH):
        u = u + cw_ref[CONV_WIDTH - 1 - k:CONV_WIDTH - k, :] * pltpu.roll(mu, k, axis=0)
    head = jnp.concatenate([prev_ref[...], mu[:SUBLANES]], axis=0)
    u_head = cb_ref[...] + cw_ref[0:1, :] * head[SUBLANES - 3:2 * SUBLANES - 3]
    for j in range(1, CONV_WIDTH):
        off = SUBLANES - (CONV_WIDTH - 1) + j
        u_head = u_head + cw_ref[j:j + 1, :] * head[off:off + SUBLANES]
    u = jnp.concatenate([u_head, u[SUBLANES:]], axis=0)
    prev_ref[...] = mu[tm - SUBLANES:, :]
    u = _silu(u).astype(BF16)

    inner = d_model // MLSTM_HEADS
    qk = wmk_ref.shape[2]
    for hd in range(MLSTM_HEADS):
        uh = u[:, hd * inner:(hd + 1) * inner]
        mk_ref[:, hd * qk:(hd + 1) * qk] = jnp.dot(uh, wmk_ref[hd], preferred_element_type=F32).astype(BF16)
        qt = lax.dot_general(wmqt_ref[hd], uh, (((1,), (1,)), ((), ())), preferred_element_type=F32)
        mqt_ref[0, hd * qk:(hd + 1) * qk, :] = (qt * (qk ** -0.5)).astype(BF16)


def _slab_rows(rows, nsteps):
    steps = 1
    while steps <= nsteps:
        if nsteps % steps == 0 and (rows * steps) % nsteps == 0 and (rows * steps // nsteps) % BF16_ROWS == 0:
            return rows * steps // nsteps, steps
        steps *= 2
    return None


def _inproj(x2, mod3, wn, bn, wt, bt, conv_w, conv_b, w_mk, w_mqt, side, layer, *, bsz, seq, tm, chunk):
    tok, d = x2.shape
    nt = seq // tm
    nsteps = tok // tm
    qkw = w_mk.shape[0] * w_mk.shape[2]
    tmaj = lambda width: pl.BlockSpec((tm, width), lambda t: (t, 0))
    fmaj = lambda rows: pl.BlockSpec((1, rows, tm), lambda t: (t // nt, 0, t % nt))

    def slab_in(a):
        rows, steps = _slab_rows(a.shape[1], nsteps)
        return pl.BlockSpec((1, rows, a.shape[2]), lambda t: (layer, t // steps, 0))

    def slab_out(a):
        rows, steps = _slab_rows(a.shape[1], nsteps)
        return pl.BlockSpec((rows, a.shape[2]), lambda t: (t // steps, 0))

    out_shape = (
        jax.ShapeDtypeStruct((tok, FOX_WIDTH), BF16),
        jax.ShapeDtypeStruct((tok, qkw), BF16),
        jax.ShapeDtypeStruct((bsz, FOX_WIDTH, seq), BF16),
        jax.ShapeDtypeStruct((bsz, FOX_WIDTH, seq), BF16),
        jax.ShapeDtypeStruct((bsz, qkw, seq), BF16),
        jax.ShapeDtypeStruct((bsz, d, seq), BF16),
        jax.ShapeDtypeStruct((bsz, d, seq), BF16),
        jax.ShapeDtypeStruct((bsz, d, seq), BF16),
        jax.ShapeDtypeStruct((bsz, d, seq), BF16),
        jax.ShapeDtypeStruct((bsz, GATE_ROWS, seq), F32),
        jax.ShapeDtypeStruct((tok, GATE_ROWS), F32),
        jax.ShapeDtypeStruct((tok, FOX_WIDTH), BF16),
        jax.ShapeDtypeStruct((bsz, FOX_HEADS * FOX_AUG, seq), BF16),
    ) + tuple(jax.ShapeDtypeStruct(a.shape[1:], BF16) for a in side)
    out_specs = (tmaj(FOX_WIDTH), tmaj(qkw), fmaj(FOX_WIDTH), fmaj(FOX_WIDTH), fmaj(qkw),
                 fmaj(d), fmaj(d), fmaj(d), fmaj(d), fmaj(GATE_ROWS), tmaj(GATE_ROWS),
                 tmaj(FOX_WIDTH), fmaj(FOX_HEADS * FOX_AUG)) + tuple(slab_out(a) for a in side)
    kern = functools.partial(_inproj_kernel, tiles_per_seq=nt, chunk=chunk, d_model=d)
    return pl.pallas_call(
        kern,
        grid=(tok // tm,),
        in_specs=[
            pl.BlockSpec((tm, d), lambda t: (t, 0)),
            pl.BlockSpec((1,) + mod3.shape[1:], lambda t: (t // nt, 0, 0)),
            _const_spec(wn.shape), _const_spec(bn.shape), _const_spec(wt.shape), _const_spec(bt.shape),
            _const_spec(conv_w.shape), _const_spec(conv_b.shape), _const_spec(w_mk.shape), _const_spec(w_mqt.shape),
        ] + [slab_in(a) for a in side],
        out_specs=out_specs,
        out_shape=out_shape,
        scratch_shapes=[pltpu.VMEM((SUBLANES, d), F32), pltpu.VMEM((FOX_HEADS, 128), F32)],
        compiler_params=pltpu.CompilerParams(dimension_semantics=("arbitrary",), vmem_limit_bytes=VMEM_LIMIT_BYTES),
        name="inproj",
    )(x2, mod3, wn, bn, wt, bt, conv_w, conv_b, w_mk, w_mqt, *side)


def _fox_scores(pr, qi, a, qt_ref, k_ref, kf_ref, qf_ref, blk):
    hd = FOX_HEAD_DIM
    half = blk // 2
    q0 = qi * blk
    main = q0 + half
    keys = q0 + blk
    pw = 2 * hd
    qt2 = qt_ref[0, pr * pw:(pr + 1) * pw, q0:keys]
    row = lax.broadcasted_iota(jnp.int32, qt2.shape, 0)
    tri_main = (lax.broadcasted_iota(jnp.int32, (half, blk), 0)
                <= lax.broadcasted_iota(jnp.int32, (half, blk), 1))
    kb = jnp.concatenate([k_ref[0, 0:keys, pr * pw:(pr + 1) * pw],
                          kf_ref[0, 0:keys, pr * pw:(pr + 1) * pw]], axis=1)
    qm = jnp.where((row >= a * hd) & (row < (a + 1) * hd), qt2, jnp.zeros_like(qt2))
    qm = jnp.concatenate([qm, qf_ref[0, (2 * pr + a) * FOX_AUG:(2 * pr + a + 1) * FOX_AUG, q0:keys],
                          jnp.zeros((2 * hd - FOX_AUG, blk), BF16)], axis=0)
    s_main = jnp.dot(kb[:main], qm, preferred_element_type=F32)
    s_last = jnp.dot(kb[main:], qm[:, half:], preferred_element_type=F32)
    s_edge = jnp.where(tri_main, s_main[q0:], -jnp.inf)
    s_last = jnp.where(tri_main[:, :half], s_last, -jnp.inf)
    m = jnp.max(s_edge, axis=0, keepdims=True)
    if q0:
        m = jnp.maximum(m, jnp.max(s_main[:q0], axis=0, keepdims=True))
    m_hi = jnp.maximum(m[:, half:], jnp.max(s_last, axis=0, keepdims=True))
    m = jnp.concatenate([m[:, :half], m_hi], axis=1)
    return s_main[:q0] if q0 else None, s_edge, s_last, m, m_hi


def _fox_values(pr, qi, a, scores, vt_ref, o_ref, blk):
    hd = FOX_HEAD_DIM
    half = blk // 2
    q0 = qi * blk
    main = q0 + half
    keys = q0 + blk
    s_top, s_edge, s_last, m, m_hi = scores
    ones = (lax.broadcasted_iota(jnp.int32, (VAUG_PAD, keys), 0) == 0).astype(BF16)
    r0 = (2 * pr + a) * hd
    vaug = jnp.concatenate([vt_ref[0, r0:r0 + hd, 0:keys], ones], axis=0)
    s_all = jnp.concatenate([s_top, s_edge], axis=0) if q0 else s_edge
    acc = jnp.dot(vaug[:, :main], jnp.exp2(s_all - m).astype(BF16), preferred_element_type=F32)
    acc_hi = jnp.dot(vaug[:, main:], jnp.exp2(s_last - m_hi).astype(BF16), preferred_element_type=F32)
    acc = jnp.concatenate([acc[:, :half], acc[:, half:] + acc_hi], axis=1)
    o_ref[0, r0:r0 + hd, q0:keys] = (acc[:hd] / acc[hd:hd + 1]).astype(BF16)


def _fox_kernel(qt_ref, k_ref, vt_ref, kf_ref, qf_ref, o_ref, *, blk):
    npairs = qt_ref.shape[1] // (2 * FOX_HEAD_DIM)
    units = [(pr, qi, a) for qi in reversed(range(k_ref.shape[1] // blk)) for pr in range(npairs)
             for a in range(2)]
    pending = []
    for pr, qi, a in units:
        pending.append((pr, qi, a, _fox_scores(pr, qi, a, qt_ref, k_ref, kf_ref, qf_ref, blk)))
        if len(pending) > FOX_SCORE_LEAD:
            _fox_values(*pending.pop(0), vt_ref, o_ref, blk)
    for unit in pending:
        _fox_values(*unit, vt_ref, o_ref, blk)


def _fox_attention(fqt, fk3, fvt, kf3, qf, *, blk, pairs_per_step):
    bsz, width, seq = fqt.shape
    steps = FOX_HEADS // 2 // pairs_per_step
    pw = 2 * FOX_HEAD_DIM * pairs_per_step
    return pl.pallas_call(
        functools.partial(_fox_kernel, blk=blk),
        grid=(bsz, steps),
        in_specs=[
            pl.BlockSpec((1, pw, seq), lambda b, p: (b, p, 0)),
            pl.BlockSpec((1, seq, pw), lambda b, p: (b, 0, p)),
            pl.BlockSpec((1, pw, seq), lambda b, p: (b, p, 0)),
            pl.BlockSpec((1, seq, pw), lambda b, p: (b, 0, p)),
            pl.BlockSpec((1, 2 * FOX_AUG * pairs_per_step, seq), lambda b, p: (b, p, 0)),
        ],
        out_specs=pl.BlockSpec((1, pw, seq), lambda b, p: (b, p, 0)),
        out_shape=jax.ShapeDtypeStruct((bsz, width, seq), BF16),
        compiler_params=pltpu.CompilerParams(dimension_semantics=("parallel", "parallel"),
                                             vmem_limit_bytes=VMEM_LIMIT_BYTES),
        name="fox_attention",
    )(fqt, fk3, fvt, kf3, qf)


def _mlstm_kernel(qt_ref, k_ref, vt_ref, ot_ref, rows_ref, cols_ref, g_ref, h_ref, state_ref, m_ref):
    c = pl.program_id(1)
    L = qt_ref.shape[2]
    qk = k_ref.shape[2] // MLSTM_HEADS
    dv = vt_ref.shape[1] // MLSTM_HEADS

    @pl.when(c == 0)
    def _():
        state_ref[...] = jnp.zeros_like(state_ref)
        m_ref[...] = jnp.zeros_like(m_ref)

    src = lax.broadcasted_iota(jnp.int32, (L, L), 0)
    dst = lax.broadcasted_iota(jnp.int32, (L, L), 1)
    causal = src <= dst
    ones_rows = (lax.broadcasted_iota(jnp.int32, (VAUG_PAD, L), 0) == 0).astype(BF16)

    for bi, hd in [(bi, hd) for bi in range(qt_ref.shape[0]) for hd in range(MLSTM_HEADS)]:
        qt = qt_ref[bi, hd * qk:(hd + 1) * qk, :]
        kk = k_ref[bi, :, hd * qk:(hd + 1) * qk]
        vaug = jnp.concatenate([vt_ref[bi, hd * dv:(hd + 1) * dv, :], ones_rows], axis=0)
        smax = rows_ref[bi, hd:hd + 1, :]
        b = rows_ref[bi, FOX_HEADS + hd:FOX_HEADS + hd + 1, :]
        ig = rows_ref[bi, FOX_HEADS + MLSTM_HEADS + hd:FOX_HEADS + MLSTM_HEADS + hd + 1, :]
        scol = cols_ref[bi, :, FOX_HEADS + hd:FOX_HEADS + hd + 1]
        m_prev = m_ref[bi, hd]
        state = state_ref[bi, hd]

        g = jnp.broadcast_to(b[:, L - 1:L], (1, L))
        inter = b + m_prev
        m_t = jnp.maximum(inter, b + smax)
        pmat = jnp.exp2(jnp.where(causal, (b - m_t) + scol, -jnp.inf))
        scores = jnp.dot(kk, qt, preferred_element_type=F32) * pmat
        w_inter = jnp.exp2(inter - m_t)
        qtw = (qt.astype(F32) * w_inter).astype(BF16)
        ht = jnp.dot(jnp.concatenate([state.astype(BF16), vaug], axis=1),
                     jnp.concatenate([qtw, scores.astype(BF16)], axis=0),
                     preferred_element_type=F32)
        den = ht[dv:dv + 1, :]
        hh = ht[:dv, :] * (1.0 / jnp.maximum(jnp.abs(den), jnp.exp2(-m_t)))
        mu = jnp.mean(hh, axis=0, keepdims=True)
        dlt = hh - mu
        var = jnp.mean(dlt * dlt, axis=0, keepdims=True)
        hn = dlt * lax.rsqrt(var + LN_EPS) * g_ref[hd * dv:(hd + 1) * dv, :]
        th = jnp.tanh(ot_ref[bi, hd * dv:(hd + 1) * dv, :].astype(F32))
        h_ref[bi, hd * dv:(hd + 1) * dv, :] = (hn + hn * th).astype(BF16)

        m_new = jnp.maximum(g + m_prev, g + jnp.broadcast_to(smax[:, L - 1:L], (1, L)))
        decay = jnp.exp2(g + m_prev - m_new)
        w = jnp.exp2((g - m_new) + (ig - b))
        upd = jnp.dot((vaug.astype(F32) * w).astype(BF16), kk, preferred_element_type=F32)
        state_ref[bi, hd] = decay[:, :qk] * state + upd
        m_ref[bi, hd] = m_new


def _mlstm(mqt, mk3, mvt, mot, rows, cols3, gcol, *, chunk, nb):
    bsz, qkw, seq = mqt.shape
    d = mvt.shape[1]
    dv = d // MLSTM_HEADS
    qk = qkw // MLSTM_HEADS
    fmaj = lambda r: pl.BlockSpec((nb, r, chunk), lambda b, c: (b, 0, c))
    return pl.pallas_call(
        _mlstm_kernel,
        grid=(bsz // nb, seq // chunk),
        in_specs=[
            fmaj(qkw),
            pl.BlockSpec((nb, chunk, qkw), lambda b, c: (b, c, 0)),
            fmaj(d), fmaj(d), fmaj(GATE_ROWS),
            pl.BlockSpec((nb, chunk, GATE_ROWS), lambda b, c: (b, c, 0)),
            _const_spec(gcol.shape),
        ],
        out_specs=fmaj(d),
        out_shape=jax.ShapeDtypeStruct((bsz, d, seq), BF16),
        scratch_shapes=[pltpu.VMEM((nb, MLSTM_HEADS, dv + VAUG_PAD, qk), F32),
                        pltpu.VMEM((nb, MLSTM_HEADS, 1, chunk), F32)],
        compiler_params=pltpu.CompilerParams(dimension_semantics=("parallel", "arbitrary"),
                                             vmem_limit_bytes=VMEM_LIMIT_BYTES),
        name="mlstm",
    )(mqt, mk3, mvt, mot, rows, cols3, gcol)


def _tail_kernel(x_ref, mod_ref, att_ref, hm_ref, ga_ref, gb_ref, wpa_ref, wpb_ref, wout_ref,
                 l1g_ref, l1b_ref, wfi_ref, wfd_ref, l2g_ref, l2b_ref, o_ref, *, d_ff, nsub):
    g1 = mod_ref[0, 2:3, :]
    sh2 = mod_ref[0, 3:4, :]
    sc2 = mod_ref[0, 4:5, :]
    g2 = mod_ref[0, 5:6, :]
    sub = x_ref.shape[0] // nsub
    sl = [slice(k * sub, (k + 1) * sub) for k in range(nsub)]
    at = [jnp.dot(wpa_ref[...], att_ref[0, :, s], preferred_element_type=F32) for s in sl]
    bt = [jnp.dot(wpb_ref[...], hm_ref[0, :, s], preferred_element_type=F32) for s in sl]
    yt = [(jax.nn.sigmoid(ga_ref[0, :, s].astype(F32)) * a
           + jax.nn.sigmoid(gb_ref[0, :, s].astype(F32)) * b).astype(BF16) for s, a, b in zip(sl, at, bt)]
    z = [lax.dot_general(y, wout_ref[...], (((0,), (0,)), ((), ())), preferred_element_type=F32) for y in yt]
    x1 = [_ln(ALPHA * x_ref[s, :] + g1 * zz) * l1g_ref[...] + l1b_ref[...] for s, zz in zip(sl, z)]
    h2 = [(_ln(v) * (1.0 + sc2) + sh2).astype(BF16) for v in x1]
    gu = [jnp.dot(v, wfi_ref[...], preferred_element_type=F32) for v in h2]
    act = [(_silu(v[:, :d_ff]) * v[:, d_ff:]).astype(BF16) for v in gu]
    ff = [jnp.dot(v, wfd_ref[...], preferred_element_type=F32) for v in act]
    for s, v, f in zip(sl, x1, ff):
        o_ref[s, :] = _ln(ALPHA * v + g2 * f) * l2g_ref[...] + l2b_ref[...]


def _tail(x2, mod3, att_t, hm_t, ga_t, gb_t, wpa_t, wpb_t, w_out, l1g, l1b, wfi, wfd, l2g, l2b, *, seq, tm):
    tok, d = x2.shape
    nt = seq // tm
    d_ff = wfd.shape[0]
    fmaj = lambda rows: pl.BlockSpec((1, rows, tm), lambda t: (t // nt, 0, t % nt))
    consts = (wpa_t, wpb_t, w_out, l1g, l1b, wfi, wfd, l2g, l2b)
    return pl.pallas_call(
        functools.partial(_tail_kernel, d_ff=d_ff, nsub=2),
        grid=(tok // tm,),
        in_specs=[
            pl.BlockSpec((tm, d), lambda t: (t, 0)),
            pl.BlockSpec((1,) + mod3.shape[1:], lambda t: (t // nt, 0, 0)),
            fmaj(att_t.shape[1]), fmaj(d), fmaj(d), fmaj(d),
        ] + [_const_spec(a.shape) for a in consts],
        out_specs=pl.BlockSpec((tm, d), lambda t: (t, 0)),
        out_shape=jax.ShapeDtypeStruct((tok, d), F32),
        compiler_params=pltpu.CompilerParams(dimension_semantics=("parallel",), vmem_limit_bytes=VMEM_LIMIT_BYTES),
        name="tail",
    )(x2, mod3, att_t, hm_t, ga_t, gb_t, *consts)


def _pick(n, pref):
    while n % pref:
        pref //= 2
    return pref


def kernel(x, c, w_ada, b_ada, w_in, b_in, conv_w, conv_b, w_mq, w_mk, mh_norm_g, w_pa, w_pb, w_out,
           ln1_g, ln1_b, w_ffn_in, w_ffn_down, ln2_g, ln2_b):
    bsz, seq, d = x.shape
    tok = bsz * seq
    chunk = _pick(seq, 256)
    fox_blk = _pick(seq, 512)
    tm_in = _pick(seq, 512)
    tm_tail = _pick(seq, 512)

    x2 = x.reshape(tok, d)
    for l in range(DEPTH):
        mod3 = _modulation(c, w_ada[l], b_ada[l]).reshape(bsz, 6, d)

        splits = (FOX_WIDTH, FOX_WIDTH, FOX_WIDTH, FOX_HEADS, d, d, MLSTM_HEADS, MLSTM_HEADS, d, d, d)
        offs = [0]
        for s in splits:
            offs.append(offs[-1] + s)
        col = lambda i: (w_in[l][:, offs[i]:offs[i + 1]], b_in[l][offs[i]:offs[i + 1]])
        (wfq, bfq), (wfk, bfk), (wfv, bfv), (wff, bff), (wmu, bmu), (wmv, bmv), (wmi, bmi), (wmf, bmf), \
            (wmo, bmo), (wga, bga), (wgb, bgb) = [col(i) for i in range(len(splits))]
        scale = FOX_HEAD_DIM ** -0.5 * LOG2E
        wn = jnp.concatenate([wfk, wmu], axis=1).astype(BF16)
        bn = jnp.concatenate([bfk, bmu])[None, :]
        wt = jnp.concatenate([wff, wmi, wmf, wfq * scale, wfv, wmv, wmo * 0.5, wga, wgb], axis=1).T.astype(BF16)
        bt = jnp.concatenate([bff, bmi, bmf, bfq * scale, bfv, bmv, bmo * 0.5, bga, bgb])[:, None]

        tail_w = (w_ffn_in, w_ffn_down, w_out)
        in_call = all(_slab_rows(a.shape[1], tok // tm_in) is not None for a in tail_w)
        fk, mk, fqt, fvt, mqt, mvt, mot, gat, gbt, rows, cols, kf, qf, *cast = _inproj(
            x2, mod3, wn, bn, wt, bt, conv_w[l], conv_b[l][None, :], w_mk[l].astype(BF16),
            jnp.swapaxes(w_mq[l], 1, 2).astype(BF16), tail_w if in_call else (), l,
            bsz=bsz, seq=seq, tm=tm_in, chunk=chunk)
        wfi, wfd, wo = cast if in_call else [a[l].astype(BF16) for a in tail_w]

        att_t = _fox_attention(fqt, fk.reshape(bsz, seq, FOX_WIDTH), fvt, kf.reshape(bsz, seq, FOX_WIDTH), qf,
                               blk=fox_blk, pairs_per_step=2)

        hm_t = _mlstm(mqt, mk.reshape(bsz, seq, -1), mvt, mot, rows, cols.reshape(bsz, seq, GATE_ROWS),
                      0.5 * mh_norm_g[l][:, None], chunk=chunk, nb=_pick(bsz, 4))

        x2 = _tail(x2, mod3, att_t, hm_t, gat, gbt, w_pa[l].T.astype(BF16), w_pb[l].T.astype(BF16),
                   wo, ln1_g[l][None, :], ln1_b[l][None, :], wfi, wfd, ln2_g[l][None, :], ln2_b[l][None, :],
                   seq=seq, tm=tm_tail)
    return x2.reshape(bsz, seq, d)
```

```python
import functools

import jax
import jax.numpy as jnp
from jax import lax
from jax.experimental import pallas as pl
from jax.experimental.pallas import tpu as pltpu

F32 = jnp.float32
BF16 = jnp.bfloat16

FOX_HEADS = 8
FOX_HEAD_DIM = 64
FOX_WIDTH = FOX_HEADS * FOX_HEAD_DIM
MLSTM_HEADS = 4
CONV_WIDTH = 4
LN_EPS = 1e-5
DEPTH = 1
ALPHA = (2.0 * DEPTH) ** 0.25
LOG2E = 1.4426950408889634

VMEM_LIMIT_BYTES = 56 * 1024 * 1024
SUBLANES = 8
BF16_ROWS = 16
GATE_ROWS = 16
VAUG_PAD = 16
FOX_AUG = 16
FOX_SCORE_LEAD = 3

def _const_spec(shape):
    nd = len(shape)
    return pl.BlockSpec(shape, lambda *_: (0,) * nd, pipeline_mode=pl.Buffered(1))


def _ln(v):
    mu = jnp.mean(v, axis=-1, keepdims=True)
    d = v - mu
    var = jnp.mean(d * d, axis=-1, keepdims=True)
    return d * lax.rsqrt(var + LN_EPS)


def _silu(v):
    return v * jax.nn.sigmoid(v)


def _mod_kernel(c_ref, w_ref, b_ref, o_ref):
    a = _silu(c_ref[...]).astype(BF16)
    o_ref[...] = jnp.dot(a, w_ref[...].astype(BF16), preferred_element_type=F32) + b_ref[...]


def _modulation(c, w_ada, b_ada):
    bsz, d = c.shape
    n = w_ada.shape[1]
    return pl.pallas_call(
        _mod_kernel,
        grid=(n // d,),
        in_specs=[
            pl.BlockSpec((bsz, d), lambda j: (0, 0)),
            pl.BlockSpec((d, d), lambda j: (0, j)),
            pl.BlockSpec((1, d), lambda j: (0, j)),
        ],
        out_specs=pl.BlockSpec((bsz, d), lambda j: (0, j)),
        out_shape=jax.ShapeDtypeStruct((bsz, n), F32),
        name="modulation",
    )(c, w_ada, b_ada.reshape(1, n))


def _prefix_scan_lanes(v, width, op, identity):
    pos = lax.broadcasted_iota(jnp.int32, v.shape, 1) & (width - 1)
    d = 1
    while d < width:
        v = op(v, jnp.where(pos >= d, pltpu.roll(v, d, axis=1), identity))
        d *= 2
    return v


INPROJ_INPUTS = 10
INPROJ_OUTPUTS = 13


def _inproj_kernel(*refs, tiles_per_seq, chunk, d_model):
    (x_ref, mod_ref, wn_ref, bn_ref, wt_ref, bt_ref, cw_ref, cb_ref, wmk_ref, wmqt_ref) = refs[:INPROJ_INPUTS]
    n_side = (len(refs) - INPROJ_INPUTS - INPROJ_OUTPUTS - 2) // 2
    side_in = refs[INPROJ_INPUTS:INPROJ_INPUTS + n_side]
    outs = refs[INPROJ_INPUTS + n_side:]
    (fk_ref, mk_ref, fqt_ref, fvt_ref, mqt_ref, mvt_ref, mot_ref, gat_ref, gbt_ref,
     rows_ref, cols_ref, kf_ref, qf_ref) = outs[:INPROJ_OUTPUTS]
    side_out = outs[INPROJ_OUTPUTS:INPROJ_OUTPUTS + n_side]
    prev_ref, fcarry_ref = outs[INPROJ_OUTPUTS + n_side:]

    for src, dst in zip(side_in, side_out):
        dst[...] = src[0].astype(BF16)

    tm = x_ref.shape[0]
    first = (pl.program_id(0) % tiles_per_seq) == 0
    sh1 = mod_ref[0, 0:1, :]
    sc1 = mod_ref[0, 1:2, :]
    @pl.when(first)
    def _():
        prev_ref[...] = jnp.zeros_like(prev_ref)
        fcarry_ref[...] = jnp.zeros_like(fcarry_ref)

    halves = [slice(k * (tm // 2), (k + 1) * (tm // 2)) for k in range(2)]
    h_parts, yn_parts = [], []
    for rows in halves:
        h_parts.append((_ln(x_ref[rows, :]) * (1.0 + sc1) + sh1).astype(BF16))
        yn_parts.append(jnp.dot(h_parts[-1], wn_ref[...], preferred_element_type=F32) + bn_ref[...])
    h = jnp.concatenate(h_parts, axis=0)
    yn = jnp.concatenate(yn_parts, axis=0)

    def seg(r0, n):
        y = lax.dot_general(wt_ref[r0:r0 + n, :], h, (((1,), (1,)), ((), ())), preferred_element_type=F32)
        return y + bt_ref[r0:r0 + n, :]

    fk_ref[...] = yn[:, :FOX_WIDTH].astype(BF16)
    mu = yn[:, FOX_WIDTH:]

    gq = seg(0, GATE_ROWS + FOX_WIDTH)
    fqt_ref[0] = gq[GATE_ROWS:].astype(BF16)
    g = gq[:GATE_ROWS]
    ls = jax.nn.log_sigmoid(g) * LOG2E
    fcum = _prefix_scan_lanes(ls[0:FOX_HEADS], tm, jnp.add, 0.0) + fcarry_ref[:, 0:1]
    fcarry_ref[...] = jnp.broadcast_to(fcum[:, tm - 1:tm], fcarry_ref.shape)
    bcum = _prefix_scan_lanes(ls[FOX_HEADS:], chunk, jnp.add, 0.0)[MLSTM_HEADS:]
    ig = g[FOX_HEADS:FOX_HEADS + MLSTM_HEADS] * LOG2E
    src = ig - bcum
    smax = _prefix_scan_lanes(jnp.concatenate([src, src], axis=0), chunk, jnp.maximum, -jnp.inf)[MLSTM_HEADS:]
    zeros4 = jnp.zeros_like(ig)
    rows_ref[0] = jnp.concatenate([smax, zeros4, bcum, ig], axis=0)
    cols_ref[...] = jnp.concatenate([zeros4, zeros4, src, zeros4], axis=0).T

    f_hi = fcum.astype(BF16).astype(F32)
    f_mid = (fcum - f_hi).astype(BF16).astype(F32)
    f_lo = (fcum - f_hi - f_mid).astype(BF16).astype(F32)
    j = lax.broadcasted_iota(jnp.int32, (FOX_AUG, tm), 0)
    for hd in range(FOX_HEADS):
        odd = hd % 2
        ones_at = (j >= 3 + 3 * odd) & (j < 6 + 3 * odd)
        blk16 = jnp.where(j == 0, f_hi[hd:hd + 1], jnp.where(j == 1, f_mid[hd:hd + 1], jnp.where(
            j == 2, f_lo[hd:hd + 1], jnp.where(ones_at, 1.0, 0.0))))
        qf_ref[0, hd * FOX_AUG:(hd + 1) * FOX_AUG, :] = blk16.astype(BF16)
    kf_ref[...] = jnp.zeros_like(kf_ref)
    for pr in range(FOX_HEADS // 2):
        ea, eb = 2 * pr, 2 * pr + 1
        blk16 = jnp.where(j < 3, 1.0, 0.0)
        for entry, term in ((3, f_hi[ea:ea + 1]), (4, f_mid[ea:ea + 1]), (5, f_lo[ea:ea + 1]),
                            (6, f_hi[eb:eb + 1]), (7, f_mid[eb:eb + 1]), (8, f_lo[eb:eb + 1])):
            blk16 = jnp.where(j == entry, -term, blk16)
        kf_ref[:, pr * 2 * FOX_HEAD_DIM:pr * 2 * FOX_HEAD_DIM + FOX_AUG] = blk16.T.astype(BF16)

    r = GATE_ROWS + FOX_WIDTH
    for ref, n in ((fvt_ref, FOX_WIDTH), (mvt_ref, d_model), (mot_ref, d_model),
                   (gat_ref, d_model), (gbt_ref, d_model)):
        ref[0] = seg(r, n).astype(BF16)
        r += n

    u = cb_ref[...] + cw_ref[CONV_WIDTH - 1:CONV_WIDTH, :] * mu
    for k in range(1, CONV_WIDTH):
        u = u + cw_ref[CONV_WIDTH - 1 - k:CONV_WIDTH - k, :] * pltpu.roll(mu, k, axis=0)
    head = jnp.concatenate([prev_ref[...], mu[:SUBLANES]], axis=0)
    u_head = cb_ref[...] + cw_ref[0:1, :] * head[SUBLANES - 3:2 * SUBLANES - 3]
    for j in range(1, CONV_WIDTH):
        off = SUBLANES - (CONV_WIDTH - 1) + j
        u_head = u_head + cw_ref[j:j + 1, :] * head[off:off + SUBLANES]
    u = jnp.concatenate([u_head, u[SUBLANES:]], axis=0)
    prev_ref[...] = mu[tm - SUBLANES:, :]
    u = _silu(u).astype(BF16)

    inner = d_model // MLSTM_HEADS
    qk = wmk_ref.shape[2]
    for hd in range(MLSTM_HEADS):
        uh = u[:, hd * inner:(hd + 1) * inner]
        mk_ref[:, hd * qk:(hd + 1) * qk] = jnp.dot(uh, wmk_ref[hd], preferred_element_type=F32).astype(BF16)
        qt = lax.dot_general(wmqt_ref[hd], uh, (((1,), (1,)), ((), ())), preferred_element_type=F32)
        mqt_ref[0, hd * qk:(hd + 1) * qk, :] = (qt * (qk ** -0.5)).astype(BF16)


def _slab_rows(rows, nsteps):
    steps = 1
    while steps <= nsteps:
        if nsteps % steps == 0 and (rows * steps) % nsteps == 0 and (rows * steps // nsteps) % BF16_ROWS == 0:
            return rows * steps // nsteps, steps
        steps *= 2
    return None


def _inproj(x2, mod3, wn, bn, wt, bt, conv_w, conv_b, w_mk, w_mqt, side, layer, *, bsz, seq, tm, chunk):
    tok, d = x2.shape
    nt = seq // tm
    nsteps = tok // tm
    qkw = w_mk.shape[0] * w_mk.shape[2]
    tmaj = lambda width: pl.BlockSpec((tm, width), lambda t: (t, 0))
    fmaj = lambda rows: pl.BlockSpec((1, rows, tm), lambda t: (t // nt, 0, t % nt))

    def slab_in(a):
        rows, steps = _slab_rows(a.shape[1], nsteps)
        return pl.BlockSpec((1, rows, a.shape[2]), lambda t: (layer, t // steps, 0))

    def slab_out(a):
        rows, steps = _slab_rows(a.shape[1], nsteps)
        return pl.BlockSpec((rows, a.shape[2]), lambda t: (t // steps, 0))

    out_shape = (
        jax.ShapeDtypeStruct((tok, FOX_WIDTH), BF16),
        jax.ShapeDtypeStruct((tok, qkw), BF16),
        jax.ShapeDtypeStruct((bsz, FOX_WIDTH, seq), BF16),
        jax.ShapeDtypeStruct((bsz, FOX_WIDTH, seq), BF16),
        jax.ShapeDtypeStruct((bsz, qkw, seq), BF16),
        jax.ShapeDtypeStruct((bsz, d, seq), BF16),
        jax.ShapeDtypeStruct((bsz, d, seq), BF16),
        jax.ShapeDtypeStruct((bsz, d, seq), BF16),
        jax.ShapeDtypeStruct((bsz, d, seq), BF16),
        jax.ShapeDtypeStruct((bsz, GATE_ROWS, seq), F32),
        jax.ShapeDtypeStruct((tok, GATE_ROWS), F32),
        jax.ShapeDtypeStruct((tok, FOX_WIDTH), BF16),
        jax.ShapeDtypeStruct((bsz, FOX_HEADS * FOX_AUG, seq), BF16),
    ) + tuple(jax.ShapeDtypeStruct(a.shape[1:], BF16) for a in side)
    out_specs = (tmaj(FOX_WIDTH), tmaj(qkw), fmaj(FOX_WIDTH), fmaj(FOX_WIDTH), fmaj(qkw),
                 fmaj(d), fmaj(d), fmaj(d), fmaj(d), fmaj(GATE_ROWS), tmaj(GATE_ROWS),
                 tmaj(FOX_WIDTH), fmaj(FOX_HEADS * FOX_AUG)) + tuple(slab_out(a) for a in side)
    kern = functools.partial(_inproj_kernel, tiles_per_seq=nt, chunk=chunk, d_model=d)
    return pl.pallas_call(
        kern,
        grid=(tok // tm,),
        in_specs=[
            pl.BlockSpec((tm, d), lambda t: (t, 0)),
            pl.BlockSpec((1,) + mod3.shape[1:], lambda t: (t // nt, 0, 0)),
            _const_spec(wn.shape), _const_spec(bn.shape), _const_spec(wt.shape), _const_spec(bt.shape),
            _const_spec(conv_w.shape), _const_spec(conv_b.shape), _const_spec(w_mk.shape), _const_spec(w_mqt.shape),
        ] + [slab_in(a) for a in side],
        out_specs=out_specs,
        out_shape=out_shape,
        scratch_shapes=[pltpu.VMEM((SUBLANES, d), F32), pltpu.VMEM((FOX_HEADS, 128), F32)],
        compiler_params=pltpu.CompilerParams(dimension_semantics=("arbitrary",), vmem_limit_bytes=VMEM_LIMIT_BYTES),
        name="inproj",
    )(x2, mod3, wn, bn, wt, bt, conv_w, conv_b, w_mk, w_mqt, *side)


def _fox_scores(qi, a, qt_ref, k_ref, kf_ref, qf_ref, blk):
    hd = FOX_HEAD_DIM
    half = blk // 2
    q0 = qi * blk
    main = q0 + half
    keys = q0 + blk
    qt2 = qt_ref[0, :, q0:keys]
    row = lax.broadcasted_iota(jnp.int32, qt2.shape, 0)
    tri_main = (lax.broadcasted_iota(jnp.int32, (half, blk), 0)
                <= lax.broadcasted_iota(jnp.int32, (half, blk), 1))
    kb = jnp.concatenate([k_ref[0, 0:keys, :], kf_ref[0, 0:keys, :]], axis=1)
    qm = jnp.where((row >= a * hd) & (row < (a + 1) * hd), qt2, jnp.zeros_like(qt2))
    qm = jnp.concatenate([qm, qf_ref[0, a * FOX_AUG:(a + 1) * FOX_AUG, q0:keys],
                          jnp.zeros((2 * hd - FOX_AUG, blk), BF16)], axis=0)
    s_main = jnp.dot(kb[:main], qm, preferred_element_type=F32)
    s_last = jnp.dot(kb[main:], qm[:, half:], preferred_element_type=F32)
    s_edge = jnp.where(tri_main, s_main[q0:], -jnp.inf)
    s_last = jnp.where(tri_main[:, :half], s_last, -jnp.inf)
    m = jnp.max(s_edge, axis=0, keepdims=True)
    if q0:
        m = jnp.maximum(m, jnp.max(s_main[:q0], axis=0, keepdims=True))
    m_hi = jnp.maximum(m[:, half:], jnp.max(s_last, axis=0, keepdims=True))
    m = jnp.concatenate([m[:, :half], m_hi], axis=1)
    return s_main[:q0] if q0 else None, s_edge, s_last, m, m_hi


def _fox_values(qi, a, scores, vt_ref, o_ref, blk):
    hd = FOX_HEAD_DIM
    half = blk // 2
    q0 = qi * blk
    main = q0 + half
    keys = q0 + blk
    s_top, s_edge, s_last, m, m_hi = scores
    ones = (lax.broadcasted_iota(jnp.int32, (VAUG_PAD, keys), 0) == 0).astype(BF16)
    vaug = jnp.concatenate([vt_ref[0, a * hd:(a + 1) * hd, 0:keys], ones], axis=0)
    acc = jnp.dot(vaug[:, q0:main], jnp.exp2(s_edge - m).astype(BF16), preferred_element_type=F32)
    if q0:
        acc = acc + jnp.dot(vaug[:, :q0], jnp.exp2(s_top - m).astype(BF16), preferred_element_type=F32)
    acc_hi = jnp.dot(vaug[:, main:], jnp.exp2(s_last - m_hi).astype(BF16), preferred_element_type=F32)
    acc = jnp.concatenate([acc[:, :half], acc[:, half:] + acc_hi], axis=1)
    o_ref[0, a * hd:(a + 1) * hd, q0:keys] = (acc[:hd] / acc[hd:hd + 1]).astype(BF16)


def _fox_kernel(qt_ref, k_ref, vt_ref, kf_ref, qf_ref, o_ref, *, blk):
    units = [(qi, a) for qi in reversed(range(k_ref.shape[1] // blk)) for a in range(2)]
    pending = []
    for qi, a in units:
        pending.append((qi, a, _fox_scores(qi, a, qt_ref, k_ref, kf_ref, qf_ref, blk)))
        if len(pending) > FOX_SCORE_LEAD:
            _fox_values(*pending.pop(0), vt_ref, o_ref, blk)
    for unit in pending:
        _fox_values(*unit, vt_ref, o_ref, blk)


def _fox_attention(fqt, fk3, fvt, kf3, qf, *, blk):
    bsz, width, seq = fqt.shape
    pairs = FOX_HEADS // 2
    pw = 2 * FOX_HEAD_DIM
    return pl.pallas_call(
        functools.partial(_fox_kernel, blk=blk),
        grid=(bsz, pairs),
        in_specs=[
            pl.BlockSpec((1, pw, seq), lambda b, p: (b, p, 0)),
            pl.BlockSpec((1, seq, pw), lambda b, p: (b, 0, p)),
            pl.BlockSpec((1, pw, seq), lambda b, p: (b, p, 0)),
            pl.BlockSpec((1, seq, pw), lambda b, p: (b, 0, p)),
            pl.BlockSpec((1, 2 * FOX_AUG, seq), lambda b, p: (b, p, 0)),
        ],
        out_specs=pl.BlockSpec((1, pw, seq), lambda b, p: (b, p, 0)),
        out_shape=jax.ShapeDtypeStruct((bsz, width, seq), BF16),
        compiler_params=pltpu.CompilerParams(dimension_semantics=("parallel", "parallel"),
                                             vmem_limit_bytes=VMEM_LIMIT_BYTES),
        name="fox_attention",
    )(fqt, fk3, fvt, kf3, qf)


def _mlstm_kernel(qt_ref, k_ref, vt_ref, ot_ref, rows_ref, cols_ref, g_ref, h_ref, state_ref, m_ref):
    c = pl.program_id(1)
    L = qt_ref.shape[2]
    qk = k_ref.shape[2] // MLSTM_HEADS
    dv = vt_ref.shape[1] // MLSTM_HEADS

    @pl.when(c == 0)
    def _():
        state_ref[...] = jnp.zeros_like(state_ref)
        m_ref[...] = jnp.zeros_like(m_ref)

    src = lax.broadcasted_iota(jnp.int32, (L, L), 0)
    dst = lax.broadcasted_iota(jnp.int32, (L, L), 1)
    causal = src <= dst
    ones_rows = (lax.broadcasted_iota(jnp.int32, (VAUG_PAD, L), 0) == 0).astype(BF16)

    for bi, hd in [(bi, hd) for bi in range(qt_ref.shape[0]) for hd in range(MLSTM_HEADS)]:
        qt = qt_ref[bi, hd * qk:(hd + 1) * qk, :]
        kk = k_ref[bi, :, hd * qk:(hd + 1) * qk]
        vaug = jnp.concatenate([vt_ref[bi, hd * dv:(hd + 1) * dv, :], ones_rows], axis=0)
        smax = rows_ref[bi, hd:hd + 1, :]
        b = rows_ref[bi, FOX_HEADS + hd:FOX_HEADS + hd + 1, :]
        ig = rows_ref[bi, FOX_HEADS + MLSTM_HEADS + hd:FOX_HEADS + MLSTM_HEADS + hd + 1, :]
        scol = cols_ref[bi, :, FOX_HEADS + hd:FOX_HEADS + hd + 1]
        m_prev = m_ref[bi, hd]
        state = state_ref[bi, hd]

        g = jnp.broadcast_to(b[:, L - 1:L], (1, L))
        inter = b + m_prev
        m_t = jnp.maximum(inter, b + smax)
        pmat = jnp.exp2(jnp.where(causal, (b - m_t) + scol, -jnp.inf))
        scores = jnp.dot(kk, qt, preferred_element_type=F32) * pmat
        w_inter = jnp.exp2(inter - m_t)
        qtw = (qt.astype(F32) * w_inter).astype(BF16)
        ht = jnp.dot(jnp.concatenate([state.astype(BF16), vaug], axis=1),
                     jnp.concatenate([qtw, scores.astype(BF16)], axis=0),
                     preferred_element_type=F32)
        den = ht[dv:dv + 1, :]
        hh = ht[:dv, :] * (1.0 / jnp.maximum(jnp.abs(den), jnp.exp2(-m_t)))
        mu = jnp.mean(hh, axis=0, keepdims=True)
        dlt = hh - mu
        var = jnp.mean(dlt * dlt, axis=0, keepdims=True)
        hn = dlt * lax.rsqrt(var + LN_EPS) * g_ref[hd * dv:(hd + 1) * dv, :]
        th = jnp.tanh(ot_ref[bi, hd * dv:(hd + 1) * dv, :].astype(F32))
        h_ref[bi, hd * dv:(hd + 1) * dv, :] = (hn + hn * th).astype(BF16)

        m_new = jnp.maximum(g + m_prev, g + jnp.broadcast_to(smax[:, L - 1:L], (1, L)))
        decay = jnp.exp2(g + m_prev - m_new)
        w = jnp.exp2((g - m_new) + (ig - b))
        upd = jnp.dot((vaug.astype(F32) * w).astype(BF16), kk, preferred_element_type=F32)
        state_ref[bi, hd] = decay[:, :qk] * state + upd
        m_ref[bi, hd] = m_new


def _mlstm(mqt, mk3, mvt, mot, rows, cols3, gcol, *, chunk, nb):
    bsz, qkw, seq = mqt.shape
    d = mvt.shape[1]
    dv = d // MLSTM_HEADS
    qk = qkw // MLSTM_HEADS
    fmaj = lambda r: pl.BlockSpec((nb, r, chunk), lambda b, c: (b, 0, c))
    return pl.pallas_call(
        _mlstm_kernel,
        grid=(bsz // nb, seq // chunk),
        in_specs=[
            fmaj(qkw),
            pl.BlockSpec((nb, chunk, qkw), lambda b, c: (b, c, 0)),
            fmaj(d), fmaj(d), fmaj(GATE_ROWS),
            pl.BlockSpec((nb, chunk, GATE_ROWS), lambda b, c: (b, c, 0)),
            _const_spec(gcol.shape),
        ],
        out_specs=fmaj(d),
        out_shape=jax.ShapeDtypeStruct((bsz, d, seq), BF16),
        scratch_shapes=[pltpu.VMEM((nb, MLSTM_HEADS, dv + VAUG_PAD, qk), F32),
                        pltpu.VMEM((nb, MLSTM_HEADS, 1, chunk), F32)],
        compiler_params=pltpu.CompilerParams(dimension_semantics=("parallel", "arbitrary"),
                                             vmem_limit_bytes=VMEM_LIMIT_BYTES),
        name="mlstm",
    )(mqt, mk3, mvt, mot, rows, cols3, gcol)


def _tail_kernel(x_ref, mod_ref, att_ref, hm_ref, ga_ref, gb_ref, wpa_ref, wpb_ref, wout_ref,
                 l1g_ref, l1b_ref, wfi_ref, wfd_ref, l2g_ref, l2b_ref, o_ref, *, d_ff, nsub):
    g1 = mod_ref[0, 2:3, :]
    sh2 = mod_ref[0, 3:4, :]
    sc2 = mod_ref[0, 4:5, :]
    g2 = mod_ref[0, 5:6, :]
    sub = x_ref.shape[0] // nsub
    sl = [slice(k * sub, (k + 1) * sub) for k in range(nsub)]
    at = [jnp.dot(wpa_ref[...], att_ref[0, :, s], preferred_element_type=F32) for s in sl]
    bt = [jnp.dot(wpb_ref[...], hm_ref[0, :, s], preferred_element_type=F32) for s in sl]
    yt = [(jax.nn.sigmoid(ga_ref[0, :, s].astype(F32)) * a
           + jax.nn.sigmoid(gb_ref[0, :, s].astype(F32)) * b).astype(BF16) for s, a, b in zip(sl, at, bt)]
    z = [lax.dot_general(y, wout_ref[...], (((0,), (0,)), ((), ())), preferred_element_type=F32) for y in yt]
    x1 = [_ln(ALPHA * x_ref[s, :] + g1 * zz) * l1g_ref[...] + l1b_ref[...] for s, zz in zip(sl, z)]
    h2 = [(_ln(v) * (1.0 + sc2) + sh2).astype(BF16) for v in x1]
    gu = [jnp.dot(v, wfi_ref[...], preferred_element_type=F32) for v in h2]
    act = [(_silu(v[:, :d_ff]) * v[:, d_ff:]).astype(BF16) for v in gu]
    ff = [jnp.dot(v, wfd_ref[...], preferred_element_type=F32) for v in act]
    for s, v, f in zip(sl, x1, ff):
        o_ref[s, :] = _ln(ALPHA * v + g2 * f) * l2g_ref[...] + l2b_ref[...]


def _tail(x2, mod3, att_t, hm_t, ga_t, gb_t, wpa_t, wpb_t, w_out, l1g, l1b, wfi, wfd, l2g, l2b, *, seq, tm):
    tok, d = x2.shape
    nt = seq // tm
    d_ff = wfd.shape[0]
    fmaj = lambda rows: pl.BlockSpec((1, rows, tm), lambda t: (t // nt, 0, t % nt))
    consts = (wpa_t, wpb_t, w_out, l1g, l1b, wfi, wfd, l2g, l2b)
    return pl.pallas_call(
        functools.partial(_tail_kernel, d_ff=d_ff, nsub=2),
        grid=(tok // tm,),
        in_specs=[
            pl.BlockSpec((tm, d), lambda t: (t, 0)),
            pl.BlockSpec((1,) + mod3.shape[1:], lambda t: (t // nt, 0, 0)),
            fmaj(att_t.shape[1]), fmaj(d), fmaj(d), fmaj(d),
        ] + [_const_spec(a.shape) for a in consts],
        out_specs=pl.BlockSpec((tm, d), lambda t: (t, 0)),
        out_shape=jax.ShapeDtypeStruct((tok, d), F32),
        compiler_params=pltpu.CompilerParams(dimension_semantics=("parallel",), vmem_limit_bytes=VMEM_LIMIT_BYTES),
        name="tail",
    )(x2, mod3, att_t, hm_t, ga_t, gb_t, *consts)


def _pick(n, pref):
    while n % pref:
        pref //= 2
    return pref


def kernel(x, c, w_ada, b_ada, w_in, b_in, conv_w, conv_b, w_mq, w_mk, mh_norm_g, w_pa, w_pb, w_out,
           ln1_g, ln1_b, w_ffn_in, w_ffn_down, ln2_g, ln2_b):
    bsz, seq, d = x.shape
    tok = bsz * seq
    chunk = _pick(seq, 256)
    fox_blk = _pick(seq, 512)
    tm_in = _pick(seq, 512)
    tm_tail = _pick(seq, 512)

    x2 = x.reshape(tok, d)
    for l in range(DEPTH):
        mod3 = _modulation(c, w_ada[l], b_ada[l]).reshape(bsz, 6, d)

        splits = (FOX_WIDTH, FOX_WIDTH, FOX_WIDTH, FOX_HEADS, d, d, MLSTM_HEADS, MLSTM_HEADS, d, d, d)
        offs = [0]
        for s in splits:
            offs.append(offs[-1] + s)
        col = lambda i: (w_in[l][:, offs[i]:offs[i + 1]], b_in[l][offs[i]:offs[i + 1]])
        (wfq, bfq), (wfk, bfk), (wfv, bfv), (wff, bff), (wmu, bmu), (wmv, bmv), (wmi, bmi), (wmf, bmf), \
            (wmo, bmo), (wga, bga), (wgb, bgb) = [col(i) for i in range(len(splits))]
        scale = FOX_HEAD_DIM ** -0.5 * LOG2E
        wn = jnp.concatenate([wfk, wmu], axis=1).astype(BF16)
        bn = jnp.concatenate([bfk, bmu])[None, :]
        wt = jnp.concatenate([wff, wmi, wmf, wfq * scale, wfv, wmv, wmo * 0.5, wga, wgb], axis=1).T.astype(BF16)
        bt = jnp.concatenate([bff, bmi, bmf, bfq * scale, bfv, bmv, bmo * 0.5, bga, bgb])[:, None]

        tail_w = (w_ffn_in, w_ffn_down, w_out)
        in_call = all(_slab_rows(a.shape[1], tok // tm_in) is not None for a in tail_w)
        fk, mk, fqt, fvt, mqt, mvt, mot, gat, gbt, rows, cols, kf, qf, *cast = _inproj(
            x2, mod3, wn, bn, wt, bt, conv_w[l], conv_b[l][None, :], w_mk[l].astype(BF16),
            jnp.swapaxes(w_mq[l], 1, 2).astype(BF16), tail_w if in_call else (), l,
            bsz=bsz, seq=seq, tm=tm_in, chunk=chunk)
        wfi, wfd, wo = cast if in_call else [a[l].astype(BF16) for a in tail_w]

        att_t = _fox_attention(fqt, fk.reshape(bsz, seq, FOX_WIDTH), fvt, kf.reshape(bsz, seq, FOX_WIDTH), qf,
                               blk=fox_blk)

        hm_t = _mlstm(mqt, mk.reshape(bsz, seq, -1), mvt, mot, rows, cols.reshape(bsz, seq, GATE_ROWS),
                      0.5 * mh_norm_g[l][:, None], chunk=chunk, nb=_pick(bsz, 8))

        x2 = _tail(x2, mod3, att_t, hm_t, gat, gbt, w_pa[l].T.astype(BF16), w_pb[l].T.astype(BF16),
                   wo, ln1_g[l][None, :], ln1_b[l][None, :], wfi, wfd, ln2_g[l][None, :], ln2_b[l][None, :],
                   seq=seq, tm=tm_tail)
    return x2.reshape(bsz, seq, d)
```

```python
import functools

import jax
import jax.numpy as jnp
from jax import lax
from jax.experimental import pallas as pl
from jax.experimental.pallas import tpu as pltpu

F32 = jnp.float32
BF16 = jnp.bfloat16

FOX_HEADS = 8
FOX_HEAD_DIM = 64
FOX_WIDTH = FOX_HEADS * FOX_HEAD_DIM
MLSTM_HEADS = 4
CONV_WIDTH = 4
LN_EPS = 1e-5
DEPTH = 1
ALPHA = (2.0 * DEPTH) ** 0.25
LOG2E = 1.4426950408889634

VMEM_LIMIT_BYTES = 56 * 1024 * 1024
SUBLANES = 8
BF16_ROWS = 16
GATE_ROWS = 16
VAUG_PAD = 16
FOX_AUG = 16
FOX_SCORE_LEAD = 3
MLSTM_CHUNKS_PER_UNIT = 4

def _const_spec(shape):
    nd = len(shape)
    return pl.BlockSpec(shape, lambda *_: (0,) * nd, pipeline_mode=pl.Buffered(1))


def _ln(v):
    mu = jnp.mean(v, axis=-1, keepdims=True)
    d = v - mu
    var = jnp.mean(d * d, axis=-1, keepdims=True)
    return d * lax.rsqrt(var + LN_EPS)


def _silu(v):
    return v * jax.nn.sigmoid(v)


def _mod_kernel(c_ref, w_ref, b_ref, o_ref):
    a = _silu(c_ref[...]).astype(BF16)
    o_ref[...] = jnp.dot(a, w_ref[...].astype(BF16), preferred_element_type=F32) + b_ref[...]


def _modulation(c, w_ada, b_ada):
    bsz, d = c.shape
    n = w_ada.shape[1]
    return pl.pallas_call(
        _mod_kernel,
        grid=(n // d,),
        in_specs=[
            pl.BlockSpec((bsz, d), lambda j: (0, 0)),
            pl.BlockSpec((d, d), lambda j: (0, j)),
            pl.BlockSpec((1, d), lambda j: (0, j)),
        ],
        out_specs=pl.BlockSpec((bsz, d), lambda j: (0, j)),
        out_shape=jax.ShapeDtypeStruct((bsz, n), F32),
        name="modulation",
    )(c, w_ada, b_ada.reshape(1, n))


def _prefix_scan_lanes(v, width, op, identity):
    pos = lax.broadcasted_iota(jnp.int32, v.shape, 1) & (width - 1)
    d = 1
    while d < width:
        v = op(v, jnp.where(pos >= d, pltpu.roll(v, d, axis=1), identity))
        d *= 2
    return v


INPROJ_INPUTS = 10
INPROJ_OUTPUTS = 13


def _inproj_kernel(*refs, tiles_per_seq, chunk, d_model):
    (x_ref, mod_ref, wn_ref, bn_ref, wt_ref, bt_ref, cw_ref, cb_ref, wmk_ref, wmqt_ref) = refs[:INPROJ_INPUTS]
    n_side = (len(refs) - INPROJ_INPUTS - INPROJ_OUTPUTS - 2) // 2
    side_in = refs[INPROJ_INPUTS:INPROJ_INPUTS + n_side]
    outs = refs[INPROJ_INPUTS + n_side:]
    (fk_ref, mk_ref, fqt_ref, fvt_ref, mqt_ref, mvt_ref, mot_ref, gat_ref, gbt_ref,
     rows_ref, cols_ref, kf_ref, qf_ref) = outs[:INPROJ_OUTPUTS]
    side_out = outs[INPROJ_OUTPUTS:INPROJ_OUTPUTS + n_side]
    prev_ref, fcarry_ref = outs[INPROJ_OUTPUTS + n_side:]

    for src, dst in zip(side_in, side_out):
        dst[...] = src[0].astype(BF16)

    tm = x_ref.shape[0]
    first = (pl.program_id(0) % tiles_per_seq) == 0
    sh1 = mod_ref[0, 0:1, :]
    sc1 = mod_ref[0, 1:2, :]
    @pl.when(first)
    def _():
        prev_ref[...] = jnp.zeros_like(prev_ref)
        fcarry_ref[...] = jnp.zeros_like(fcarry_ref)

    halves = [slice(k * (tm // 2), (k + 1) * (tm // 2)) for k in range(2)]
    h_parts, yn_parts = [], []
    for rows in halves:
        h_parts.append((_ln(x_ref[rows, :]) * (1.0 + sc1) + sh1).astype(BF16))
        yn_parts.append(jnp.dot(h_parts[-1], wn_ref[...], preferred_element_type=F32) + bn_ref[...])
    h = jnp.concatenate(h_parts, axis=0)
    yn = jnp.concatenate(yn_parts, axis=0)

    def seg(r0, n):
        y = lax.dot_general(wt_ref[r0:r0 + n, :], h, (((1,), (1,)), ((), ())), preferred_element_type=F32)
        return y + bt_ref[r0:r0 + n, :]

    fk_ref[...] = yn[:, :FOX_WIDTH].astype(BF16)
    mu = yn[:, FOX_WIDTH:]

    gq = seg(0, GATE_ROWS + FOX_WIDTH)
    fqt_ref[0] = gq[GATE_ROWS:].astype(BF16)
    g = gq[:GATE_ROWS]
    ls = jax.nn.log_sigmoid(g) * LOG2E
    fcum = _prefix_scan_lanes(ls[0:FOX_HEADS], tm, jnp.add, 0.0) + fcarry_ref[:, 0:1]
    fcarry_ref[...] = jnp.broadcast_to(fcum[:, tm - 1:tm], fcarry_ref.shape)
    bcum = _prefix_scan_lanes(ls[FOX_HEADS:], chunk, jnp.add, 0.0)[MLSTM_HEADS:]
    ig = g[FOX_HEADS:FOX_HEADS + MLSTM_HEADS] * LOG2E
    src = ig - bcum
    smax = _prefix_scan_lanes(jnp.concatenate([src, src], axis=0), chunk, jnp.maximum, -jnp.inf)[MLSTM_HEADS:]
    zeros4 = jnp.zeros_like(ig)
    rows_ref[0] = jnp.concatenate([smax, zeros4, bcum, ig], axis=0)
    cols_ref[...] = jnp.concatenate([zeros4, zeros4, src, zeros4], axis=0).T

    f_hi = fcum.astype(BF16).astype(F32)
    f_mid = (fcum - f_hi).astype(BF16).astype(F32)
    f_lo = (fcum - f_hi - f_mid).astype(BF16).astype(F32)
    j = lax.broadcasted_iota(jnp.int32, (FOX_AUG, tm), 0)
    for hd in range(FOX_HEADS):
        odd = hd % 2
        ones_at = (j >= 3 + 3 * odd) & (j < 6 + 3 * odd)
        blk16 = jnp.where(j == 0, f_hi[hd:hd + 1], jnp.where(j == 1, f_mid[hd:hd + 1], jnp.where(
            j == 2, f_lo[hd:hd + 1], jnp.where(ones_at, 1.0, 0.0))))
        qf_ref[0, hd * FOX_AUG:(hd + 1) * FOX_AUG, :] = blk16.astype(BF16)
    kf_ref[...] = jnp.zeros_like(kf_ref)
    for pr in range(FOX_HEADS // 2):
        ea, eb = 2 * pr, 2 * pr + 1
        blk16 = jnp.where(j < 3, 1.0, 0.0)
        for entry, term in ((3, f_hi[ea:ea + 1]), (4, f_mid[ea:ea + 1]), (5, f_lo[ea:ea + 1]),
                            (6, f_hi[eb:eb + 1]), (7, f_mid[eb:eb + 1]), (8, f_lo[eb:eb + 1])):
            blk16 = jnp.where(j == entry, -term, blk16)
        kf_ref[:, pr * 2 * FOX_HEAD_DIM:pr * 2 * FOX_HEAD_DIM + FOX_AUG] = blk16.T.astype(BF16)

    r = GATE_ROWS + FOX_WIDTH
    for ref, n in ((fvt_ref, FOX_WIDTH), (mvt_ref, d_model), (mot_ref, d_model),
                   (gat_ref, d_model), (gbt_ref, d_model)):
        ref[0] = seg(r, n).astype(BF16)
        r += n

    u = cb_ref[...] + cw_ref[CONV_WIDTH - 1:CONV_WIDTH, :] * mu
    for k in range(1, CONV_WIDTH):
        u = u + cw_ref[CONV_WIDTH - 1 - k:CONV_WIDTH - k, :] * pltpu.roll(mu, k, axis=0)
    head = jnp.concatenate([prev_ref[...], mu[:SUBLANES]], axis=0)
    u_head = cb_ref[...] + cw_ref[0:1, :] * head[SUBLANES - 3:2 * SUBLANES - 3]
    for j in range(1, CONV_WIDTH):
        off = SUBLANES - (CONV_WIDTH - 1) + j
        u_head = u_head + cw_ref[j:j + 1, :] * head[off:off + SUBLANES]
    u = jnp.concatenate([u_head, u[SUBLANES:]], axis=0)
    prev_ref[...] = mu[tm - SUBLANES:, :]
    u = _silu(u).astype(BF16)

    inner = d_model // MLSTM_HEADS
    qk = wmk_ref.shape[2]
    for hd in range(MLSTM_HEADS):
        uh = u[:, hd * inner:(hd + 1) * inner]
        mk_ref[:, hd * qk:(hd + 1) * qk] = jnp.dot(uh, wmk_ref[hd], preferred_element_type=F32).astype(BF16)
        qt = lax.dot_general(wmqt_ref[hd], uh, (((1,), (1,)), ((), ())), preferred_element_type=F32)
        mqt_ref[0, hd * qk:(hd + 1) * qk, :] = (qt * (qk ** -0.5)).astype(BF16)


def _slab_rows(rows, nsteps):
    steps = 1
    while steps <= nsteps:
        if nsteps % steps == 0 and (rows * steps) % nsteps == 0 and (rows * steps // nsteps) % BF16_ROWS == 0:
            return rows * steps // nsteps, steps
        steps *= 2
    return None


def _inproj(x2, mod3, wn, bn, wt, bt, conv_w, conv_b, w_mk, w_mqt, side, layer, *, bsz, seq, tm, chunk):
    tok, d = x2.shape
    nt = seq // tm
    nsteps = tok // tm
    qkw = w_mk.shape[0] * w_mk.shape[2]
    tmaj = lambda width: pl.BlockSpec((tm, width), lambda t: (t, 0))
    fmaj = lambda rows: pl.BlockSpec((1, rows, tm), lambda t: (t // nt, 0, t % nt))

    def slab_in(a):
        rows, steps = _slab_rows(a.shape[1], nsteps)
        return pl.BlockSpec((1, rows, a.shape[2]), lambda t: (layer, t // steps, 0))

    def slab_out(a):
        rows, steps = _slab_rows(a.shape[1], nsteps)
        return pl.BlockSpec((rows, a.shape[2]), lambda t: (t // steps, 0))

    out_shape = (
        jax.ShapeDtypeStruct((tok, FOX_WIDTH), BF16),
        jax.ShapeDtypeStruct((tok, qkw), BF16),
        jax.ShapeDtypeStruct((bsz, FOX_WIDTH, seq), BF16),
        jax.ShapeDtypeStruct((bsz, FOX_WIDTH, seq), BF16),
        jax.ShapeDtypeStruct((bsz, qkw, seq), BF16),
        jax.ShapeDtypeStruct((bsz, d, seq), BF16),
        jax.ShapeDtypeStruct((bsz, d, seq), BF16),
        jax.ShapeDtypeStruct((bsz, d, seq), BF16),
        jax.ShapeDtypeStruct((bsz, d, seq), BF16),
        jax.ShapeDtypeStruct((bsz, GATE_ROWS, seq), F32),
        jax.ShapeDtypeStruct((tok, GATE_ROWS), F32),
        jax.ShapeDtypeStruct((tok, FOX_WIDTH), BF16),
        jax.ShapeDtypeStruct((bsz, FOX_HEADS * FOX_AUG, seq), BF16),
    ) + tuple(jax.ShapeDtypeStruct(a.shape[1:], BF16) for a in side)
    out_specs = (tmaj(FOX_WIDTH), tmaj(qkw), fmaj(FOX_WIDTH), fmaj(FOX_WIDTH), fmaj(qkw),
                 fmaj(d), fmaj(d), fmaj(d), fmaj(d), fmaj(GATE_ROWS), tmaj(GATE_ROWS),
                 tmaj(FOX_WIDTH), fmaj(FOX_HEADS * FOX_AUG)) + tuple(slab_out(a) for a in side)
    kern = functools.partial(_inproj_kernel, tiles_per_seq=nt, chunk=chunk, d_model=d)
    return pl.pallas_call(
        kern,
        grid=(tok // tm,),
        in_specs=[
            pl.BlockSpec((tm, d), lambda t: (t, 0)),
            pl.BlockSpec((1,) + mod3.shape[1:], lambda t: (t // nt, 0, 0)),
            _const_spec(wn.shape), _const_spec(bn.shape), _const_spec(wt.shape), _const_spec(bt.shape),
            _const_spec(conv_w.shape), _const_spec(conv_b.shape), _const_spec(w_mk.shape), _const_spec(w_mqt.shape),
        ] + [slab_in(a) for a in side],
        out_specs=out_specs,
        out_shape=out_shape,
        scratch_shapes=[pltpu.VMEM((SUBLANES, d), F32), pltpu.VMEM((FOX_HEADS, 128), F32)],
        compiler_params=pltpu.CompilerParams(dimension_semantics=("arbitrary",), vmem_limit_bytes=VMEM_LIMIT_BYTES),
        name="inproj",
    )(x2, mod3, wn, bn, wt, bt, conv_w, conv_b, w_mk, w_mqt, *side)


def _fox_scores(qi, a, qt_ref, k_ref, kf_ref, qf_ref, blk):
    hd = FOX_HEAD_DIM
    half = blk // 2
    q0 = qi * blk
    main = q0 + half
    keys = q0 + blk
    qt2 = qt_ref[0, :, q0:keys]
    row = lax.broadcasted_iota(jnp.int32, qt2.shape, 0)
    tri_main = (lax.broadcasted_iota(jnp.int32, (half, blk), 0)
                <= lax.broadcasted_iota(jnp.int32, (half, blk), 1))
    kb = jnp.concatenate([k_ref[0, 0:keys, :], kf_ref[0, 0:keys, :]], axis=1)
    qm = jnp.where((row >= a * hd) & (row < (a + 1) * hd), qt2, jnp.zeros_like(qt2))
    qm = jnp.concatenate([qm, qf_ref[0, a * FOX_AUG:(a + 1) * FOX_AUG, q0:keys],
                          jnp.zeros((2 * hd - FOX_AUG, blk), BF16)], axis=0)
    s_main = jnp.dot(kb[:main], qm, preferred_element_type=F32)
    s_last = jnp.dot(kb[main:], qm[:, half:], preferred_element_type=F32)
    s_edge = jnp.where(tri_main, s_main[q0:], -jnp.inf)
    s_last = jnp.where(tri_main[:, :half], s_last, -jnp.inf)
    m = jnp.max(s_edge, axis=0, keepdims=True)
    if q0:
        m = jnp.maximum(m, jnp.max(s_main[:q0], axis=0, keepdims=True))
    m_hi = jnp.maximum(m[:, half:], jnp.max(s_last, axis=0, keepdims=True))
    m = jnp.concatenate([m[:, :half], m_hi], axis=1)
    return s_main[:q0] if q0 else None, s_edge, s_last, m, m_hi


def _fox_values(qi, a, scores, vt_ref, o_ref, blk):
    hd = FOX_HEAD_DIM
    half = blk // 2
    q0 = qi * blk
    main = q0 + half
    keys = q0 + blk
    s_top, s_edge, s_last, m, m_hi = scores
    ones = (lax.broadcasted_iota(jnp.int32, (VAUG_PAD, keys), 0) == 0).astype(BF16)
    vaug = jnp.concatenate([vt_ref[0, a * hd:(a + 1) * hd, 0:keys], ones], axis=0)
    acc = jnp.dot(vaug[:, q0:main], jnp.exp2(s_edge - m).astype(BF16), preferred_element_type=F32)
    if q0:
        acc = acc + jnp.dot(vaug[:, :q0], jnp.exp2(s_top - m).astype(BF16), preferred_element_type=F32)
    acc_hi = jnp.dot(vaug[:, main:], jnp.exp2(s_last - m_hi).astype(BF16), preferred_element_type=F32)
    acc = jnp.concatenate([acc[:, :half], acc[:, half:] + acc_hi], axis=1)
    o_ref[0, a * hd:(a + 1) * hd, q0:keys] = (acc[:hd] / acc[hd:hd + 1]).astype(BF16)


def _mlstm_chunk(ci, state, m_prev, qt_ref, k_ref, vt_ref, ot_ref, mrow_ref, mcol_ref, g_ref, h_ref, L):
    dv = vt_ref.shape[1]
    qk = k_ref.shape[2]
    cs = slice(ci * L, (ci + 1) * L)
    causal = (lax.broadcasted_iota(jnp.int32, (L, L), 0) <= lax.broadcasted_iota(jnp.int32, (L, L), 1))
    ones_rows = (lax.broadcasted_iota(jnp.int32, (VAUG_PAD, L), 0) == 0).astype(BF16)
    qt = qt_ref[0, :, cs]
    kk = k_ref[0, cs, :]
    vaug = jnp.concatenate([vt_ref[0, :, cs], ones_rows], axis=0)
    smax = mrow_ref[0, 0, 0:1, cs]
    b = mrow_ref[0, 0, 1:2, cs]
    ig = mrow_ref[0, 0, 2:3, cs]
    scol = mcol_ref[0, 0, cs, :]

    g = jnp.broadcast_to(b[:, L - 1:L], (1, L))
    inter = b + m_prev
    m_t = jnp.maximum(inter, b + smax)
    pmat = jnp.exp2(jnp.where(causal, (b - m_t) + scol, -jnp.inf))
    scores = jnp.dot(kk, qt, preferred_element_type=F32) * pmat
    w_inter = jnp.exp2(inter - m_t)
    qtw = (qt.astype(F32) * w_inter).astype(BF16)
    ht = jnp.dot(jnp.concatenate([state.astype(BF16), vaug], axis=1),
                 jnp.concatenate([qtw, scores.astype(BF16)], axis=0),
                 preferred_element_type=F32)
    den = ht[dv:dv + 1, :]
    hh = ht[:dv, :] * (1.0 / jnp.maximum(jnp.abs(den), jnp.exp2(-m_t)))
    mu = jnp.mean(hh, axis=0, keepdims=True)
    dlt = hh - mu
    var = jnp.mean(dlt * dlt, axis=0, keepdims=True)
    hn = dlt * lax.rsqrt(var + LN_EPS) * g_ref[...]
    th = jnp.tanh(ot_ref[0, :, cs].astype(F32))
    h_ref[0, :, cs] = (hn + hn * th).astype(BF16)

    m_new = jnp.maximum(g + m_prev, g + jnp.broadcast_to(smax[:, L - 1:L], (1, L)))
    decay = jnp.exp2(g + m_prev - m_new)
    w = jnp.exp2((g - m_new) + (ig - b))
    upd = jnp.dot((vaug.astype(F32) * w).astype(BF16), kk, preferred_element_type=F32)
    return decay[:, :qk] * state + upd, m_new


def _mixer_kernel(fq_ref, fk_ref, fv_ref, kf_ref, qf_ref, mq_ref, mk_ref, mv_ref, mo_ref, mrow_ref, mcol_ref,
                  g_ref, att_ref, hm_ref, *, blk, chunk):
    units = [(qi, a) for qi in reversed(range(fk_ref.shape[1] // blk)) for a in range(2)]
    nchunks = mk_ref.shape[1] // chunk
    state = jnp.zeros((mv_ref.shape[1] + VAUG_PAD, mk_ref.shape[2]), F32)
    m_run = jnp.zeros((1, chunk), F32)
    pending = []
    for n, (qi, a) in enumerate(units):
        pending.append((qi, a, _fox_scores(qi, a, fq_ref, fk_ref, kf_ref, qf_ref, blk)))
        for ci in range(min(nchunks, MLSTM_CHUNKS_PER_UNIT * n), min(nchunks, MLSTM_CHUNKS_PER_UNIT * (n + 1))):
            state, m_run = _mlstm_chunk(ci, state, m_run, mq_ref, mk_ref, mv_ref, mo_ref, mrow_ref, mcol_ref,
                                        g_ref, hm_ref, chunk)
        if len(pending) > FOX_SCORE_LEAD:
            _fox_values(*pending.pop(0), fv_ref, att_ref, blk)
    for unit in pending:
        _fox_values(*unit, fv_ref, att_ref, blk)


def _mixer(fqt, fk3, fvt, kf3, qf, mqt, mk3, mvt, mot, mrows, mcols, gcol, *, blk, chunk):
    bsz, width, seq = fqt.shape
    pairs = FOX_HEADS // 2
    assert pairs == MLSTM_HEADS
    pw = 2 * FOX_HEAD_DIM
    qk = mqt.shape[1] // MLSTM_HEADS
    d = mvt.shape[1]
    dv = d // MLSTM_HEADS
    fmaj = lambda rows: pl.BlockSpec((1, rows, seq), lambda b, p: (b, p, 0))
    tmaj = lambda cols: pl.BlockSpec((1, seq, cols), lambda b, p: (b, 0, p))
    return pl.pallas_call(
        functools.partial(_mixer_kernel, blk=blk, chunk=chunk),
        grid=(bsz, pairs),
        in_specs=[
            fmaj(pw), tmaj(pw), fmaj(pw), tmaj(pw), fmaj(2 * FOX_AUG),
            fmaj(qk), tmaj(qk), fmaj(dv), fmaj(dv),
            pl.BlockSpec((1, 1) + mrows.shape[2:], lambda b, p: (b, p, 0, 0)),
            pl.BlockSpec((1, 1) + mcols.shape[2:], lambda b, p: (b, p, 0, 0)),
            pl.BlockSpec((dv, 1), lambda b, p: (p, 0)),
        ],
        out_specs=(fmaj(pw), fmaj(dv)),
        out_shape=(jax.ShapeDtypeStruct((bsz, width, seq), BF16), jax.ShapeDtypeStruct((bsz, d, seq), BF16)),
        compiler_params=pltpu.CompilerParams(dimension_semantics=("parallel", "parallel"),
                                             vmem_limit_bytes=VMEM_LIMIT_BYTES),
        name="mixer",
    )(fqt, fk3, fvt, kf3, qf, mqt, mk3, mvt, mot, mrows, mcols, gcol)


def _tail_kernel(x_ref, mod_ref, att_ref, hm_ref, ga_ref, gb_ref, wpa_ref, wpb_ref, wout_ref,
                 l1g_ref, l1b_ref, wfi_ref, wfd_ref, l2g_ref, l2b_ref, o_ref, *, d_ff, nsub):
    g1 = mod_ref[0, 2:3, :]
    sh2 = mod_ref[0, 3:4, :]
    sc2 = mod_ref[0, 4:5, :]
    g2 = mod_ref[0, 5:6, :]
    sub = x_ref.shape[0] // nsub
    sl = [slice(k * sub, (k + 1) * sub) for k in range(nsub)]
    at = [jnp.dot(wpa_ref[...], att_ref[0, :, s], preferred_element_type=F32) for s in sl]
    bt = [jnp.dot(wpb_ref[...], hm_ref[0, :, s], preferred_element_type=F32) for s in sl]
    yt = [(jax.nn.sigmoid(ga_ref[0, :, s].astype(F32)) * a
           + jax.nn.sigmoid(gb_ref[0, :, s].astype(F32)) * b).astype(BF16) for s, a, b in zip(sl, at, bt)]
    z = [lax.dot_general(y, wout_ref[...], (((0,), (0,)), ((), ())), preferred_element_type=F32) for y in yt]
    x1 = [_ln(ALPHA * x_ref[s, :] + g1 * zz) * l1g_ref[...] + l1b_ref[...] for s, zz in zip(sl, z)]
    h2 = [(_ln(v) * (1.0 + sc2) + sh2).astype(BF16) for v in x1]
    gu = [jnp.dot(v, wfi_ref[...], preferred_element_type=F32) for v in h2]
    act = [(_silu(v[:, :d_ff]) * v[:, d_ff:]).astype(BF16) for v in gu]
    ff = [jnp.dot(v, wfd_ref[...], preferred_element_type=F32) for v in act]
    for s, v, f in zip(sl, x1, ff):
        o_ref[s, :] = _ln(ALPHA * v + g2 * f) * l2g_ref[...] + l2b_ref[...]


def _tail(x2, mod3, att_t, hm_t, ga_t, gb_t, wpa_t, wpb_t, w_out, l1g, l1b, wfi, wfd, l2g, l2b, *, seq, tm):
    tok, d = x2.shape
    nt = seq // tm
    d_ff = wfd.shape[0]
    fmaj = lambda rows: pl.BlockSpec((1, rows, tm), lambda t: (t // nt, 0, t % nt))
    consts = (wpa_t, wpb_t, w_out, l1g, l1b, wfi, wfd, l2g, l2b)
    return pl.pallas_call(
        functools.partial(_tail_kernel, d_ff=d_ff, nsub=2),
        grid=(tok // tm,),
        in_specs=[
            pl.BlockSpec((tm, d), lambda t: (t, 0)),
            pl.BlockSpec((1,) + mod3.shape[1:], lambda t: (t // nt, 0, 0)),
            fmaj(att_t.shape[1]), fmaj(d), fmaj(d), fmaj(d),
        ] + [_const_spec(a.shape) for a in consts],
        out_specs=pl.BlockSpec((tm, d), lambda t: (t, 0)),
        out_shape=jax.ShapeDtypeStruct((tok, d), F32),
        compiler_params=pltpu.CompilerParams(dimension_semantics=("parallel",), vmem_limit_bytes=VMEM_LIMIT_BYTES),
        name="tail",
    )(x2, mod3, att_t, hm_t, ga_t, gb_t, *consts)


def _pick(n, pref):
    while n % pref:
        pref //= 2
    return pref


def kernel(x, c, w_ada, b_ada, w_in, b_in, conv_w, conv_b, w_mq, w_mk, mh_norm_g, w_pa, w_pb, w_out,
           ln1_g, ln1_b, w_ffn_in, w_ffn_down, ln2_g, ln2_b):
    bsz, seq, d = x.shape
    tok = bsz * seq
    chunk = _pick(seq, 256)
    fox_blk = _pick(seq, 512)
    tm_in = _pick(seq, 512)
    tm_tail = _pick(seq, 512)

    x2 = x.reshape(tok, d)
    for l in range(DEPTH):
        mod3 = _modulation(c, w_ada[l], b_ada[l]).reshape(bsz, 6, d)

        splits = (FOX_WIDTH, FOX_WIDTH, FOX_WIDTH, FOX_HEADS, d, d, MLSTM_HEADS, MLSTM_HEADS, d, d, d)
        offs = [0]
        for s in splits:
            offs.append(offs[-1] + s)
        col = lambda i: (w_in[l][:, offs[i]:offs[i + 1]], b_in[l][offs[i]:offs[i + 1]])
        (wfq, bfq), (wfk, bfk), (wfv, bfv), (wff, bff), (wmu, bmu), (wmv, bmv), (wmi, bmi), (wmf, bmf), \
            (wmo, bmo), (wga, bga), (wgb, bgb) = [col(i) for i in range(len(splits))]
        scale = FOX_HEAD_DIM ** -0.5 * LOG2E
        wn = jnp.concatenate([wfk, wmu], axis=1).astype(BF16)
        bn = jnp.concatenate([bfk, bmu])[None, :]
        wt = jnp.concatenate([wff, wmi, wmf, wfq * scale, wfv, wmv, wmo * 0.5, wga, wgb], axis=1).T.astype(BF16)
        bt = jnp.concatenate([bff, bmi, bmf, bfq * scale, bfv, bmv, bmo * 0.5, bga, bgb])[:, None]

        tail_w = (w_ffn_in, w_ffn_down, w_out)
        in_call = all(_slab_rows(a.shape[1], tok // tm_in) is not None for a in tail_w)
        fk, mk, fqt, fvt, mqt, mvt, mot, gat, gbt, rows, cols, kf, qf, *cast = _inproj(
            x2, mod3, wn, bn, wt, bt, conv_w[l], conv_b[l][None, :], w_mk[l].astype(BF16),
            jnp.swapaxes(w_mq[l], 1, 2).astype(BF16), tail_w if in_call else (), l,
            bsz=bsz, seq=seq, tm=tm_in, chunk=chunk)
        wfi, wfd, wo = cast if in_call else [a[l].astype(BF16) for a in tail_w]

        mrows = jnp.stack([rows[:, 0:MLSTM_HEADS], rows[:, FOX_HEADS:FOX_HEADS + MLSTM_HEADS],
                           rows[:, FOX_HEADS + MLSTM_HEADS:]], axis=2)
        mcols = cols.reshape(bsz, seq, GATE_ROWS)[:, :, FOX_HEADS:FOX_HEADS + MLSTM_HEADS]
        mcols = mcols.transpose(0, 2, 1)[..., None]
        att_t, hm_t = _mixer(fqt, fk.reshape(bsz, seq, FOX_WIDTH), fvt, kf.reshape(bsz, seq, FOX_WIDTH), qf,
                             mqt, mk.reshape(bsz, seq, -1), mvt, mot, mrows, mcols,
                             0.5 * mh_norm_g[l][:, None], blk=fox_blk, chunk=chunk)

        x2 = _tail(x2, mod3, att_t, hm_t, gat, gbt, w_pa[l].T.astype(BF16), w_pb[l].T.astype(BF16),
                   wo, ln1_g[l][None, :], ln1_b[l][None, :], wfi, wfd, ln2_g[l][None, :], ln2_b[l][None, :],
                   seq=seq, tm=tm_tail)
    return x2.reshape(bsz, seq, d)
```

```python
import functools

import jax
import jax.numpy as jnp
from jax import lax
from jax.experimental import pallas as pl
from jax.experimental.pallas import tpu as pltpu

F32 = jnp.float32
BF16 = jnp.bfloat16

FOX_HEADS = 8
FOX_HEAD_DIM = 64
FOX_WIDTH = FOX_HEADS * FOX_HEAD_DIM
MLSTM_HEADS = 4
CONV_WIDTH = 4
LN_EPS = 1e-5
DEPTH = 1
ALPHA = (2.0 * DEPTH) ** 0.25
LOG2E = 1.4426950408889634

VMEM_LIMIT_BYTES = 56 * 1024 * 1024
SUBLANES = 8
BF16_ROWS = 16
GATE_ROWS = 16
VAUG_PAD = 16
FOX_AUG = 16
FOX_SCORE_LEAD = 2

def _const_spec(shape):
    nd = len(shape)
    return pl.BlockSpec(shape, lambda *_: (0,) * nd, pipeline_mode=pl.Buffered(1))


def _ln(v):
    mu = jnp.mean(v, axis=-1, keepdims=True)
    d = v - mu
    var = jnp.mean(d * d, axis=-1, keepdims=True)
    return d * lax.rsqrt(var + LN_EPS)


def _silu(v):
    return v * jax.nn.sigmoid(v)


def _mod_kernel(c_ref, w_ref, b_ref, o_ref):
    a = _silu(c_ref[...]).astype(BF16)
    o_ref[...] = jnp.dot(a, w_ref[...].astype(BF16), preferred_element_type=F32) + b_ref[...]


def _modulation(c, w_ada, b_ada):
    bsz, d = c.shape
    n = w_ada.shape[1]
    return pl.pallas_call(
        _mod_kernel,
        grid=(n // d,),
        in_specs=[
            pl.BlockSpec((bsz, d), lambda j: (0, 0)),
            pl.BlockSpec((d, d), lambda j: (0, j)),
            pl.BlockSpec((1, d), lambda j: (0, j)),
        ],
        out_specs=pl.BlockSpec((bsz, d), lambda j: (0, j)),
        out_shape=jax.ShapeDtypeStruct((bsz, n), F32),
        name="modulation",
    )(c, w_ada, b_ada.reshape(1, n))


def _prefix_scan_lanes(v, width, op, identity):
    pos = lax.broadcasted_iota(jnp.int32, v.shape, 1) & (width - 1)
    d = 1
    while d < width:
        v = op(v, jnp.where(pos >= d, pltpu.roll(v, d, axis=1), identity))
        d *= 2
    return v


INPROJ_INPUTS = 10
INPROJ_OUTPUTS = 13


def _inproj_kernel(*refs, tiles_per_seq, chunk, d_model):
    (x_ref, mod_ref, wn_ref, bn_ref, wt_ref, bt_ref, cw_ref, cb_ref, wmk_ref, wmqt_ref) = refs[:INPROJ_INPUTS]
    n_side = (len(refs) - INPROJ_INPUTS - INPROJ_OUTPUTS - 2) // 2
    side_in = refs[INPROJ_INPUTS:INPROJ_INPUTS + n_side]
    outs = refs[INPROJ_INPUTS + n_side:]
    (fk_ref, mk_ref, fqt_ref, fvt_ref, mqt_ref, mvt_ref, mot_ref, gat_ref, gbt_ref,
     rows_ref, cols_ref, kf_ref, qf_ref) = outs[:INPROJ_OUTPUTS]
    side_out = outs[INPROJ_OUTPUTS:INPROJ_OUTPUTS + n_side]
    prev_ref, fcarry_ref = outs[INPROJ_OUTPUTS + n_side:]

    for src, dst in zip(side_in, side_out):
        dst[...] = src[0].astype(BF16)

    tm = x_ref.shape[0]
    first = (pl.program_id(0) % tiles_per_seq) == 0
    sh1 = mod_ref[0, 0:1, :]
    sc1 = mod_ref[0, 1:2, :]
    @pl.when(first)
    def _():
        prev_ref[...] = jnp.zeros_like(prev_ref)
        fcarry_ref[...] = jnp.zeros_like(fcarry_ref)

    halves = [slice(k * (tm // 2), (k + 1) * (tm // 2)) for k in range(2)]
    h_parts, yn_parts = [], []
    for rows in halves:
        h_parts.append((_ln(x_ref[rows, :]) * (1.0 + sc1) + sh1).astype(BF16))
        yn_parts.append(jnp.dot(h_parts[-1], wn_ref[...], preferred_element_type=F32) + bn_ref[...])
    h = jnp.concatenate(h_parts, axis=0)
    yn = jnp.concatenate(yn_parts, axis=0)

    def seg(r0, n):
        y = lax.dot_general(wt_ref[r0:r0 + n, :], h, (((1,), (1,)), ((), ())), preferred_element_type=F32)
        return y + bt_ref[r0:r0 + n, :]

    fk_ref[...] = yn[:, :FOX_WIDTH].astype(BF16)
    mu = yn[:, FOX_WIDTH:]

    gq = seg(0, GATE_ROWS + FOX_WIDTH)
    fqt_ref[0] = gq[GATE_ROWS:].astype(BF16)
    g = gq[:GATE_ROWS]
    ls = jax.nn.log_sigmoid(g) * LOG2E
    fcum = _prefix_scan_lanes(ls[0:FOX_HEADS], tm, jnp.add, 0.0) + fcarry_ref[:, 0:1]
    fcarry_ref[...] = jnp.broadcast_to(fcum[:, tm - 1:tm], fcarry_ref.shape)
    bcum = _prefix_scan_lanes(ls[FOX_HEADS:], chunk, jnp.add, 0.0)[MLSTM_HEADS:]
    ig = g[FOX_HEADS:FOX_HEADS + MLSTM_HEADS] * LOG2E
    src = ig - bcum
    smax = _prefix_scan_lanes(jnp.concatenate([src, src], axis=0), chunk, jnp.maximum, -jnp.inf)[MLSTM_HEADS:]
    zeros4 = jnp.zeros_like(ig)
    rows_ref[0] = jnp.concatenate([smax, zeros4, bcum, ig], axis=0)
    cols_ref[...] = jnp.concatenate([zeros4, zeros4, src, zeros4], axis=0).T

    f_hi = fcum.astype(BF16).astype(F32)
    f_mid = (fcum - f_hi).astype(BF16).astype(F32)
    f_lo = (fcum - f_hi - f_mid).astype(BF16).astype(F32)
    j = lax.broadcasted_iota(jnp.int32, (FOX_AUG, tm), 0)
    for hd in range(FOX_HEADS):
        odd = hd % 2
        ones_at = (j >= 3 + 3 * odd) & (j < 6 + 3 * odd)
        blk16 = jnp.where(j == 0, f_hi[hd:hd + 1], jnp.where(j == 1, f_mid[hd:hd + 1], jnp.where(
            j == 2, f_lo[hd:hd + 1], jnp.where(ones_at, 1.0, 0.0))))
        qf_ref[0, hd * FOX_AUG:(hd + 1) * FOX_AUG, :] = blk16.astype(BF16)
    kf_ref[...] = jnp.zeros_like(kf_ref)
    for pr in range(FOX_HEADS // 2):
        ea, eb = 2 * pr, 2 * pr + 1
        blk16 = jnp.where(j < 3, 1.0, 0.0)
        for entry, term in ((3, f_hi[ea:ea + 1]), (4, f_mid[ea:ea + 1]), (5, f_lo[ea:ea + 1]),
                            (6, f_hi[eb:eb + 1]), (7, f_mid[eb:eb + 1]), (8, f_lo[eb:eb + 1])):
            blk16 = jnp.where(j == entry, -term, blk16)
        kf_ref[:, pr * 2 * FOX_HEAD_DIM:pr * 2 * FOX_HEAD_DIM + FOX_AUG] = blk16.T.astype(BF16)

    r = GATE_ROWS + FOX_WIDTH
    for ref, n in ((fvt_ref, FOX_WIDTH), (mvt_ref, d_model), (mot_ref, d_model),
                   (gat_ref, d_model), (gbt_ref, d_model)):
        ref[0] = seg(r, n).astype(BF16)
        r += n

    u = cb_ref[...] + cw_ref[CONV_WIDTH - 1:CONV_WIDTH, :] * mu
    for k in range(1, CONV_WIDTH):
        u = u + cw_ref[CONV_WIDTH - 1 - k:CONV_WIDTH - k, :] * pltpu.roll(mu, k, axis=0)
    head = jnp.concatenate([prev_ref[...], mu[:SUBLANES]], axis=0)
    u_head = cb_ref[...] + cw_ref[0:1, :] * head[SUBLANES - 3:2 * SUBLANES - 3]
    for j in range(1, CONV_WIDTH):
        off = SUBLANES - (CONV_WIDTH - 1) + j
        u_head = u_head + cw_ref[j:j + 1, :] * head[off:off + SUBLANES]
    u = jnp.concatenate([u_head, u[SUBLANES:]], axis=0)
    prev_ref[...] = mu[tm - SUBLANES:, :]
    u = _silu(u).astype(BF16)

    inner = d_model // MLSTM_HEADS
    qk = wmk_ref.shape[2]
    for hd in range(MLSTM_HEADS):
        uh = u[:, hd * inner:(hd + 1) * inner]
        mk_ref[:, hd * qk:(hd + 1) * qk] = jnp.dot(uh, wmk_ref[hd], preferred_element_type=F32).astype(BF16)
        qt = lax.dot_general(wmqt_ref[hd], uh, (((1,), (1,)), ((), ())), preferred_element_type=F32)
        mqt_ref[0, hd * qk:(hd + 1) * qk, :] = (qt * (qk ** -0.5)).astype(BF16)


def _slab_rows(rows, nsteps):
    steps = 1
    while steps <= nsteps:
        if nsteps % steps == 0 and (rows * steps) % nsteps == 0 and (rows * steps // nsteps) % BF16_ROWS == 0:
            return rows * steps // nsteps, steps
        steps *= 2
    return None


def _inproj(x2, mod3, wn, bn, wt, bt, conv_w, conv_b, w_mk, w_mqt, side, layer, *, bsz, seq, tm, chunk):
    tok, d = x2.shape
    nt = seq // tm
    nsteps = tok // tm
    qkw = w_mk.shape[0] * w_mk.shape[2]
    tmaj = lambda width: pl.BlockSpec((tm, width), lambda t: (t, 0))
    fmaj = lambda rows: pl.BlockSpec((1, rows, tm), lambda t: (t // nt, 0, t % nt))

    def slab_in(a):
        rows, steps = _slab_rows(a.shape[1], nsteps)
        return pl.BlockSpec((1, rows, a.shape[2]), lambda t: (layer, t // steps, 0))

    def slab_out(a):
        rows, steps = _slab_rows(a.shape[1], nsteps)
        return pl.BlockSpec((rows, a.shape[2]), lambda t: (t // steps, 0))

    out_shape = (
        jax.ShapeDtypeStruct((tok, FOX_WIDTH), BF16),
        jax.ShapeDtypeStruct((tok, qkw), BF16),
        jax.ShapeDtypeStruct((bsz, FOX_WIDTH, seq), BF16),
        jax.ShapeDtypeStruct((bsz, FOX_WIDTH, seq), BF16),
        jax.ShapeDtypeStruct((bsz, qkw, seq), BF16),
        jax.ShapeDtypeStruct((bsz, d, seq), BF16),
        jax.ShapeDtypeStruct((bsz, d, seq), BF16),
        jax.ShapeDtypeStruct((bsz, d, seq), BF16),
        jax.ShapeDtypeStruct((bsz, d, seq), BF16),
        jax.ShapeDtypeStruct((bsz, GATE_ROWS, seq), F32),
        jax.ShapeDtypeStruct((tok, GATE_ROWS), F32),
        jax.ShapeDtypeStruct((tok, FOX_WIDTH), BF16),
        jax.ShapeDtypeStruct((bsz, FOX_HEADS * FOX_AUG, seq), BF16),
    ) + tuple(jax.ShapeDtypeStruct(a.shape[1:], BF16) for a in side)
    out_specs = (tmaj(FOX_WIDTH), tmaj(qkw), fmaj(FOX_WIDTH), fmaj(FOX_WIDTH), fmaj(qkw),
                 fmaj(d), fmaj(d), fmaj(d), fmaj(d), fmaj(GATE_ROWS), tmaj(GATE_ROWS),
                 tmaj(FOX_WIDTH), fmaj(FOX_HEADS * FOX_AUG)) + tuple(slab_out(a) for a in side)
    kern = functools.partial(_inproj_kernel, tiles_per_seq=nt, chunk=chunk, d_model=d)
    return pl.pallas_call(
        kern,
        grid=(tok // tm,),
        in_specs=[
            pl.BlockSpec((tm, d), lambda t: (t, 0)),
            pl.BlockSpec((1,) + mod3.shape[1:], lambda t: (t // nt, 0, 0)),
            _const_spec(wn.shape), _const_spec(bn.shape), _const_spec(wt.shape), _const_spec(bt.shape),
            _const_spec(conv_w.shape), _const_spec(conv_b.shape), _const_spec(w_mk.shape), _const_spec(w_mqt.shape),
        ] + [slab_in(a) for a in side],
        out_specs=out_specs,
        out_shape=out_shape,
        scratch_shapes=[pltpu.VMEM((SUBLANES, d), F32), pltpu.VMEM((FOX_HEADS, 128), F32)],
        compiler_params=pltpu.CompilerParams(dimension_semantics=("arbitrary",), vmem_limit_bytes=VMEM_LIMIT_BYTES),
        name="inproj",
    )(x2, mod3, wn, bn, wt, bt, conv_w, conv_b, w_mk, w_mqt, *side)


def _fox_scores(qi, a, qt_ref, k_ref, kf_ref, qf_ref, blk):
    hd = FOX_HEAD_DIM
    half = blk // 2
    q0 = qi * blk
    main = q0 + half
    keys = q0 + blk
    qt2 = qt_ref[0, :, q0:keys]
    row = lax.broadcasted_iota(jnp.int32, qt2.shape, 0)
    tri_main = (lax.broadcasted_iota(jnp.int32, (half, blk), 0)
                <= lax.broadcasted_iota(jnp.int32, (half, blk), 1))
    kb = jnp.concatenate([k_ref[0, 0:keys, :], kf_ref[0, 0:keys, :]], axis=1)
    qm = jnp.where((row >= a * hd) & (row < (a + 1) * hd), qt2, jnp.zeros_like(qt2))
    qm = jnp.concatenate([qm, qf_ref[0, a * FOX_AUG:(a + 1) * FOX_AUG, q0:keys],
                          jnp.zeros((2 * hd - FOX_AUG, blk), BF16)], axis=0)
    s_main = jnp.dot(kb[:main], qm, preferred_element_type=F32)
    s_last = jnp.dot(kb[main:], qm[:, half:], preferred_element_type=F32)
    s_edge = jnp.where(tri_main, s_main[q0:], -jnp.inf)
    s_last = jnp.where(tri_main[:, :half], s_last, -jnp.inf)
    m = jnp.max(s_edge, axis=0, keepdims=True)
    if q0:
        m = jnp.maximum(m, jnp.max(s_main[:q0], axis=0, keepdims=True))
    m_hi = jnp.maximum(m[:, half:], jnp.max(s_last, axis=0, keepdims=True))
    m = jnp.concatenate([m[:, :half], m_hi], axis=1)
    return s_main[:q0] if q0 else None, s_edge, s_last, m, m_hi


def _fox_values(qi, a, scores, vt_ref, o_ref, blk):
    hd = FOX_HEAD_DIM
    half = blk // 2
    q0 = qi * blk
    main = q0 + half
    keys = q0 + blk
    s_top, s_edge, s_last, m, m_hi = scores
    ones = (lax.broadcasted_iota(jnp.int32, (VAUG_PAD, keys), 0) == 0).astype(BF16)
    vaug = jnp.concatenate([vt_ref[0, a * hd:(a + 1) * hd, 0:keys], ones], axis=0)
    acc = jnp.dot(vaug[:, q0:main], jnp.exp2(s_edge - m).astype(BF16), preferred_element_type=F32)
    if q0:
        acc = acc + jnp.dot(vaug[:, :q0], jnp.exp2(s_top - m).astype(BF16), preferred_element_type=F32)
    acc_hi = jnp.dot(vaug[:, main:], jnp.exp2(s_last - m_hi).astype(BF16), preferred_element_type=F32)
    acc = jnp.concatenate([acc[:, :half], acc[:, half:] + acc_hi], axis=1)
    o_ref[0, a * hd:(a + 1) * hd, q0:keys] = (acc[:hd] / acc[hd:hd + 1]).astype(BF16)


def _fox_kernel(qt_ref, k_ref, vt_ref, kf_ref, qf_ref, o_ref, *, blk):
    units = [(qi, a) for qi in reversed(range(k_ref.shape[1] // blk)) for a in range(2)]
    pending = []
    for qi, a in units:
        pending.append((qi, a, _fox_scores(qi, a, qt_ref, k_ref, kf_ref, qf_ref, blk)))
        if len(pending) > FOX_SCORE_LEAD:
            _fox_values(*pending.pop(0), vt_ref, o_ref, blk)
    for unit in pending:
        _fox_values(*unit, vt_ref, o_ref, blk)


def _fox_attention(fqt, fk3, fvt, kf3, qf, *, blk):
    bsz, width, seq = fqt.shape
    pairs = FOX_HEADS // 2
    pw = 2 * FOX_HEAD_DIM
    return pl.pallas_call(
        functools.partial(_fox_kernel, blk=blk),
        grid=(bsz, pairs),
        in_specs=[
            pl.BlockSpec((1, pw, seq), lambda b, p: (b, p, 0)),
            pl.BlockSpec((1, seq, pw), lambda b, p: (b, 0, p)),
            pl.BlockSpec((1, pw, seq), lambda b, p: (b, p, 0)),
            pl.BlockSpec((1, seq, pw), lambda b, p: (b, 0, p)),
            pl.BlockSpec((1, 2 * FOX_AUG, seq), lambda b, p: (b, p, 0)),
        ],
        out_specs=pl.BlockSpec((1, pw, seq), lambda b, p: (b, p, 0)),
        out_shape=jax.ShapeDtypeStruct((bsz, width, seq), BF16),
        compiler_params=pltpu.CompilerParams(dimension_semantics=("parallel", "parallel"),
                                             vmem_limit_bytes=VMEM_LIMIT_BYTES),
        name="fox_attention",
    )(fqt, fk3, fvt, kf3, qf)


def _mlstm_kernel(qt_ref, k_ref, vt_ref, ot_ref, rows_ref, cols_ref, g_ref, h_ref, state_ref, m_ref):
    c = pl.program_id(1)
    L = qt_ref.shape[2]
    qk = k_ref.shape[2] // MLSTM_HEADS
    dv = vt_ref.shape[1] // MLSTM_HEADS

    @pl.when(c == 0)
    def _():
        state_ref[...] = jnp.zeros_like(state_ref)
        m_ref[...] = jnp.zeros_like(m_ref)

    src = lax.broadcasted_iota(jnp.int32, (L, L), 0)
    dst = lax.broadcasted_iota(jnp.int32, (L, L), 1)
    causal = src <= dst
    ones_rows = (lax.broadcasted_iota(jnp.int32, (VAUG_PAD, L), 0) == 0).astype(BF16)

    for bi, hd in [(bi, hd) for bi in range(qt_ref.shape[0]) for hd in range(MLSTM_HEADS)]:
        qt = qt_ref[bi, hd * qk:(hd + 1) * qk, :]
        kk = k_ref[bi, :, hd * qk:(hd + 1) * qk]
        vaug = jnp.concatenate([vt_ref[bi, hd * dv:(hd + 1) * dv, :], ones_rows], axis=0)
        smax = rows_ref[bi, hd:hd + 1, :]
        b = rows_ref[bi, FOX_HEADS + hd:FOX_HEADS + hd + 1, :]
        ig = rows_ref[bi, FOX_HEADS + MLSTM_HEADS + hd:FOX_HEADS + MLSTM_HEADS + hd + 1, :]
        scol = cols_ref[bi, :, FOX_HEADS + hd:FOX_HEADS + hd + 1]
        m_prev = m_ref[bi, hd]
        state = state_ref[bi, hd]

        g = jnp.broadcast_to(b[:, L - 1:L], (1, L))
        inter = b + m_prev
        m_t = jnp.maximum(inter, b + smax)
        pmat = jnp.exp2(jnp.where(causal, (b - m_t) + scol, -jnp.inf))
        scores = jnp.dot(kk, qt, preferred_element_type=F32) * pmat
        w_inter = jnp.exp2(inter - m_t)
        qtw = (qt.astype(F32) * w_inter).astype(BF16)
        ht = jnp.dot(jnp.concatenate([state.astype(BF16), vaug], axis=1),
                     jnp.concatenate([qtw, scores.astype(BF16)], axis=0),
                     preferred_element_type=F32)
        den = ht[dv:dv + 1, :]
        hh = ht[:dv, :] * (1.0 / jnp.maximum(jnp.abs(den), jnp.exp2(-m_t)))
        mu = jnp.mean(hh, axis=0, keepdims=True)
        dlt = hh - mu
        var = jnp.mean(dlt * dlt, axis=0, keepdims=True)
        hn = dlt * lax.rsqrt(var + LN_EPS) * g_ref[hd * dv:(hd + 1) * dv, :]
        th = jnp.tanh(ot_ref[bi, hd * dv:(hd + 1) * dv, :].astype(F32))
        h_ref[bi, hd * dv:(hd + 1) * dv, :] = (hn + hn * th).astype(BF16)

        m_new = jnp.maximum(g + m_prev, g + jnp.broadcast_to(smax[:, L - 1:L], (1, L)))
        decay = jnp.exp2(g + m_prev - m_new)
        w = jnp.exp2((g - m_new) + (ig - b))
        upd = jnp.dot((vaug.astype(F32) * w).astype(BF16), kk, preferred_element_type=F32)
        state_ref[bi, hd] = decay[:, :qk] * state + upd
        m_ref[bi, hd] = m_new


def _mlstm(mqt, mk3, mvt, mot, rows, cols3, gcol, *, chunk, nb):
    bsz, qkw, seq = mqt.shape
    d = mvt.shape[1]
    dv = d // MLSTM_HEADS
    qk = qkw // MLSTM_HEADS
    fmaj = lambda r: pl.BlockSpec((nb, r, chunk), lambda b, c: (b, 0, c))
    return pl.pallas_call(
        _mlstm_kernel,
        grid=(bsz // nb, seq // chunk),
        in_specs=[
            fmaj(qkw),
            pl.BlockSpec((nb, chunk, qkw), lambda b, c: (b, c, 0)),
            fmaj(d), fmaj(d), fmaj(GATE_ROWS),
            pl.BlockSpec((nb, chunk, GATE_ROWS), lambda b, c: (b, c, 0)),
            _const_spec(gcol.shape),
        ],
        out_specs=fmaj(d),
        out_shape=jax.ShapeDtypeStruct((bsz, d, seq), BF16),
        scratch_shapes=[pltpu.VMEM((nb, MLSTM_HEADS, dv + VAUG_PAD, qk), F32),
                        pltpu.VMEM((nb, MLSTM_HEADS, 1, chunk), F32)],
        compiler_params=pltpu.CompilerParams(dimension_semantics=("parallel", "arbitrary"),
                                             vmem_limit_bytes=VMEM_LIMIT_BYTES),
        name="mlstm",
    )(mqt, mk3, mvt, mot, rows, cols3, gcol)


def _tail_kernel(x_ref, mod_ref, att_ref, hm_ref, ga_ref, gb_ref, wpa_ref, wpb_ref, wout_ref,
                 l1g_ref, l1b_ref, wfi_ref, wfd_ref, l2g_ref, l2b_ref, o_ref, *, d_ff, nsub):
    g1 = mod_ref[0, 2:3, :]
    sh2 = mod_ref[0, 3:4, :]
    sc2 = mod_ref[0, 4:5, :]
    g2 = mod_ref[0, 5:6, :]
    sub = x_ref.shape[0] // nsub
    sl = [slice(k * sub, (k + 1) * sub) for k in range(nsub)]
    ab = [(jnp.dot(wpa_ref[...], att_ref[0, :, s], preferred_element_type=F32),
           jnp.dot(wpb_ref[...], hm_ref[0, :, s], preferred_element_type=F32)) for s in sl]
    yt = [(jax.nn.sigmoid(ga_ref[0, :, s].astype(F32)) * a
           + jax.nn.sigmoid(gb_ref[0, :, s].astype(F32)) * b).astype(BF16) for s, (a, b) in zip(sl, ab)]
    z = [lax.dot_general(y, wout_ref[...], (((0,), (0,)), ((), ())), preferred_element_type=F32) for y in yt]
    x1 = [_ln(ALPHA * x_ref[s, :] + g1 * zz) * l1g_ref[...] + l1b_ref[...] for s, zz in zip(sl, z)]
    h2 = [(_ln(v) * (1.0 + sc2) + sh2).astype(BF16) for v in x1]
    gu = [jnp.dot(v, wfi_ref[...], preferred_element_type=F32) for v in h2]
    act = [(_silu(v[:, :d_ff]) * v[:, d_ff:]).astype(BF16) for v in gu]
    ff = [jnp.dot(v, wfd_ref[...], preferred_element_type=F32) for v in act]
    for s, v, f in zip(sl, x1, ff):
        o_ref[s, :] = _ln(ALPHA * v + g2 * f) * l2g_ref[...] + l2b_ref[...]


def _tail(x2, mod3, att_t, hm_t, ga_t, gb_t, wpa_t, wpb_t, w_out, l1g, l1b, wfi, wfd, l2g, l2b, *, seq, tm):
    tok, d = x2.shape
    nt = seq // tm
    d_ff = wfd.shape[0]
    fmaj = lambda rows: pl.BlockSpec((1, rows, tm), lambda t: (t // nt, 0, t % nt))
    consts = (wpa_t, wpb_t, w_out, l1g, l1b, wfi, wfd, l2g, l2b)
    return pl.pallas_call(
        functools.partial(_tail_kernel, d_ff=d_ff, nsub=2),
        grid=(tok // tm,),
        in_specs=[
            pl.BlockSpec((tm, d), lambda t: (t, 0)),
            pl.BlockSpec((1,) + mod3.shape[1:], lambda t: (t // nt, 0, 0)),
            fmaj(att_t.shape[1]), fmaj(d), fmaj(d), fmaj(d),
        ] + [_const_spec(a.shape) for a in consts],
        out_specs=pl.BlockSpec((tm, d), lambda t: (t, 0)),
        out_shape=jax.ShapeDtypeStruct((tok, d), F32),
        compiler_params=pltpu.CompilerParams(dimension_semantics=("parallel",), vmem_limit_bytes=VMEM_LIMIT_BYTES),
        name="tail",
    )(x2, mod3, att_t, hm_t, ga_t, gb_t, *consts)


def _pick(n, pref):
    while n % pref:
        pref //= 2
    return pref


def kernel(x, c, w_ada, b_ada, w_in, b_in, conv_w, conv_b, w_mq, w_mk, mh_norm_g, w_pa, w_pb, w_out,
           ln1_g, ln1_b, w_ffn_in, w_ffn_down, ln2_g, ln2_b):
    bsz, seq, d = x.shape
    tok = bsz * seq
    chunk = _pick(seq, 256)
    fox_blk = _pick(seq, 512)
    tm_in = _pick(seq, 512)
    tm_tail = _pick(seq, 512)

    x2 = x.reshape(tok, d)
    for l in range(DEPTH):
        mod3 = _modulation(c, w_ada[l], b_ada[l]).reshape(bsz, 6, d)

        splits = (FOX_WIDTH, FOX_WIDTH, FOX_WIDTH, FOX_HEADS, d, d, MLSTM_HEADS, MLSTM_HEADS, d, d, d)
        offs = [0]
        for s in splits:
            offs.append(offs[-1] + s)
        col = lambda i: (w_in[l][:, offs[i]:offs[i + 1]], b_in[l][offs[i]:offs[i + 1]])
        (wfq, bfq), (wfk, bfk), (wfv, bfv), (wff, bff), (wmu, bmu), (wmv, bmv), (wmi, bmi), (wmf, bmf), \
            (wmo, bmo), (wga, bga), (wgb, bgb) = [col(i) for i in range(len(splits))]
        scale = FOX_HEAD_DIM ** -0.5 * LOG2E
        wn = jnp.concatenate([wfk, wmu], axis=1).astype(BF16)
        bn = jnp.concatenate([bfk, bmu])[None, :]
        wt = jnp.concatenate([wff, wmi, wmf, wfq * scale, wfv, wmv, wmo * 0.5, wga, wgb], axis=1).T.astype(BF16)
        bt = jnp.concatenate([bff, bmi, bmf, bfq * scale, bfv, bmv, bmo * 0.5, bga, bgb])[:, None]

        tail_w = (w_ffn_in, w_ffn_down, w_out)
        in_call = all(_slab_rows(a.shape[1], tok // tm_in) is not None for a in tail_w)
        fk, mk, fqt, fvt, mqt, mvt, mot, gat, gbt, rows, cols, kf, qf, *cast = _inproj(
            x2, mod3, wn, bn, wt, bt, conv_w[l], conv_b[l][None, :], w_mk[l].astype(BF16),
            jnp.swapaxes(w_mq[l], 1, 2).astype(BF16), tail_w if in_call else (), l,
            bsz=bsz, seq=seq, tm=tm_in, chunk=chunk)
        wfi, wfd, wo = cast if in_call else [a[l].astype(BF16) for a in tail_w]

        att_t = _fox_attention(fqt, fk.reshape(bsz, seq, FOX_WIDTH), fvt, kf.reshape(bsz, seq, FOX_WIDTH), qf,
                               blk=fox_blk)

        hm_t = _mlstm(mqt, mk.reshape(bsz, seq, -1), mvt, mot, rows, cols.reshape(bsz, seq, GATE_ROWS),
                      0.5 * mh_norm_g[l][:, None], chunk=chunk, nb=_pick(bsz, 8))

        x2 = _tail(x2, mod3, att_t, hm_t, gat, gbt, w_pa[l].T.astype(BF16), w_pb[l].T.astype(BF16),
                   wo, ln1_g[l][None, :], ln1_b[l][None, :], wfi, wfd, ln2_g[l][None, :], ln2_b[l][None, :],
                   seq=seq, tm=tm_tail)
    return x2.reshape(bsz, seq, d)
```

```python
import functools

import jax
import jax.numpy as jnp
from jax import lax
from jax.experimental import pallas as pl
from jax.experimental.pallas import tpu as pltpu

F32 = jnp.float32
BF16 = jnp.bfloat16

FOX_HEADS = 8
FOX_HEAD_DIM = 64
FOX_WIDTH = FOX_HEADS * FOX_HEAD_DIM
MLSTM_HEADS = 4
CONV_WIDTH = 4
LN_EPS = 1e-5
DEPTH = 1
ALPHA = (2.0 * DEPTH) ** 0.25
LOG2E = 1.4426950408889634

VMEM_LIMIT_BYTES = 56 * 1024 * 1024
SUBLANES = 8
BF16_ROWS = 16
GATE_ROWS = 16
VAUG_PAD = 16
FOX_AUG = 16
FOX_SCORE_LEAD = 3

def _const_spec(shape):
    nd = len(shape)
    return pl.BlockSpec(shape, lambda *_: (0,) * nd, pipeline_mode=pl.Buffered(1))


def _ln(v):
    mu = jnp.mean(v, axis=-1, keepdims=True)
    d = v - mu
    var = jnp.mean(d * d, axis=-1, keepdims=True)
    return d * lax.rsqrt(var + LN_EPS)


def _silu(v):
    return v * jax.nn.sigmoid(v)


def _mod_kernel(c_ref, w_ref, b_ref, o_ref):
    a = _silu(c_ref[...]).astype(BF16)
    o_ref[...] = jnp.dot(a, w_ref[...].astype(BF16), preferred_element_type=F32) + b_ref[...]


def _modulation(c, w_ada, b_ada):
    bsz, d = c.shape
    n = w_ada.shape[1]
    return pl.pallas_call(
        _mod_kernel,
        grid=(n // d,),
        in_specs=[
            pl.BlockSpec((bsz, d), lambda j: (0, 0)),
            pl.BlockSpec((d, d), lambda j: (0, j)),
            pl.BlockSpec((1, d), lambda j: (0, j)),
        ],
        out_specs=pl.BlockSpec((bsz, d), lambda j: (0, j)),
        out_shape=jax.ShapeDtypeStruct((bsz, n), F32),
        name="modulation",
    )(c, w_ada, b_ada.reshape(1, n))


def _prefix_scan_lanes(v, width, op, identity):
    pos = lax.broadcasted_iota(jnp.int32, v.shape, 1) & (width - 1)
    d = 1
    while d < width:
        v = op(v, jnp.where(pos >= d, pltpu.roll(v, d, axis=1), identity))
        d *= 2
    return v


INPROJ_INPUTS = 10
INPROJ_OUTPUTS = 13


def _inproj_kernel(*refs, tiles_per_seq, chunk, d_model):
    (x_ref, mod_ref, wn_ref, bn_ref, wt_ref, bt_ref, cw_ref, cb_ref, wmk_ref, wmqt_ref) = refs[:INPROJ_INPUTS]
    n_side = (len(refs) - INPROJ_INPUTS - INPROJ_OUTPUTS - 2) // 2
    side_in = refs[INPROJ_INPUTS:INPROJ_INPUTS + n_side]
    outs = refs[INPROJ_INPUTS + n_side:]
    (fk_ref, mk_ref, fqt_ref, fvt_ref, mqt_ref, mvt_ref, mot_ref, gat_ref, gbt_ref,
     rows_ref, cols_ref, kf_ref, qf_ref) = outs[:INPROJ_OUTPUTS]
    side_out = outs[INPROJ_OUTPUTS:INPROJ_OUTPUTS + n_side]
    prev_ref, fcarry_ref = outs[INPROJ_OUTPUTS + n_side:]

    for src, dst in zip(side_in, side_out):
        dst[...] = src[0].astype(BF16)

    tm = x_ref.shape[0]
    first = (pl.program_id(0) % tiles_per_seq) == 0
    sh1 = mod_ref[0, 0:1, :]
    sc1 = mod_ref[0, 1:2, :]
    @pl.when(first)
    def _():
        prev_ref[...] = jnp.zeros_like(prev_ref)
        fcarry_ref[...] = jnp.zeros_like(fcarry_ref)

    halves = [slice(k * (tm // 2), (k + 1) * (tm // 2)) for k in range(2)]
    h_parts, yn_parts = [], []
    for rows in halves:
        h_parts.append((_ln(x_ref[rows, :]) * (1.0 + sc1) + sh1).astype(BF16))
        yn_parts.append(jnp.dot(h_parts[-1], wn_ref[...], preferred_element_type=F32) + bn_ref[...])
    h = jnp.concatenate(h_parts, axis=0)
    yn = jnp.concatenate(yn_parts, axis=0)

    def seg(r0, n):
        y = lax.dot_general(wt_ref[r0:r0 + n, :], h, (((1,), (1,)), ((), ())), preferred_element_type=F32)
        return y + bt_ref[r0:r0 + n, :]

    fk_ref[...] = yn[:, :FOX_WIDTH].astype(BF16)
    mu = yn[:, FOX_WIDTH:]

    gq = seg(0, GATE_ROWS + FOX_WIDTH)
    fqt_ref[0] = gq[GATE_ROWS:].astype(BF16)
    g = gq[:GATE_ROWS]
    ls = jax.nn.log_sigmoid(g) * LOG2E
    fcum = _prefix_scan_lanes(ls[0:FOX_HEADS], tm, jnp.add, 0.0) + fcarry_ref[:, 0:1]
    fcarry_ref[...] = jnp.broadcast_to(fcum[:, tm - 1:tm], fcarry_ref.shape)
    bcum = _prefix_scan_lanes(ls[FOX_HEADS:], chunk, jnp.add, 0.0)[MLSTM_HEADS:]
    ig = g[FOX_HEADS:FOX_HEADS + MLSTM_HEADS] * LOG2E
    src = ig - bcum
    smax = _prefix_scan_lanes(jnp.concatenate([src, src], axis=0), chunk, jnp.maximum, -jnp.inf)[MLSTM_HEADS:]
    zeros4 = jnp.zeros_like(ig)
    rows_ref[0] = jnp.concatenate([smax, zeros4, bcum, ig], axis=0)
    cols_ref[...] = jnp.concatenate([zeros4, zeros4, src, zeros4], axis=0).T

    f_hi = fcum.astype(BF16).astype(F32)
    f_mid = (fcum - f_hi).astype(BF16).astype(F32)
    f_lo = (fcum - f_hi - f_mid).astype(BF16).astype(F32)
    j = lax.broadcasted_iota(jnp.int32, (FOX_AUG, tm), 0)
    for hd in range(FOX_HEADS):
        odd = hd % 2
        ones_at = (j >= 3 + 3 * odd) & (j < 6 + 3 * odd)
        blk16 = jnp.where(j == 0, f_hi[hd:hd + 1], jnp.where(j == 1, f_mid[hd:hd + 1], jnp.where(
            j == 2, f_lo[hd:hd + 1], jnp.where(ones_at, 1.0, 0.0))))
        qf_ref[0, hd * FOX_AUG:(hd + 1) * FOX_AUG, :] = blk16.astype(BF16)
    kf_ref[...] = jnp.zeros_like(kf_ref)
    for pr in range(FOX_HEADS // 2):
        ea, eb = 2 * pr, 2 * pr + 1
        blk16 = jnp.where(j < 3, 1.0, 0.0)
        for entry, term in ((3, f_hi[ea:ea + 1]), (4, f_mid[ea:ea + 1]), (5, f_lo[ea:ea + 1]),
                            (6, f_hi[eb:eb + 1]), (7, f_mid[eb:eb + 1]), (8, f_lo[eb:eb + 1])):
            blk16 = jnp.where(j == entry, -term, blk16)
        kf_ref[:, pr * 2 * FOX_HEAD_DIM:pr * 2 * FOX_HEAD_DIM + FOX_AUG] = blk16.T.astype(BF16)

    r = GATE_ROWS + FOX_WIDTH
    for ref, n in ((fvt_ref, FOX_WIDTH), (mvt_ref, d_model), (mot_ref, d_model),
                   (gat_ref, d_model), (gbt_ref, d_model)):
        ref[0] = seg(r, n).astype(BF16)
        r += n

    u = cb_ref[...] + cw_ref[CONV_WIDTH - 1:CONV_WIDTH, :] * mu
    for k in range(1, CONV_WIDTH):
        u = u + cw_ref[CONV_WIDTH - 1 - k:CONV_WIDTH - k, :] * pltpu.roll(mu, k, axis=0)
    head = jnp.concatenate([prev_ref[...], mu[:SUBLANES]], axis=0)
    u_head = cb_ref[...] + cw_ref[0:1, :] * head[SUBLANES - 3:2 * SUBLANES - 3]
    for j in range(1, CONV_WIDTH):
        off = SUBLANES - (CONV_WIDTH - 1) + j
        u_head = u_head + cw_ref[j:j + 1, :] * head[off:off + SUBLANES]
    u = jnp.concatenate([u_head, u[SUBLANES:]], axis=0)
    prev_ref[...] = mu[tm - SUBLANES:, :]
    u = _silu(u).astype(BF16)

    inner = d_model // MLSTM_HEADS
    qk = wmk_ref.shape[2]
    for hd in range(MLSTM_HEADS):
        uh = u[:, hd * inner:(hd + 1) * inner]
        mk_ref[:, hd * qk:(hd + 1) * qk] = jnp.dot(uh, wmk_ref[hd], preferred_element_type=F32).astype(BF16)
        qt = lax.dot_general(wmqt_ref[hd], uh, (((1,), (1,)), ((), ())), preferred_element_type=F32)
        mqt_ref[0, hd * qk:(hd + 1) * qk, :] = (qt * (qk ** -0.5)).astype(BF16)


def _slab_rows(rows, nsteps):
    steps = 1
    while steps <= nsteps:
        if nsteps % steps == 0 and (rows * steps) % nsteps == 0 and (rows * steps // nsteps) % BF16_ROWS == 0:
            return rows * steps // nsteps, steps
        steps *= 2
    return None


def _inproj(x2, mod3, wn, bn, wt, bt, conv_w, conv_b, w_mk, w_mqt, side, layer, *, bsz, seq, tm, chunk):
    tok, d = x2.shape
    nt = seq // tm
    nsteps = tok // tm
    qkw = w_mk.shape[0] * w_mk.shape[2]
    tmaj = lambda width: pl.BlockSpec((tm, width), lambda t: (t, 0))
    fmaj = lambda rows: pl.BlockSpec((1, rows, tm), lambda t: (t // nt, 0, t % nt))

    def slab_in(a):
        rows, steps = _slab_rows(a.shape[1], nsteps)
        return pl.BlockSpec((1, rows, a.shape[2]), lambda t: (layer, t // steps, 0))

    def slab_out(a):
        rows, steps = _slab_rows(a.shape[1], nsteps)
        return pl.BlockSpec((rows, a.shape[2]), lambda t: (t // steps, 0))

    out_shape = (
        jax.ShapeDtypeStruct((tok, FOX_WIDTH), BF16),
        jax.ShapeDtypeStruct((tok, qkw), BF16),
        jax.ShapeDtypeStruct((bsz, FOX_WIDTH, seq), BF16),
        jax.ShapeDtypeStruct((bsz, FOX_WIDTH, seq), BF16),
        jax.ShapeDtypeStruct((bsz, qkw, seq), BF16),
        jax.ShapeDtypeStruct((bsz, d, seq), BF16),
        jax.ShapeDtypeStruct((bsz, d, seq), BF16),
        jax.ShapeDtypeStruct((bsz, d, seq), BF16),
        jax.ShapeDtypeStruct((bsz, d, seq), BF16),
        jax.ShapeDtypeStruct((bsz, GATE_ROWS, seq), F32),
        jax.ShapeDtypeStruct((tok, GATE_ROWS), F32),
        jax.ShapeDtypeStruct((tok, FOX_WIDTH), BF16),
        jax.ShapeDtypeStruct((bsz, FOX_HEADS * FOX_AUG, seq), BF16),
    ) + tuple(jax.ShapeDtypeStruct(a.shape[1:], BF16) for a in side)
    out_specs = (tmaj(FOX_WIDTH), tmaj(qkw), fmaj(FOX_WIDTH), fmaj(FOX_WIDTH), fmaj(qkw),
                 fmaj(d), fmaj(d), fmaj(d), fmaj(d), fmaj(GATE_ROWS), tmaj(GATE_ROWS),
                 tmaj(FOX_WIDTH), fmaj(FOX_HEADS * FOX_AUG)) + tuple(slab_out(a) for a in side)
    kern = functools.partial(_inproj_kernel, tiles_per_seq=nt, chunk=chunk, d_model=d)
    return pl.pallas_call(
        kern,
        grid=(tok // tm,),
        in_specs=[
            pl.BlockSpec((tm, d), lambda t: (t, 0)),
            pl.BlockSpec((1,) + mod3.shape[1:], lambda t: (t // nt, 0, 0)),
            _const_spec(wn.shape), _const_spec(bn.shape), _const_spec(wt.shape), _const_spec(bt.shape),
            _const_spec(conv_w.shape), _const_spec(conv_b.shape), _const_spec(w_mk.shape), _const_spec(w_mqt.shape),
        ] + [slab_in(a) for a in side],
        out_specs=out_specs,
        out_shape=out_shape,
        scratch_shapes=[pltpu.VMEM((SUBLANES, d), F32), pltpu.VMEM((FOX_HEADS, 128), F32)],
        compiler_params=pltpu.CompilerParams(dimension_semantics=("arbitrary",), vmem_limit_bytes=VMEM_LIMIT_BYTES),
        name="inproj",
    )(x2, mod3, wn, bn, wt, bt, conv_w, conv_b, w_mk, w_mqt, *side)


def _fox_scores(qi, a, qt_ref, k_ref, kf_ref, qf_ref, blk):
    hd = FOX_HEAD_DIM
    half = blk // 2
    q0 = qi * blk
    main = q0 + half
    keys = q0 + blk
    qt2 = qt_ref[0, :, q0:keys]
    row = lax.broadcasted_iota(jnp.int32, qt2.shape, 0)
    tri_main = (lax.broadcasted_iota(jnp.int32, (half, blk), 0)
                <= lax.broadcasted_iota(jnp.int32, (half, blk), 1))
    kb = jnp.concatenate([k_ref[0, 0:keys, :], kf_ref[0, 0:keys, :]], axis=1)
    qm = jnp.where((row >= a * hd) & (row < (a + 1) * hd), qt2, jnp.zeros_like(qt2))
    qm = jnp.concatenate([qm, qf_ref[0, a * FOX_AUG:(a + 1) * FOX_AUG, q0:keys],
                          jnp.zeros((2 * hd - FOX_AUG, blk), BF16)], axis=0)
    s_main = jnp.dot(kb[:main], qm, preferred_element_type=F32)
    s_last = jnp.dot(kb[main:], qm[:, half:], preferred_element_type=F32)
    s_edge = jnp.where(tri_main, s_main[q0:], -jnp.inf)
    s_last = jnp.where(tri_main[:, :half], s_last, -jnp.inf)
    m = jnp.max(s_edge, axis=0, keepdims=True)
    if q0:
        m = jnp.maximum(m, jnp.max(s_main[:q0], axis=0, keepdims=True))
    m_hi = jnp.maximum(m[:, half:], jnp.max(s_last, axis=0, keepdims=True))
    m = jnp.concatenate([m[:, :half], m_hi], axis=1)
    return s_main[:q0] if q0 else None, s_edge, s_last, m, m_hi


def _fox_values(qi, a, scores, vt_ref, o_ref, blk):
    hd = FOX_HEAD_DIM
    half = blk // 2
    q0 = qi * blk
    main = q0 + half
    keys = q0 + blk
    s_top, s_edge, s_last, m, m_hi = scores
    ones = (lax.broadcasted_iota(jnp.int32, (VAUG_PAD, keys), 0) == 0).astype(BF16)
    vaug = jnp.concatenate([vt_ref[0, a * hd:(a + 1) * hd, 0:keys], ones], axis=0)
    acc = jnp.dot(vaug[:, q0:main], jnp.exp2(s_edge - m).astype(BF16), preferred_element_type=F32)
    if q0:
        acc = acc + jnp.dot(vaug[:, :q0], jnp.exp2(s_top - m).astype(BF16), preferred_element_type=F32)
    acc_hi = jnp.dot(vaug[:, main:], jnp.exp2(s_last - m_hi).astype(BF16), preferred_element_type=F32)
    acc = jnp.concatenate([acc[:, :half], acc[:, half:] + acc_hi], axis=1)
    o_ref[0, a * hd:(a + 1) * hd, q0:keys] = (acc[:hd] / acc[hd:hd + 1]).astype(BF16)


def _fox_kernel(qt_ref, k_ref, vt_ref, kf_ref, qf_ref, o_ref, *, blk):
    units = [(qi, a) for qi in reversed(range(k_ref.shape[1] // blk)) for a in range(2)]
    pending = []
    for qi, a in units:
        pending.append((qi, a, _fox_scores(qi, a, qt_ref, k_ref, kf_ref, qf_ref, blk)))
        if len(pending) > FOX_SCORE_LEAD:
            _fox_values(*pending.pop(0), vt_ref, o_ref, blk)
    for unit in pending:
        _fox_values(*unit, vt_ref, o_ref, blk)


def _fox_attention(fqt, fk3, fvt, kf3, qf, *, blk):
    bsz, width, seq = fqt.shape
    pairs = FOX_HEADS // 2
    pw = 2 * FOX_HEAD_DIM
    return pl.pallas_call(
        functools.partial(_fox_kernel, blk=blk),
        grid=(bsz, pairs),
        in_specs=[
            pl.BlockSpec((1, pw, seq), lambda b, p: (b, p, 0)),
            pl.BlockSpec((1, seq, pw), lambda b, p: (b, 0, p)),
            pl.BlockSpec((1, pw, seq), lambda b, p: (b, p, 0)),
            pl.BlockSpec((1, seq, pw), lambda b, p: (b, 0, p)),
            pl.BlockSpec((1, 2 * FOX_AUG, seq), lambda b, p: (b, p, 0)),
        ],
        out_specs=pl.BlockSpec((1, pw, seq), lambda b, p: (b, p, 0)),
        out_shape=jax.ShapeDtypeStruct((bsz, width, seq), BF16),
        compiler_params=pltpu.CompilerParams(dimension_semantics=("parallel", "parallel"),
                                             vmem_limit_bytes=VMEM_LIMIT_BYTES),
        name="fox_attention",
    )(fqt, fk3, fvt, kf3, qf)


def _mlstm_kernel(qt_ref, k_ref, vt_ref, ot_ref, rows_ref, cols_ref, g_ref, h_ref, state_ref, m_ref):
    c = pl.program_id(1)
    L = qt_ref.shape[2]
    qk = k_ref.shape[2] // MLSTM_HEADS
    dv = vt_ref.shape[1] // MLSTM_HEADS

    @pl.when(c == 0)
    def _():
        state_ref[...] = jnp.zeros_like(state_ref)
        m_ref[...] = jnp.zeros_like(m_ref)

    src = lax.broadcasted_iota(jnp.int32, (L, L), 0)
    dst = lax.broadcasted_iota(jnp.int32, (L, L), 1)
    causal = src <= dst
    ones_rows = (lax.broadcasted_iota(jnp.int32, (VAUG_PAD, L), 0) == 0).astype(BF16)

    for bi, hd in [(bi, hd) for bi in range(qt_ref.shape[0]) for hd in range(MLSTM_HEADS)]:
        qt = qt_ref[bi, hd * qk:(hd + 1) * qk, :]
        kk = k_ref[bi, :, hd * qk:(hd + 1) * qk]
        vaug = jnp.concatenate([vt_ref[bi, hd * dv:(hd + 1) * dv, :], ones_rows], axis=0)
        smax = rows_ref[bi, hd:hd + 1, :]
        b = rows_ref[bi, FOX_HEADS + hd:FOX_HEADS + hd + 1, :]
        ig = rows_ref[bi, FOX_HEADS + MLSTM_HEADS + hd:FOX_HEADS + MLSTM_HEADS + hd + 1, :]
        scol = cols_ref[bi, :, FOX_HEADS + hd:FOX_HEADS + hd + 1]
        m_prev = m_ref[bi, hd]
        state = state_ref[bi, hd]

        g = jnp.broadcast_to(b[:, L - 1:L], (1, L))
        inter = b + m_prev
        m_t = jnp.maximum(inter, b + smax)
        pmat = jnp.exp2(jnp.where(causal, (b - m_t) + scol, -jnp.inf))
        scores = jnp.dot(kk, qt, preferred_element_type=F32) * pmat
        w_inter = jnp.exp2(inter - m_t)
        qtw = (qt.astype(F32) * w_inter).astype(BF16)
        ht = jnp.dot(jnp.concatenate([state.astype(BF16), vaug], axis=1),
                     jnp.concatenate([qtw, scores.astype(BF16)], axis=0),
                     preferred_element_type=F32)
        den = ht[dv:dv + 1, :]
        hh = ht[:dv, :] * (1.0 / jnp.maximum(jnp.abs(den), jnp.exp2(-m_t)))
        mu = jnp.mean(hh, axis=0, keepdims=True)
        dlt = hh - mu
        var = jnp.mean(dlt * dlt, axis=0, keepdims=True)
        hn = dlt * lax.rsqrt(var + LN_EPS) * g_ref[hd * dv:(hd + 1) * dv, :]
        th = jnp.tanh(ot_ref[bi, hd * dv:(hd + 1) * dv, :].astype(F32))
        h_ref[bi, hd * dv:(hd + 1) * dv, :] = (hn + hn * th).astype(BF16)

        m_new = jnp.maximum(g + m_prev, g + jnp.broadcast_to(smax[:, L - 1:L], (1, L)))
        decay = jnp.exp2(g + m_prev - m_new)
        w = jnp.exp2((g - m_new) + (ig - b))
        upd = jnp.dot((vaug.astype(F32) * w).astype(BF16), kk, preferred_element_type=F32)
        state_ref[bi, hd] = decay[:, :qk] * state + upd
        m_ref[bi, hd] = m_new


def _mlstm(mqt, mk3, mvt, mot, rows, cols3, gcol, *, chunk, nb):
    bsz, qkw, seq = mqt.shape
    d = mvt.shape[1]
    dv = d // MLSTM_HEADS
    qk = qkw // MLSTM_HEADS
    fmaj = lambda r: pl.BlockSpec((nb, r, chunk), lambda b, c: (b, 0, c))
    return pl.pallas_call(
        _mlstm_kernel,
        grid=(bsz // nb, seq // chunk),
        in_specs=[
            fmaj(qkw),
            pl.BlockSpec((nb, chunk, qkw), lambda b, c: (b, c, 0)),
            fmaj(d), fmaj(d), fmaj(GATE_ROWS),
            pl.BlockSpec((nb, chunk, GATE_ROWS), lambda b, c: (b, c, 0)),
            _const_spec(gcol.shape),
        ],
        out_specs=fmaj(d),
        out_shape=jax.ShapeDtypeStruct((bsz, d, seq), BF16),
        scratch_shapes=[pltpu.VMEM((nb, MLSTM_HEADS, dv + VAUG_PAD, qk), F32),
                        pltpu.VMEM((nb, MLSTM_HEADS, 1, chunk), F32)],
        compiler_params=pltpu.CompilerParams(dimension_semantics=("parallel", "arbitrary"),
                                             vmem_limit_bytes=VMEM_LIMIT_BYTES),
        name="mlstm",
    )(mqt, mk3, mvt, mot, rows, cols3, gcol)


def _tail_kernel(x_ref, mod_ref, att_ref, hm_ref, ga_ref, gb_ref, wpa_ref, wpb_ref, wout_ref,
                 l1g_ref, l1b_ref, wfi_ref, wfd_ref, l2g_ref, l2b_ref, o_ref, *, d_ff, nsub):
    g1 = mod_ref[0, 2:3, :]
    sh2 = mod_ref[0, 3:4, :]
    sc2 = mod_ref[0, 4:5, :]
    g2 = mod_ref[0, 5:6, :]
    sub = x_ref.shape[0] // nsub
    sl = [slice(k * sub, (k + 1) * sub) for k in range(nsub)]
    tn = (((0,), (0,)), ((), ()))
    at = [lax.dot_general(wpa_ref[...], att_ref[0, :, s], tn, preferred_element_type=F32) for s in sl]
    bt = [lax.dot_general(wpb_ref[...], hm_ref[0, :, s], tn, preferred_element_type=F32) for s in sl]
    yt = [(jax.nn.sigmoid(ga_ref[0, :, s].astype(F32)) * a
           + jax.nn.sigmoid(gb_ref[0, :, s].astype(F32)) * b).astype(BF16) for s, a, b in zip(sl, at, bt)]
    z = [lax.dot_general(y, wout_ref[...], (((0,), (0,)), ((), ())), preferred_element_type=F32) for y in yt]
    x1 = [_ln(ALPHA * x_ref[s, :] + g1 * zz) * l1g_ref[...] + l1b_ref[...] for s, zz in zip(sl, z)]
    h2 = [(_ln(v) * (1.0 + sc2) + sh2).astype(BF16) for v in x1]
    gu = [jnp.dot(v, wfi_ref[...], preferred_element_type=F32) for v in h2]
    act = [(_silu(v[:, :d_ff]) * v[:, d_ff:]).astype(BF16) for v in gu]
    ff = [jnp.dot(v, wfd_ref[...], preferred_element_type=F32) for v in act]
    for s, v, f in zip(sl, x1, ff):
        o_ref[s, :] = _ln(ALPHA * v + g2 * f) * l2g_ref[...] + l2b_ref[...]


def _tail(x2, mod3, att_t, hm_t, ga_t, gb_t, wpa_t, wpb_t, w_out, l1g, l1b, wfi, wfd, l2g, l2b, *, seq, tm):
    tok, d = x2.shape
    nt = seq // tm
    d_ff = wfd.shape[0]
    fmaj = lambda rows: pl.BlockSpec((1, rows, tm), lambda t: (t // nt, 0, t % nt))
    consts = (wpa_t, wpb_t, w_out, l1g, l1b, wfi, wfd, l2g, l2b)
    return pl.pallas_call(
        functools.partial(_tail_kernel, d_ff=d_ff, nsub=2),
        grid=(tok // tm,),
        in_specs=[
            pl.BlockSpec((tm, d), lambda t: (t, 0)),
            pl.BlockSpec((1,) + mod3.shape[1:], lambda t: (t // nt, 0, 0)),
            fmaj(att_t.shape[1]), fmaj(d), fmaj(d), fmaj(d),
        ] + [_const_spec(a.shape) for a in consts],
        out_specs=pl.BlockSpec((tm, d), lambda t: (t, 0)),
        out_shape=jax.ShapeDtypeStruct((tok, d), F32),
        compiler_params=pltpu.CompilerParams(dimension_semantics=("parallel",), vmem_limit_bytes=VMEM_LIMIT_BYTES),
        name="tail",
    )(x2, mod3, att_t, hm_t, ga_t, gb_t, *consts)


def _pick(n, pref):
    while n % pref:
        pref //= 2
    return pref


def kernel(x, c, w_ada, b_ada, w_in, b_in, conv_w, conv_b, w_mq, w_mk, mh_norm_g, w_pa, w_pb, w_out,
           ln1_g, ln1_b, w_ffn_in, w_ffn_down, ln2_g, ln2_b):
    bsz, seq, d = x.shape
    tok = bsz * seq
    chunk = _pick(seq, 256)
    fox_blk = _pick(seq, 512)
    tm_in = _pick(seq, 512)
    tm_tail = _pick(seq, 512)

    x2 = x.reshape(tok, d)
    for l in range(DEPTH):
        mod3 = _modulation(c, w_ada[l], b_ada[l]).reshape(bsz, 6, d)

        splits = (FOX_WIDTH, FOX_WIDTH, FOX_WIDTH, FOX_HEADS, d, d, MLSTM_HEADS, MLSTM_HEADS, d, d, d)
        offs = [0]
        for s in splits:
            offs.append(offs[-1] + s)
        col = lambda i: (w_in[l][:, offs[i]:offs[i + 1]], b_in[l][offs[i]:offs[i + 1]])
        (wfq, bfq), (wfk, bfk), (wfv, bfv), (wff, bff), (wmu, bmu), (wmv, bmv), (wmi, bmi), (wmf, bmf), \
            (wmo, bmo), (wga, bga), (wgb, bgb) = [col(i) for i in range(len(splits))]
        scale = FOX_HEAD_DIM ** -0.5 * LOG2E
        wn = jnp.concatenate([wfk, wmu], axis=1).astype(BF16)
        bn = jnp.concatenate([bfk, bmu])[None, :]
        wt = jnp.concatenate([wff, wmi, wmf, wfq * scale, wfv, wmv, wmo * 0.5, wga, wgb], axis=1).T.astype(BF16)
        bt = jnp.concatenate([bff, bmi, bmf, bfq * scale, bfv, bmv, bmo * 0.5, bga, bgb])[:, None]

        tail_w = (w_ffn_in, w_ffn_down, w_out, w_pa, w_pb)
        in_call = all(_slab_rows(a.shape[1], tok // tm_in) is not None for a in tail_w)
        fk, mk, fqt, fvt, mqt, mvt, mot, gat, gbt, rows, cols, kf, qf, *cast = _inproj(
            x2, mod3, wn, bn, wt, bt, conv_w[l], conv_b[l][None, :], w_mk[l].astype(BF16),
            jnp.swapaxes(w_mq[l], 1, 2).astype(BF16), tail_w if in_call else (), l,
            bsz=bsz, seq=seq, tm=tm_in, chunk=chunk)
        wfi, wfd, wo, wpa, wpb = cast if in_call else [a[l].astype(BF16) for a in tail_w]

        att_t = _fox_attention(fqt, fk.reshape(bsz, seq, FOX_WIDTH), fvt, kf.reshape(bsz, seq, FOX_WIDTH), qf,
                               blk=fox_blk)

        hm_t = _mlstm(mqt, mk.reshape(bsz, seq, -1), mvt, mot, rows, cols.reshape(bsz, seq, GATE_ROWS),
                      0.5 * mh_norm_g[l][:, None], chunk=chunk, nb=_pick(bsz, 8))

        x2 = _tail(x2, mod3, att_t, hm_t, gat, gbt, wpa, wpb,
                   wo, ln1_g[l][None, :], ln1_b[l][None, :], wfi, wfd, ln2_g[l][None, :], ln2_b[l][None, :],
                   seq=seq, tm=tm_tail)
    return x2.reshape(bsz, seq, d)
```
